```python
import math
import functools
import jax
import jax.numpy as jnp
from jax import lax
import numpy as np

D_MODEL = 2048
BATCH = 8
SEQ = 2048
DEPTH = 1
DEC_BATCH = 128
DEC_SEQ = 4
PAST_LEN = 2048
PAGE_SIZE = 128

DIFF_HEADS = 4
DIFF_HD = 128
DIFF_VD = 2 * DIFF_HD
GLA_HEADS = 4
GLA_DK = 128
GLA_DV = 256
GLA_GATE_RANK = 16
GLA_TAU = 16.0
GLA_CHUNK = 64
Q_BLOCK = 128
D_FF = 5504
NORM_EPS = 1e-6
DIFF_WIDTH = DIFF_HEADS * DIFF_VD
GLA_WIDTH = GLA_HEADS * GLA_DV
IN_SIZES = (DIFF_HEADS * 2 * DIFF_HD, DIFF_HEADS * 2 * DIFF_HD, DIFF_WIDTH,
            GLA_HEADS * GLA_DK, GLA_HEADS * GLA_DK, GLA_WIDTH, GLA_WIDTH, GLA_GATE_RANK)
N_IN = sum(IN_SIZES)

kernel_name = 'hymba_diffattn_gla_macaron_step'


def rmsnorm(x, g):
    xf = x.astype(jnp.float32)
    y = xf * lax.rsqrt(jnp.mean(xf * xf, axis=-1, keepdims=True) + NORM_EPS)
    return (y * g.astype(jnp.float32)).astype(x.dtype)


def swiglu(h, w_gate, w_up, w_down):
    return (jax.nn.silu(h @ w_gate) * (h @ w_up)) @ w_down


def alibi_slopes():
    return 2.0 ** (-8.0 * jnp.arange(1, DIFF_HEADS + 1, dtype=jnp.float32) / DIFF_HEADS)


def lambda_init(layer):
    return 0.8 - 0.6 * math.exp(-0.3 * layer)


def diff_attend(q, k, v, q_pos, k_pos, lam):
    s = jnp.einsum('bqhmd,bkhmd->bhmqk', q, k).astype(jnp.float32) * (DIFF_HD ** -0.5)
    dist = (q_pos[:, None] - k_pos[None, :]).astype(jnp.float32)
    s = s - alibi_slopes()[None, :, None, None, None] * dist
    s = jnp.where(dist >= 0, s, -jnp.inf)
    p = jax.nn.softmax(s, axis=-1)
    a = p[:, :, 0] - lam * p[:, :, 1]
    return jnp.einsum('bhqk,bkhe->bqhe', a.astype(v.dtype), v)


def attend_prompt(q, k, v, lam):
    b, l = q.shape[:2]
    nb = l // Q_BLOCK
    q_blocks = q.reshape(b, nb, Q_BLOCK, DIFF_HEADS, 2, DIFF_HD).swapaxes(0, 1)
    k_pos = jnp.arange(l)

    def block(args):
        q_blk, i = args
        return diff_attend(q_blk, k, v, i * Q_BLOCK + jnp.arange(Q_BLOCK), k_pos, lam)

    o = lax.map(block, (q_blocks, jnp.arange(nb)))
    return o.swapaxes(0, 1).reshape(b, l, DIFF_HEADS, DIFF_VD)


def attend_sample(q, k, v, lam, k_past, v_past):
    past = k_past.shape[1]
    l = q.shape[1]
    k_all = jnp.concatenate([k_past.astype(k.dtype), k], axis=1)
    v_all = jnp.concatenate([v_past.astype(v.dtype), v], axis=1)
    q_pos = past + jnp.arange(l)
    k_pos = jnp.arange(past + l)
    return diff_attend(q, k_all, v_all, q_pos, k_pos, lam)


def gla_recurrence(q, k, v, log_a, s0):
    b, l = q.shape[:2]
    c = math.gcd(l, GLA_CHUNK)
    n = l // c

    def chunks(t):
        return t.astype(jnp.float32).reshape(b, n, c, t.shape[2], t.shape[3]).swapaxes(0, 1)

    causal = jnp.tril(jnp.ones((c, c), dtype=bool))[None, :, :, None, None]

    def step(s, inp):
        qc, kc, vc, gc = inp
        cum = jnp.cumsum(gc, axis=1)
        rel = jnp.exp(jnp.where(causal, cum[:, :, None] - cum[:, None, :], -jnp.inf))
        att = jnp.einsum('bihd,bijhd,bjhd->bhij', qc, rel, kc)
        o = (jnp.einsum('bhij,bjhe->bihe', att, vc)
             + jnp.einsum('bihd,bhde->bihe', qc * jnp.exp(cum), s))
        last = cum[:, -1]
        s_new = (jnp.exp(last)[..., None] * s
                 + jnp.einsum('bjhd,bjhe->bhde', kc * jnp.exp(last[:, None] - cum), vc))
        return s_new, o

    s_fin, o = lax.scan(step, s0.astype(jnp.float32),
                        (chunks(q), chunks(k), chunks(v), chunks(log_a)))
    o = o.swapaxes(0, 1).reshape(b, l, GLA_HEADS, GLA_DV)
    return o.astype(v.dtype), s_fin


def hybrid_layer(x, layer, p, attend, gla_state):
    b, l, _ = x.shape
    x = x + 0.5 * swiglu(rmsnorm(x, p['norm_ffn1']), p['ffn1_w_gate'], p['ffn1_w_up'], p['ffn1_w_down'])
    h = rmsnorm(x, p['norm_mix'])
    z = h @ p['w_in']
    split_points = [int(o) for o in np.cumsum(IN_SIZES)[:-1]]
    dq, dk, dv, gq, gk, gv, gr, glr = jnp.split(z, split_points, axis=-1)
    dq = dq.reshape(b, l, DIFF_HEADS, 2, DIFF_HD)
    dk = dk.reshape(b, l, DIFF_HEADS, 2, DIFF_HD)
    dv = dv.reshape(b, l, DIFF_HEADS, DIFF_VD)
    gq = gq.reshape(b, l, GLA_HEADS, GLA_DK) * (GLA_DK ** -0.5)
    gk = gk.reshape(b, l, GLA_HEADS, GLA_DK)
    gv = gv.reshape(b, l, GLA_HEADS, GLA_DV)
    log_a = (jax.nn.log_sigmoid((glr @ p['gla_gate_w2'] + p['gla_gate_b']).astype(jnp.float32))
             / GLA_TAU).reshape(b, l, GLA_HEADS, GLA_DK)
    lam = (jnp.exp(jnp.sum(p['lambda_q1'].astype(jnp.float32) * p['lambda_k1'].astype(jnp.float32)))
           - jnp.exp(jnp.sum(p['lambda_q2'].astype(jnp.float32) * p['lambda_k2'].astype(jnp.float32)))
           + lambda_init(layer))
    o_diff = attend(dq, dk, dv, lam)
    o_diff = rmsnorm(o_diff, p['diff_norm']) * (1.0 - lambda_init(layer))
    o_gla, s_new = gla_recurrence(gq, gk, gv, log_a, gla_state)
    o_gla = rmsnorm(o_gla, p['gla_norm']).reshape(b, l, GLA_WIDTH) * jax.nn.silu(gr)
    mixed = jnp.concatenate([o_diff.reshape(b, l, DIFF_WIDTH), o_gla], axis=-1) @ p['w_out']
    x = x + mixed
    x = x + 0.5 * swiglu(rmsnorm(x, p['norm_ffn2']), p['ffn2_w_gate'], p['ffn2_w_up'], p['ffn2_w_down'])
    return x, dk, dv, s_new


def setup_inputs(seed: int = 0) -> dict:
    key = jax.random.key(seed)
    ks = jax.random.split(key, 32)
    f32 = jnp.float32
    n_pages = PAST_LEN // PAGE_SIZE
    n_pool = (DEC_BATCH * n_pages * 5) // 4

    def nrm(k, shape, scale):
        return jax.random.normal(k, shape, f32) * scale

    def gain(k, shape):
        return 1.0 + 0.01 * jax.random.normal(k, shape, f32)

    page_table = jax.random.permutation(ks[5], n_pool)[:DEC_BATCH * n_pages]
    page_table = page_table.reshape(DEC_BATCH, n_pages).astype(jnp.int32)
    return {
        'x_prompt': nrm(ks[0], (BATCH, SEQ, D_MODEL), 1.0),
        'x_sample': nrm(ks[1], (DEC_BATCH, DEC_SEQ, D_MODEL), 1.0),
        'cache_k': nrm(ks[2], (DEPTH, n_pool, PAGE_SIZE, DIFF_HEADS, 2, DIFF_HD), 1.0),
        'cache_v': nrm(ks[3], (DEPTH, n_pool, PAGE_SIZE, DIFF_HEADS, DIFF_VD), 1.0),
        'state_gla': nrm(ks[4], (DEPTH, DEC_BATCH, GLA_HEADS, GLA_DK, GLA_DV), 1.0),
        'page_table': page_table,
        'norm_ffn1': gain(ks[6], (DEPTH, D_MODEL)),
        'ffn1_w_gate': nrm(ks[7], (DEPTH, D_MODEL, D_FF), D_MODEL ** -0.5),
        'ffn1_w_up': nrm(ks[8], (DEPTH, D_MODEL, D_FF), D_MODEL ** -0.5),
        'ffn1_w_down': nrm(ks[9], (DEPTH, D_FF, D_MODEL), D_FF ** -0.5),
        'norm_mix': gain(ks[10], (DEPTH, D_MODEL)),
        'w_in': nrm(ks[11], (DEPTH, D_MODEL, N_IN), D_MODEL ** -0.5),
        'gla_gate_w2': nrm(ks[12], (DEPTH, GLA_GATE_RANK, GLA_HEADS * GLA_DK), GLA_GATE_RANK ** -0.5),
        'gla_gate_b': nrm(ks[13], (DEPTH, GLA_HEADS * GLA_DK), 0.1),
        'lambda_q1': nrm(ks[14], (DEPTH, DIFF_HD), 0.1),
        'lambda_k1': nrm(ks[15], (DEPTH, DIFF_HD), 0.1),
        'lambda_q2': nrm(ks[16], (DEPTH, DIFF_HD), 0.1),
        'lambda_k2': nrm(ks[17], (DEPTH, DIFF_HD), 0.1),
        'diff_norm': gain(ks[18], (DEPTH, DIFF_VD)),
        'gla_norm': gain(ks[19], (DEPTH, GLA_DV)),
        'w_out': nrm(ks[20], (DEPTH, DIFF_WIDTH + GLA_WIDTH, D_MODEL), (DIFF_WIDTH + GLA_WIDTH) ** -0.5),
        'norm_ffn2': gain(ks[21], (DEPTH, D_MODEL)),
        'ffn2_w_gate': nrm(ks[22], (DEPTH, D_MODEL, D_FF), D_MODEL ** -0.5),
        'ffn2_w_up': nrm(ks[23], (DEPTH, D_MODEL, D_FF), D_MODEL ** -0.5),
        'ffn2_w_down': nrm(ks[24], (DEPTH, D_FF, D_MODEL), D_FF ** -0.5),
        'norm_final': gain(ks[25], (D_MODEL,)),
    }


def reference(x_prompt, x_sample, cache_k, cache_v, state_gla, page_table,
              norm_ffn1, ffn1_w_gate, ffn1_w_up, ffn1_w_down, norm_mix, w_in,
              gla_gate_w2, gla_gate_b, lambda_q1, lambda_k1, lambda_q2, lambda_k2,
              diff_norm, gla_norm, w_out, norm_ffn2, ffn2_w_gate, ffn2_w_up, ffn2_w_down,
              norm_final):
    dec_b, n_pages = page_table.shape
    past = n_pages * PAGE_SIZE
    yp, ys = x_prompt, x_sample
    kp, vp, sp, kd, vd, sd = [], [], [], [], [], []
    for layer in range(DEPTH):
        p = {
            'norm_ffn1': norm_ffn1[layer], 'ffn1_w_gate': ffn1_w_gate[layer],
            'ffn1_w_up': ffn1_w_up[layer], 'ffn1_w_down': ffn1_w_down[layer],
            'norm_mix': norm_mix[layer], 'w_in': w_in[layer],
            'gla_gate_w2': gla_gate_w2[layer], 'gla_gate_b': gla_gate_b[layer],
            'lambda_q1': lambda_q1[layer], 'lambda_k1': lambda_k1[layer],
            'lambda_q2': lambda_q2[layer], 'lambda_k2': lambda_k2[layer],
            'diff_norm': diff_norm[layer], 'gla_norm': gla_norm[layer], 'w_out': w_out[layer],
            'norm_ffn2': norm_ffn2[layer], 'ffn2_w_gate': ffn2_w_gate[layer],
            'ffn2_w_up': ffn2_w_up[layer], 'ffn2_w_down': ffn2_w_down[layer],
        }
        s0 = jnp.zeros((x_prompt.shape[0], GLA_HEADS, GLA_DK, GLA_DV), jnp.float32)
        yp, k_new, v_new, s_new = hybrid_layer(yp, layer, p, attend_prompt, s0)
        kp.append(k_new)
        vp.append(v_new)
        sp.append(s_new.astype(state_gla.dtype))
        k_past = cache_k[layer][page_table].reshape(dec_b, past, DIFF_HEADS, 2, DIFF_HD)
        v_past = cache_v[layer][page_table].reshape(dec_b, past, DIFF_HEADS, DIFF_VD)
        attend = functools.partial(attend_sample, k_past=k_past, v_past=v_past)
        ys, k_new, v_new, s_new = hybrid_layer(ys, layer, p, attend, state_gla[layer])
        kd.append(k_new)
        vd.append(v_new)
        sd.append(s_new.astype(state_gla.dtype))
    y_prompt = rmsnorm(yp, norm_final)
    y_sample = rmsnorm(ys, norm_final)
    return (y_prompt, y_sample, jnp.stack(kp), jnp.stack(vp), jnp.stack(sp),
            jnp.stack(kd), jnp.stack(vd), jnp.stack(sd))
```

```python
import functools
import math

import jax
import jax.numpy as jnp
from jax import lax
from jax.experimental import pallas as pl
from jax.experimental.pallas import tpu as pltpu

F32 = jnp.float32
BF16 = jnp.bfloat16

NORM_EPS = 1e-6
GLA_TAU = 16.0
LANE = 128
VMEM_PHYSICAL = 64 * 2**20


def _lambda_init(layer):
    return 0.8 - 0.6 * math.exp(-0.3 * layer)


def _vmem_limit(nbytes):
    return int(min(nbytes * 5 // 4 + (4 << 20), VMEM_PHYSICAL - (8 << 20)))


def _rms(x, gain):
    return x * lax.rsqrt(jnp.mean(x * x, axis=-1, keepdims=True) + NORM_EPS) * gain


def _dot(a, b):
    return jnp.dot(a, b, preferred_element_type=F32)


def _dot_nt(a, b):
    return lax.dot_general(a, b, (((1,), (1,)), ((), ())), preferred_element_type=F32)


def _ffn_kernel(x_ref, g_ref, wg_ref, wu_ref, wd_ref, gf_ref, o_ref, h_ref, acc_ref, *, final_norm):
    j = pl.program_id(1)

    @pl.when(j == 0)
    def _():
        h_ref[...] = _rms(x_ref[...], g_ref[...]).astype(BF16)
        acc_ref[...] = jnp.zeros_like(acc_ref)

    h = h_ref[...]
    gate = _dot(h, wg_ref[...])
    up = _dot(h, wu_ref[...])
    act = (gate * jax.nn.sigmoid(gate)) * up
    acc_ref[...] += _dot(act.astype(BF16), wd_ref[...])

    @pl.when(j == pl.num_programs(1) - 1)
    def _():
        y = x_ref[...] + 0.5 * acc_ref[...]
        if final_norm:
            y = _rms(y, gf_ref[...])
        o_ref[...] = y


def _ffn(x, gain, wg, wu, wd, gain_final, *, final_norm, tm=512, tf=512):
    m, d = x.shape
    fpad = wg.shape[1]
    assert m % tm == 0 and fpad % tf == 0
    est = (2 * 2 * tm * d * 4
           + tm * d * (2 + 4)
           + 2 * 3 * d * tf * 2
           + 3 * tm * tf * 4)
    return pl.pallas_call(
        functools.partial(_ffn_kernel, final_norm=final_norm),
        name="ffn_final" if final_norm else "ffn",
        grid=(m // tm, fpad // tf),
        in_specs=[
            pl.BlockSpec((tm, d), lambda i, j: (i, 0)),
            pl.BlockSpec((1, d), lambda i, j: (0, 0)),
            pl.BlockSpec((d, tf), lambda i, j: (0, j)),
            pl.BlockSpec((d, tf), lambda i, j: (0, j)),
            pl.BlockSpec((tf, d), lambda i, j: (j, 0)),
            pl.BlockSpec((1, d), lambda i, j: (0, 0)),
        ],
        out_specs=pl.BlockSpec((tm, d), lambda i, j: (i, 0)),
        out_shape=jax.ShapeDtypeStruct((m, d), F32),
        scratch_shapes=[pltpu.VMEM((tm, d), BF16), pltpu.VMEM((tm, d), F32)],
        compiler_params=pltpu.CompilerParams(
            dimension_semantics=("parallel", "arbitrary"),
            vmem_limit_bytes=_vmem_limit(est)),
    )(x, gain, wg, wu, wd, gain_final)


def _inproj_kernel(x_ref, g_ref, w_ref, z_ref, h_ref):
    @pl.when(pl.program_id(1) == 0)
    def _():
        h_ref[...] = _rms(x_ref[...], g_ref[...]).astype(BF16)

    z_ref[...] = _dot(h_ref[...], w_ref[...])


def _inproj(x, gain, w, *, tm=512, tn=896):
    m, d = x.shape
    n = w.shape[1]
    assert m % tm == 0 and n % tn == 0
    est = 2 * tm * d * 4 + tm * d * 2 + 2 * d * tn * 2 + 2 * tm * tn * 4
    return pl.pallas_call(
        _inproj_kernel,
        name="in_proj",
        grid=(m // tm, n // tn),
        in_specs=[
            pl.BlockSpec((tm, d), lambda i, j: (i, 0)),
            pl.BlockSpec((1, d), lambda i, j: (0, 0)),
            pl.BlockSpec((d, tn), lambda i, j: (0, j)),
        ],
        out_specs=pl.BlockSpec((tm, tn), lambda i, j: (i, j)),
        out_shape=jax.ShapeDtypeStruct((m, n), F32),
        scratch_shapes=[pltpu.VMEM((tm, d), BF16)],
        compiler_params=pltpu.CompilerParams(
            dimension_semantics=("parallel", "arbitrary"),
            vmem_limit_bytes=_vmem_limit(est)),
    )(x, gain, w)


def _outproj_kernel(x_ref, a_ref, b_ref, wa_ref, wb_ref, o_ref):
    o_ref[...] = x_ref[...] + _dot(a_ref[...], wa_ref[...]) + _dot(b_ref[...], wb_ref[...])


def _outproj(x, mix_a, mix_b, w_a, w_b, *, tm=512):
    m, d = x.shape
    ka, kb = mix_a.shape[1], mix_b.shape[1]
    assert m % tm == 0
    est = 2 * 2 * tm * d * 4 + 2 * tm * (ka + kb) * 2 + 2 * (ka + kb) * d * 2 + tm * d * 4
    return pl.pallas_call(
        _outproj_kernel,
        name="out_proj",
        grid=(m // tm,),
        in_specs=[
            pl.BlockSpec((tm, d), lambda i: (i, 0)),
            pl.BlockSpec((tm, ka), lambda i: (i, 0)),
            pl.BlockSpec((tm, kb), lambda i: (i, 0)),
            pl.BlockSpec((ka, d), lambda i: (0, 0)),
            pl.BlockSpec((kb, d), lambda i: (0, 0)),
        ],
        out_specs=pl.BlockSpec((tm, d), lambda i: (i, 0)),
        out_shape=jax.ShapeDtypeStruct((m, d), F32),
        compiler_params=pltpu.CompilerParams(
            dimension_semantics=("parallel",),
            vmem_limit_bytes=_vmem_limit(est)),
    )(x, mix_a, mix_b, w_a, w_b)


def _diff_lambda(lam_ref, lam_init):
    lp = lam_ref[...]
    e1 = jnp.exp(jnp.sum(lp[0:1] * lp[1:2], axis=-1, keepdims=True))
    e2 = jnp.exp(jnp.sum(lp[2:3] * lp[3:4], axis=-1, keepdims=True))
    return e1 - e2 + lam_init


def _alibi_slope(h, n_heads):
    slope = F32(0.0)
    for i in range(n_heads):
        slope = jnp.where(h == i, F32(2.0 ** (-8.0 * (i + 1) / n_heads)), slope)
    return slope


def _online_softmax_step(s, v_bf16, m_ref, l_ref, acc_ref, idx):
    m_old = m_ref[idx][:, :1]
    m_new = jnp.maximum(m_old, jnp.max(s, axis=-1, keepdims=True))
    alpha = jnp.exp(m_old - m_new)
    p = jnp.exp(s - m_new)
    l_new = alpha * l_ref[idx][:, :1] + jnp.sum(p, axis=-1, keepdims=True)
    acc_ref[idx] = alpha * acc_ref[idx] + _dot(p.astype(BF16), v_bf16)
    m_ref[idx] = jnp.broadcast_to(m_new, m_ref.shape[1:])
    l_ref[idx] = jnp.broadcast_to(l_new, l_ref.shape[1:])


def _attn_prompt_kernel(q_ref, k_ref, v_ref, lam_ref, dn_ref, o_ref, m_ref, l_ref, acc_ref,
                        *, tq, hd, n_heads, lam_init):
    h = pl.program_id(1)
    qi = pl.program_id(2)
    scale = hd ** -0.5
    slope = _alibi_slope(h, n_heads)
    q = q_ref[...]
    qs = (q[:, :hd].astype(BF16), q[:, hd:].astype(BF16))
    row = lax.broadcasted_iota(jnp.int32, (tq, tq), 0)
    col = lax.broadcasted_iota(jnp.int32, (tq, tq), 1)
    d0 = (row - col).astype(F32)

    m_ref[...] = jnp.full_like(m_ref, -jnp.inf)
    l_ref[...] = jnp.zeros_like(l_ref)
    acc_ref[...] = jnp.zeros_like(acc_ref)

    def chunk(c, masked):
        ks = pl.multiple_of(c * tq, tq)
        k = k_ref[pl.ds(ks, tq), :]
        v = v_ref[pl.ds(ks, tq), :].astype(BF16)
        off = jnp.full((1, 1), (qi - c) * tq, jnp.int32).astype(F32)
        dist = d0 + off
        bias = slope * dist
        for m in range(2):
            s = _dot_nt(qs[m], k[:, m * hd:(m + 1) * hd].astype(BF16)) * scale - bias
            if masked:
                s = jnp.where(d0 >= 0, s, -jnp.inf)
            _online_softmax_step(s, v, m_ref, l_ref, acc_ref, m)

    def body(c, carry):
        chunk(c, False)
        return carry

    lax.fori_loop(0, qi, body, 0)
    chunk(qi, True)

    lam = _diff_lambda(lam_ref, lam_init)
    o = acc_ref[0] / l_ref[0][:, :1] - lam * (acc_ref[1] / l_ref[1][:, :1])
    o_ref[...] = (_rms(o, dn_ref[...]) * (1.0 - lam_init)).astype(o_ref.dtype)


def _attn_prompt(z, lam_p, diff_norm, *, batch, seq, n_heads, hd, vd, col_q, col_k, col_v,
                 lam_init, tq=256):
    assert vd == 2 * hd and seq % tq == 0
    nq = seq // tq
    cw = 2 * hd
    bq, bk, bv = col_q // cw, col_k // cw, col_v // cw
    est = (2 * tq * cw * 4 + 2 * 2 * seq * cw * 4 + 2 * tq * vd * 2
           + 2 * 2 * tq * LANE * 4 + 2 * tq * vd * 4 + 8 * tq * tq * 4)
    return pl.pallas_call(
        functools.partial(_attn_prompt_kernel, tq=tq, hd=hd, n_heads=n_heads, lam_init=lam_init),
        name="attn_prompt",
        grid=(batch, n_heads, nq),
        in_specs=[
            pl.BlockSpec((tq, cw), lambda b, h, i: (b * nq + i, bq + h)),
            pl.BlockSpec((seq, cw), lambda b, h, i: (b, bk + h)),
            pl.BlockSpec((seq, cw), lambda b, h, i: (b, bv + h)),
            pl.BlockSpec((4, hd), lambda b, h, i: (0, 0)),
            pl.BlockSpec((1, vd), lambda b, h, i: (0, 0)),
        ],
        out_specs=pl.BlockSpec((tq, vd), lambda b, h, i: (b * nq + i, h)),
        out_shape=jax.ShapeDtypeStruct((batch * seq, n_heads * vd), BF16),
        scratch_shapes=[pltpu.VMEM((2, tq, LANE), F32), pltpu.VMEM((2, tq, LANE), F32),
                        pltpu.VMEM((2, tq, vd), F32)],
        compiler_params=pltpu.CompilerParams(
            dimension_semantics=("parallel", "parallel", "arbitrary"),
            vmem_limit_bytes=_vmem_limit(est)),
    )(z, z, z, lam_p, diff_norm)


def _attn_sample_kernel(pt_ref, z_ref, kp_ref, vp_ref, lam_ref, dn_ref, o_ref,
                        qbd_ref, q2_ref, kn_ref, vn_ref, m_ref, l_ref, acc_ref,
                        *, n_pages, page, n_heads, hd, dec, col_q, col_k, col_v, lam_init):
    del pt_ref
    p = pl.program_id(1)
    cw = 2 * hd
    rows = 2 * dec
    scale = hd ** -0.5
    past = n_pages * page
    r_iota = lax.broadcasted_iota(jnp.int32, (rows, page), 0)
    j_iota = lax.broadcasted_iota(jnp.int32, (rows, page), 1)
    q_idx = jnp.where(r_iota >= dec, r_iota - dec, r_iota)

    @pl.when(p == 0)
    def _():
        m_ref[...] = jnp.full_like(m_ref, -jnp.inf)
        l_ref[...] = jnp.zeros_like(l_ref)
        acc_ref[...] = jnp.zeros_like(acc_ref)
        rr = lax.broadcasted_iota(jnp.int32, (rows, cw), 0)
        cc = lax.broadcasted_iota(jnp.int32, (rows, cw), 1)
        keep = (rr < dec) == (cc < hd)
        for h in range(n_heads):
            qh = z_ref[0, :, col_q + h * cw: col_q + (h + 1) * cw]
            q2_ref[0:dec, :] = qh
            q2_ref[dec:rows, :] = qh
            qbd_ref[h] = jnp.where(keep, q2_ref[...], 0.0).astype(BF16)

    def attend(kpage, vpage, dist, valid):
        for h in range(n_heads):
            slope = 2.0 ** (-8.0 * (h + 1) / n_heads)
            kh = kpage[:, h * cw:(h + 1) * cw].astype(BF16)
            vh = vpage[:, h * cw:(h + 1) * cw].astype(BF16)
            s = _dot_nt(qbd_ref[h], kh) * scale - slope * dist
            if valid is not None:
                s = jnp.where(valid, s, -jnp.inf)
            _online_softmax_step(s, vh, m_ref, l_ref, acc_ref, h)

    @pl.when(p < n_pages)
    def _():
        dist = (past + q_idx - (p * page + j_iota)).astype(F32)
        attend(kp_ref[0], vp_ref[0], dist, None)

    @pl.when(p == n_pages)
    def _():
        width = n_heads * cw
        kn_ref[...] = jnp.zeros_like(kn_ref)
        vn_ref[...] = jnp.zeros_like(vn_ref)
        kn_ref[0:dec, :] = z_ref[0, :, col_k:col_k + width]
        vn_ref[0:dec, :] = z_ref[0, :, col_v:col_v + width]
        dist = (q_idx - j_iota).astype(F32)
        attend(kn_ref[...], vn_ref[...], dist, j_iota <= q_idx)
        lam = _diff_lambda(lam_ref, lam_init)
        for h in range(n_heads):
            on = acc_ref[h] / l_ref[h][:, :1]
            o = on[:dec] - lam * on[dec:]
            o_ref[0, :, h * cw:(h + 1) * cw] = (
                _rms(o, dn_ref[...]) * (1.0 - lam_init)).astype(o_ref.dtype)


def _attn_sample(z3, cache_k, cache_v, page_table, layer, lam_p, diff_norm, *, n_heads, hd, vd,
                 col_q, col_k, col_v, lam_init):
    dec_b, dec, zw = z3.shape
    n_pages = page_table.shape[1]
    depth, n_pool, page = cache_k.shape[:3]
    assert vd == 2 * hd
    width = n_heads * 2 * hd
    ck = cache_k.reshape(depth * n_pool, page, width)
    cv = cache_v.reshape(depth * n_pool, page, width)
    pt = (page_table + layer * n_pool).reshape(-1).astype(jnp.int32)
    rows = 2 * dec

    def page_map(b, p, pt_ref):
        return (pt_ref[b * n_pages + jnp.minimum(p, n_pages - 1)], 0, 0)

    est = (2 * dec * zw * 4 + 2 * 2 * page * width * 4 + 2 * dec * width * 2
           + n_heads * rows * (2 * hd * 2 + 2 * LANE * 4 + vd * 4) + 4 * page * width * 4)
    grid_spec = pltpu.PrefetchScalarGridSpec(
        num_scalar_prefetch=1,
        grid=(dec_b, n_pages + 1),
        in_specs=[
            pl.BlockSpec((1, dec, zw), lambda b, p, pt_ref: (b, 0, 0)),
            pl.BlockSpec((1, page, width), page_map),
            pl.BlockSpec((1, page, width), page_map),
            pl.BlockSpec((4, hd), lambda b, p, pt_ref: (0, 0)),
            pl.BlockSpec((1, vd), lambda b, p, pt_ref: (0, 0)),
        ],
        out_specs=pl.BlockSpec((1, dec, n_heads * vd), lambda b, p, pt_ref: (b, 0, 0)),
        scratch_shapes=[pltpu.VMEM((n_heads, rows, 2 * hd), BF16),
                        pltpu.VMEM((rows, 2 * hd), F32),
                        pltpu.VMEM((page, width), F32),
                        pltpu.VMEM((page, width), F32),
                        pltpu.VMEM((n_heads, rows, LANE), F32),
                        pltpu.VMEM((n_heads, rows, LANE), F32),
                        pltpu.VMEM((n_heads, rows, vd), F32)],
    )
    return pl.pallas_call(
        functools.partial(_attn_sample_kernel, n_pages=n_pages, page=page, n_heads=n_heads, hd=hd,
                          dec=dec, col_q=col_q, col_k=col_k, col_v=col_v, lam_init=lam_init),
        name="attn_sample",
        grid_spec=grid_spec,
        out_shape=jax.ShapeDtypeStruct((dec_b, dec, n_heads * vd), BF16),
        compiler_params=pltpu.CompilerParams(
            dimension_semantics=("parallel", "arbitrary"),
            vmem_limit_bytes=_vmem_limit(est)),
    )(pt, z3, ck, cv, lam_p, diff_norm)


def _log_decay(glr, w2_ref, b_ref):
    x = _dot(glr.astype(BF16), w2_ref[...]) + b_ref[...]
    return (jnp.minimum(x, 0.0) - jnp.log1p(jnp.exp(-jnp.abs(x)))) / GLA_TAU


def _split_bf16(x):
    hi = x.astype(BF16)
    r1 = x - hi.astype(F32)
    mid = r1.astype(BF16)
    lo = (r1 - mid.astype(F32)).astype(BF16)
    return hi, mid, lo


def _gla_out(o, gn_ref, gr):
    return _rms(o, gn_ref[...]) * (gr * jax.nn.sigmoid(gr))


def _gla_prompt_kernel(q_ref, k_ref, v_ref, gr_ref, glr_ref, w2_ref, b_ref, gn_ref,
                       o_ref, s_ref, st_ref, *, chunk, dk):
    c = pl.program_id(2)

    @pl.when(c == 0)
    def _():
        st_ref[...] = jnp.zeros_like(st_ref)

    log_a = _log_decay(glr_ref[...], w2_ref, b_ref)
    row = lax.broadcasted_iota(jnp.int32, (chunk, chunk), 0)
    col = lax.broadcasted_iota(jnp.int32, (chunk, chunk), 1)
    causal = row >= col
    tri = jnp.where(causal, 1.0, 0.0).astype(BF16)
    hi, mid, lo = _split_bf16(log_a)
    cum = _dot(tri, hi) + _dot(tri, mid) + _dot(tri, lo)
    last = cum[chunk - 1:chunk, :]

    q = q_ref[...] * (dk ** -0.5)
    k = k_ref[...]
    v = v_ref[...]
    v16 = v.astype(BF16)
    st = st_ref[...]

    qt = (q * jnp.exp(cum)).astype(BF16)
    kt = (k * jnp.exp(-cum)).astype(BF16)
    att = jnp.where(causal, _dot_nt(qt, kt), 0.0)
    o = _dot(att.astype(BF16), v16) + _dot_nt(qt, st.astype(BF16))

    kd = (k * jnp.exp(last - cum)).astype(BF16)
    st_new = st * jnp.exp(last) + _dot(v.T.astype(BF16), kd)
    st_ref[...] = st_new

    o_ref[...] = _gla_out(o, gn_ref, gr_ref[...]).astype(o_ref.dtype)

    @pl.when(c == pl.num_programs(2) - 1)
    def _():
        s_ref[0, 0] = st_new.T


def _gla_prompt(z, w2p, bias, gla_norm, *, batch, seq, n_heads, dk, dv, col_q, col_k, col_v,
                col_gr, col_glr, chunk=256):
    assert seq % chunk == 0
    nc = seq // chunk
    bq, bk, bv, bg, bl = col_q // dk, col_k // dk, col_v // dv, col_gr // dv, col_glr // LANE
    est = 2 * chunk * (2 * dk + 2 * dv + LANE) * 4 + 2 * chunk * dv * 2 + 3 * dk * dv * 4 \
        + 6 * chunk * chunk * 4 + 12 * chunk * dv * 4
    return pl.pallas_call(
        functools.partial(_gla_prompt_kernel, chunk=chunk, dk=dk),
        name="gla_prompt",
        grid=(batch, n_heads, nc),
        in_specs=[
            pl.BlockSpec((chunk, dk), lambda b, h, c: (b * nc + c, bq + h)),
            pl.BlockSpec((chunk, dk), lambda b, h, c: (b * nc + c, bk + h)),
            pl.BlockSpec((chunk, dv), lambda b, h, c: (b * nc + c, bv + h)),
            pl.BlockSpec((chunk, dv), lambda b, h, c: (b * nc + c, bg + h)),
            pl.BlockSpec((chunk, LANE), lambda b, h, c: (b * nc + c, bl)),
            pl.BlockSpec((LANE, dk), lambda b, h, c: (0, h)),
            pl.BlockSpec((1, dk), lambda b, h, c: (0, h)),
            pl.BlockSpec((1, dv), lambda b, h, c: (0, 0)),
        ],
        out_specs=[
            pl.BlockSpec((chunk, dv), lambda b, h, c: (b * nc + c, h)),
            pl.BlockSpec((1, 1, dk, dv), lambda b, h, c: (b, h, 0, 0)),
        ],
        out_shape=[jax.ShapeDtypeStruct((batch * seq, n_heads * dv), BF16),
                   jax.ShapeDtypeStruct((batch, n_heads, dk, dv), F32)],
        scratch_shapes=[pltpu.VMEM((dv, dk), F32)],
        compiler_params=pltpu.CompilerParams(
            dimension_semantics=("parallel", "parallel", "arbitrary"),
            vmem_limit_bytes=_vmem_limit(est)),
    )(z, z, z, z, z, w2p, bias, gla_norm)


def _gla_sample_kernel(z_ref, s0_ref, w2_ref, b_ref, gn_ref, o_ref, s_ref,
                       *, n_heads, dk, dv, dec, col_q, col_k, col_v, col_gr, col_glr):
    glr = z_ref[0, :, col_glr:col_glr + LANE]
    log_a_all = _log_decay(glr, w2_ref, b_ref)
    t_k = lax.broadcasted_iota(jnp.int32, (dec, dk), 0)
    t_v = lax.broadcasted_iota(jnp.int32, (dec, dv), 0)
    eye = (lax.broadcasted_iota(jnp.int32, (dk, dk), 0)
           == lax.broadcasted_iota(jnp.int32, (dk, dk), 1))

    def to_col(r):
        return jnp.sum(jnp.where(eye, jnp.broadcast_to(r, (dk, dk)), 0.0), axis=-1, keepdims=True)

    for h in range(n_heads):
        log_a = log_a_all[:, h * dk:(h + 1) * dk]
        cum = jnp.zeros((dec, dk), F32)
        for t in range(dec):
            cum = cum + jnp.where(t_k >= t, log_a[t:t + 1], 0.0)
        last = cum[dec - 1:dec]
        q = z_ref[0, :, col_q + h * dk: col_q + (h + 1) * dk] * (dk ** -0.5)
        k = z_ref[0, :, col_k + h * dk: col_k + (h + 1) * dk]
        v = z_ref[0, :, col_v + h * dv: col_v + (h + 1) * dv]
        gr = z_ref[0, :, col_gr + h * dv: col_gr + (h + 1) * dv]
        s0 = s0_ref[0, h]

        o = _dot((q * jnp.exp(cum)).astype(BF16), s0.astype(BF16))
        for t in range(dec):
            o_t = jnp.zeros((1, dv), F32)
            for j in range(t + 1):
                w = jnp.exp(cum[t:t + 1] - cum[j:j + 1])
                a = jnp.sum(q[t:t + 1] * k[j:j + 1] * w, axis=-1, keepdims=True)
                o_t = o_t + a * v[j:j + 1]
            o = o + jnp.where(t_v == t, o_t, 0.0)

        kd = k * jnp.exp(last - cum)
        s_new = s0 * to_col(jnp.exp(last))
        for t in range(dec):
            s_new = s_new + to_col(kd[t:t + 1]) * v[t:t + 1]
        s_ref[0, h] = s_new
        o_ref[0, :, h * dv:(h + 1) * dv] = _gla_out(o, gn_ref, gr).astype(o_ref.dtype)


def _gla_sample(z3, state, w2p, bias, gla_norm, *, n_heads, dk, dv, col_q, col_k, col_v,
                col_gr, col_glr):
    dec_b, dec, zw = z3.shape
    est = 2 * dec * zw * 4 + 2 * 2 * n_heads * dk * dv * 4 + LANE * n_heads * dk * 2 \
        + 8 * dk * dv * 4
    return pl.pallas_call(
        functools.partial(_gla_sample_kernel, n_heads=n_heads, dk=dk, dv=dv, dec=dec, col_q=col_q,
                          col_k=col_k, col_v=col_v, col_gr=col_gr, col_glr=col_glr),
        name="gla_sample",
        grid=(dec_b,),
        in_specs=[
            pl.BlockSpec((1, dec, zw), lambda b: (b, 0, 0)),
            pl.BlockSpec((1, n_heads, dk, dv), lambda b: (b, 0, 0, 0)),
            pl.BlockSpec((LANE, n_heads * dk), lambda b: (0, 0)),
            pl.BlockSpec((1, n_heads * dk), lambda b: (0, 0)),
            pl.BlockSpec((1, dv), lambda b: (0, 0)),
        ],
        out_specs=[
            pl.BlockSpec((1, dec, n_heads * dv), lambda b: (b, 0, 0)),
            pl.BlockSpec((1, n_heads, dk, dv), lambda b: (b, 0, 0, 0)),
        ],
        out_shape=[jax.ShapeDtypeStruct((dec_b, dec, n_heads * dv), BF16),
                   jax.ShapeDtypeStruct((dec_b, n_heads, dk, dv), F32)],
        compiler_params=pltpu.CompilerParams(
            dimension_semantics=("parallel",),
            vmem_limit_bytes=_vmem_limit(est)),
    )(z3, state, w2p, bias, gla_norm)


def _pad_cols(w, n):
    return jnp.pad(w, ((0, 0), (0, n - w.shape[1])))


def _round_up(x, m):
    return (x + m - 1) // m * m


def kernel(x_prompt, x_sample, cache_k, cache_v, state_gla, page_table, norm_ffn1, ffn1_w_gate, ffn1_w_up, ffn1_w_down, norm_mix, w_in, gla_gate_w2, gla_gate_b, lambda_q1, lambda_k1, lambda_q2, lambda_k2, diff_norm, gla_norm, w_out, norm_ffn2, ffn2_w_gate, ffn2_w_up, ffn2_w_down, norm_final):
    batch, seq, d = x_prompt.shape
    dec_b, dec, _ = x_sample.shape
    depth = norm_ffn1.shape[0]
    n_heads, hd = cache_k.shape[3], cache_k.shape[5]
    vd = cache_v.shape[4]
    g_heads, dk, dv = state_gla.shape[2:]
    rank = gla_gate_w2.shape[1]
    d_ff = ffn1_w_gate.shape[2]

    w_qk = n_heads * 2 * hd
    sizes = (w_qk, w_qk, n_heads * vd, g_heads * dk, g_heads * dk, g_heads * dv, g_heads * dv, rank)
    cols = [0]
    for s in sizes:
        cols.append(cols[-1] + s)
    col_dq, col_dk, col_dv, col_gq, col_gk, col_gv, col_gr, col_glr = cols[:8]
    assert w_in.shape[2] == cols[8] and rank <= LANE and col_glr % LANE == 0
    n_in_pad = _round_up(col_glr + LANE, 896)
    ff_pad = _round_up(d_ff, 512)
    diff_w = n_heads * vd

    yp = x_prompt.reshape(batch * seq, d)
    ys = x_sample.reshape(dec_b * dec, d)
    nf = norm_final.reshape(1, d)
    outs = [[] for _ in range(6)]

    for layer in range(depth):
        lam_init = _lambda_init(layer)
        ffn_w = []
        for wg, wu, wd in ((ffn1_w_gate, ffn1_w_up, ffn1_w_down), (ffn2_w_gate, ffn2_w_up, ffn2_w_down)):
            ffn_w.append((_pad_cols(wg[layer], ff_pad).astype(BF16),
                          _pad_cols(wu[layer], ff_pad).astype(BF16),
                          jnp.pad(wd[layer], ((0, ff_pad - d_ff), (0, 0))).astype(BF16)))
        w_in_p = _pad_cols(w_in[layer], n_in_pad).astype(BF16)
        w2p = jnp.pad(gla_gate_w2[layer], ((0, LANE - rank), (0, 0))).astype(BF16)
        gate_b = gla_gate_b[layer].reshape(1, -1)
        lam_p = jnp.stack([lambda_q1[layer], lambda_k1[layer], lambda_q2[layer], lambda_k2[layer]])
        dn = diff_norm[layer].reshape(1, vd)
        gn = gla_norm[layer].reshape(1, dv)
        wo = w_out[layer].astype(BF16)
        wo_a, wo_b = wo[:diff_w], wo[diff_w:]
        g1 = norm_ffn1[layer].reshape(1, d)
        gm = norm_mix[layer].reshape(1, d)
        g2 = norm_ffn2[layer].reshape(1, d)
        last = layer == depth - 1

        def pre(x):
            x = _ffn(x, g1, *ffn_w[0], nf, final_norm=False)
            return x, _inproj(x, gm, w_in_p)

        def post(x, mix_a, mix_b):
            x = _outproj(x, mix_a, mix_b, wo_a, wo_b)
            return _ffn(x, g2, *ffn_w[1], nf, final_norm=last)

        yp, zp = pre(yp)
        mix_a = _attn_prompt(zp, lam_p, dn, batch=batch, seq=seq, n_heads=n_heads, hd=hd, vd=vd,
                             col_q=col_dq, col_k=col_dk, col_v=col_dv, lam_init=lam_init)
        mix_b, s_p = _gla_prompt(zp, w2p, gate_b, gn, batch=batch, seq=seq, n_heads=g_heads, dk=dk,
                                 dv=dv, col_q=col_gq, col_k=col_gk, col_v=col_gv, col_gr=col_gr,
                                 col_glr=col_glr)
        yp = post(yp, mix_a, mix_b)
        outs[0].append(zp[:, col_dk:col_dk + w_qk].reshape(batch, seq, n_heads, 2, hd))
        outs[1].append(zp[:, col_dv:col_dv + diff_w].reshape(batch, seq, n_heads, vd))
        outs[2].append(s_p.astype(state_gla.dtype))

        ys, zs = pre(ys)
        zs3 = zs.reshape(dec_b, dec, n_in_pad)
        mix_a = _attn_sample(zs3, cache_k, cache_v, page_table, layer, lam_p, dn, n_heads=n_heads,
                             hd=hd, vd=vd, col_q=col_dq, col_k=col_dk, col_v=col_dv,
                             lam_init=lam_init)
        mix_b, s_s = _gla_sample(zs3, state_gla[layer], w2p, gate_b, gn, n_heads=g_heads, dk=dk,
                                 dv=dv, col_q=col_gq, col_k=col_gk, col_v=col_gv, col_gr=col_gr,
                                 col_glr=col_glr)
        ys = post(ys, mix_a.reshape(dec_b * dec, -1), mix_b.reshape(dec_b * dec, -1))
        outs[3].append(zs[:, col_dk:col_dk + w_qk].reshape(dec_b, dec, n_heads, 2, hd))
        outs[4].append(zs[:, col_dv:col_dv + diff_w].reshape(dec_b, dec, n_heads, vd))
        outs[5].append(s_s.astype(state_gla.dtype))

    if depth == 0:
        raise ValueError("depth must be positive")
    y_prompt = yp.reshape(batch, seq, d)
    y_sample = ys.reshape(dec_b, dec, d)
    return (y_prompt, y_sample) + tuple(jnp.stack(o) for o in outs)
```

```python
import functools
import math

import jax
import jax.numpy as jnp
from jax import lax
from jax.experimental import pallas as pl
from jax.experimental.pallas import tpu as pltpu

F32 = jnp.float32
BF16 = jnp.bfloat16

NORM_EPS = 1e-6
GLA_TAU = 16.0
LANE = 128
VMEM_PHYSICAL = 64 * 2**20


def _lambda_init(layer):
    return 0.8 - 0.6 * math.exp(-0.3 * layer)


def _vmem_limit(nbytes):
    return int(min(nbytes * 5 // 4 + (4 << 20), VMEM_PHYSICAL - (8 << 20)))


def _rms(x, gain):
    return x * lax.rsqrt(jnp.mean(x * x, axis=-1, keepdims=True) + NORM_EPS) * gain


def _dot(a, b):
    return jnp.dot(a, b, preferred_element_type=F32)


def _dot_nt(a, b):
    return lax.dot_general(a, b, (((1,), (1,)), ((), ())), preferred_element_type=F32)


def _ffn_kernel(x_ref, g_ref, wg_ref, wu_ref, wd_ref, gf_ref, o_ref, h_ref, acc_ref, *, final_norm):
    j = pl.program_id(1)

    @pl.when(j == 0)
    def _():
        h_ref[...] = _rms(x_ref[...], g_ref[...]).astype(BF16)
        acc_ref[...] = jnp.zeros_like(acc_ref)

    h = h_ref[...]
    gate = _dot(h, wg_ref[...])
    up = _dot(h, wu_ref[...])
    act = (gate * jax.nn.sigmoid(gate)) * up
    acc_ref[...] += _dot(act.astype(BF16), wd_ref[...])

    @pl.when(j == pl.num_programs(1) - 1)
    def _():
        y = x_ref[...] + 0.5 * acc_ref[...]
        if final_norm:
            y = _rms(y, gf_ref[...])
        o_ref[...] = y


def _ffn(x, gain, wg, wu, wd, gain_final, *, final_norm, tm=512, tf=512):
    m, d = x.shape
    fpad = wg.shape[1]
    assert m % tm == 0 and fpad % tf == 0
    est = (2 * 2 * tm * d * 4
           + tm * d * (2 + 4)
           + 2 * 3 * d * tf * 2
           + 3 * tm * tf * 4)
    return pl.pallas_call(
        functools.partial(_ffn_kernel, final_norm=final_norm),
        name="ffn_final" if final_norm else "ffn",
        grid=(m // tm, fpad // tf),
        in_specs=[
            pl.BlockSpec((tm, d), lambda i, j: (i, 0)),
            pl.BlockSpec((1, d), lambda i, j: (0, 0)),
            pl.BlockSpec((d, tf), lambda i, j: (0, j)),
            pl.BlockSpec((d, tf), lambda i, j: (0, j)),
            pl.BlockSpec((tf, d), lambda i, j: (j, 0)),
            pl.BlockSpec((1, d), lambda i, j: (0, 0)),
        ],
        out_specs=pl.BlockSpec((tm, d), lambda i, j: (i, 0)),
        out_shape=jax.ShapeDtypeStruct((m, d), F32),
        scratch_shapes=[pltpu.VMEM((tm, d), BF16), pltpu.VMEM((tm, d), F32)],
        compiler_params=pltpu.CompilerParams(
            dimension_semantics=("parallel", "arbitrary"),
            vmem_limit_bytes=_vmem_limit(est)),
    )(x, gain, wg, wu, wd, gain_final)


def _inproj_kernel(x_ref, g_ref, w_ref, z_ref, h_ref):
    @pl.when(pl.program_id(1) == 0)
    def _():
        h_ref[...] = _rms(x_ref[...], g_ref[...]).astype(BF16)

    z_ref[...] = _dot(h_ref[...], w_ref[...])


def _inproj(x, gain, w, *, tm=512, tn=896):
    m, d = x.shape
    n = w.shape[1]
    assert m % tm == 0 and n % tn == 0
    est = 2 * tm * d * 4 + tm * d * 2 + 2 * d * tn * 2 + 2 * tm * tn * 4
    return pl.pallas_call(
        _inproj_kernel,
        name="in_proj",
        grid=(m // tm, n // tn),
        in_specs=[
            pl.BlockSpec((tm, d), lambda i, j: (i, 0)),
            pl.BlockSpec((1, d), lambda i, j: (0, 0)),
            pl.BlockSpec((d, tn), lambda i, j: (0, j)),
        ],
        out_specs=pl.BlockSpec((tm, tn), lambda i, j: (i, j)),
        out_shape=jax.ShapeDtypeStruct((m, n), F32),
        scratch_shapes=[pltpu.VMEM((tm, d), BF16)],
        compiler_params=pltpu.CompilerParams(
            dimension_semantics=("parallel", "arbitrary"),
            vmem_limit_bytes=_vmem_limit(est)),
    )(x, gain, w)


def _outproj_kernel(x_ref, a_ref, b_ref, wa_ref, wb_ref, o_ref):
    o_ref[...] = x_ref[...] + _dot(a_ref[...], wa_ref[...]) + _dot(b_ref[...], wb_ref[...])


def _outproj(x, mix_a, mix_b, w_a, w_b, *, tm=512):
    m, d = x.shape
    ka, kb = mix_a.shape[1], mix_b.shape[1]
    assert m % tm == 0
    est = 2 * 2 * tm * d * 4 + 2 * tm * (ka + kb) * 2 + 2 * (ka + kb) * d * 2 + tm * d * 4
    return pl.pallas_call(
        _outproj_kernel,
        name="out_proj",
        grid=(m // tm,),
        in_specs=[
            pl.BlockSpec((tm, d), lambda i: (i, 0)),
            pl.BlockSpec((tm, ka), lambda i: (i, 0)),
            pl.BlockSpec((tm, kb), lambda i: (i, 0)),
            pl.BlockSpec((ka, d), lambda i: (0, 0)),
            pl.BlockSpec((kb, d), lambda i: (0, 0)),
        ],
        out_specs=pl.BlockSpec((tm, d), lambda i: (i, 0)),
        out_shape=jax.ShapeDtypeStruct((m, d), F32),
        compiler_params=pltpu.CompilerParams(
            dimension_semantics=("parallel",),
            vmem_limit_bytes=_vmem_limit(est)),
    )(x, mix_a, mix_b, w_a, w_b)


def _diff_lambda(lam_ref, lam_init):
    lp = lam_ref[...]
    e1 = jnp.exp(jnp.sum(lp[0:1] * lp[1:2], axis=-1, keepdims=True))
    e2 = jnp.exp(jnp.sum(lp[2:3] * lp[3:4], axis=-1, keepdims=True))
    return e1 - e2 + lam_init


def _alibi_slope(h, n_heads):
    slope = F32(0.0)
    for i in range(n_heads):
        slope = jnp.where(h == i, F32(2.0 ** (-8.0 * (i + 1) / n_heads)), slope)
    return slope


def _lane_fold(x, op):
    parts = [x[:, i * LANE:(i + 1) * LANE] for i in range(x.shape[1] // LANE)]
    return functools.reduce(op, parts)


def _attn_prompt_kernel(q_ref, k_ref, v_ref, lam_ref, dn_ref, o_ref, s_ref, m_ref, l_ref, acc_ref,
                        *, tq, tk, hd, n_heads, lam_init):
    h = pl.program_id(1)
    qi = pl.program_id(2)
    scale = hd ** -0.5
    slope = _alibi_slope(h, n_heads)
    n_diag = tq // tk
    n_full = qi * n_diag
    q = q_ref[...]
    zero = jnp.zeros((tq, hd), F32)
    qbd = jnp.concatenate([jnp.concatenate([q[:, :hd], zero], axis=1),
                           jnp.concatenate([zero, q[:, hd:]], axis=1)], axis=0).astype(BF16)
    row = lax.broadcasted_iota(jnp.int32, (tq, tk), 0)
    col = lax.broadcasted_iota(jnp.int32, (tq, tk), 1)
    d0 = (row - col).astype(F32)

    m_ref[...] = jnp.full_like(m_ref, -jnp.inf)

    def scores(c, masked):
        ks = pl.multiple_of(c * tk, tk)
        kc = k_ref[pl.ds(ks, tk), :].astype(BF16)
        off = jnp.full((1, 1), qi * tq - c * tk, jnp.int32).astype(F32)
        dist = d0 + off
        bias = slope * dist
        s = _dot_nt(qbd, kc) * scale - jnp.concatenate([bias, bias], axis=0)
        if masked:
            s = jnp.where(jnp.concatenate([dist, dist], axis=0) >= 0, s, -jnp.inf)
        s_ref[c] = s
        m_ref[...] = jnp.maximum(m_ref[...], _lane_fold(s, jnp.maximum))

    def scores_body(c, carry):
        scores(c, False)
        return carry

    lax.fori_loop(0, n_full, scores_body, 0)
    for j in range(n_diag):
        scores(n_full + j, True)

    m_row = jnp.max(m_ref[...], axis=-1, keepdims=True)
    m_ref[...] = jnp.broadcast_to(m_row, m_ref.shape)
    l_ref[...] = jnp.zeros_like(l_ref)
    acc_ref[...] = jnp.zeros_like(acc_ref)

    def pv_body(c, carry):
        ks = pl.multiple_of(c * tk, tk)
        mb = m_ref[...]
        p = jnp.exp(s_ref[c] - jnp.concatenate([mb] * (tk // LANE), axis=1))
        l_ref[...] += _lane_fold(p, jnp.add)
        acc_ref[...] += _dot(p.astype(BF16), v_ref[pl.ds(ks, tk), :].astype(BF16))
        return carry

    lax.fori_loop(0, n_full + n_diag, pv_body, 0)

    lam = _diff_lambda(lam_ref, lam_init)
    on = acc_ref[...] / jnp.sum(l_ref[...], axis=-1, keepdims=True)
    o = on[:tq] - lam * on[tq:]
    o_ref[...] = (_rms(o, dn_ref[...]) * (1.0 - lam_init)).astype(o_ref.dtype)


def _attn_prompt(z, lam_p, diff_norm, *, batch, seq, n_heads, hd, vd, col_q, col_k, col_v,
                 lam_init, tq=512, tk=256):
    assert vd == 2 * hd and seq % tq == 0 and tq % tk == 0 and tk % LANE == 0
    nq = seq // tq
    cw = 2 * hd
    bq, bk, bv = col_q // cw, col_k // cw, col_v // cw
    est = (2 * tq * cw * 4 + 2 * 2 * seq * cw * 4 + 2 * tq * vd * 2
           + (seq // tk) * 2 * tq * tk * 4 + 2 * 2 * tq * LANE * 4 + 2 * tq * vd * 4
           + 6 * 2 * tq * tk * 4)
    return pl.pallas_call(
        functools.partial(_attn_prompt_kernel, tq=tq, tk=tk, hd=hd, n_heads=n_heads,
                          lam_init=lam_init),
        name="attn_prompt",
        grid=(batch, n_heads, nq),
        in_specs=[
            pl.BlockSpec((tq, cw), lambda b, h, i: (b * nq + i, bq + h)),
            pl.BlockSpec((seq, cw), lambda b, h, i: (b, bk + h)),
            pl.BlockSpec((seq, cw), lambda b, h, i: (b, bv + h)),
            pl.BlockSpec((4, hd), lambda b, h, i: (0, 0)),
            pl.BlockSpec((1, vd), lambda b, h, i: (0, 0)),
        ],
        out_specs=pl.BlockSpec((tq, vd), lambda b, h, i: (b * nq + i, h)),
        out_shape=jax.ShapeDtypeStruct((batch * seq, n_heads * vd), BF16),
        scratch_shapes=[pltpu.VMEM((seq // tk, 2 * tq, tk), F32),
                        pltpu.VMEM((2 * tq, LANE), F32), pltpu.VMEM((2 * tq, LANE), F32),
                        pltpu.VMEM((2 * tq, vd), F32)],
        compiler_params=pltpu.CompilerParams(
            dimension_semantics=("parallel", "parallel", "arbitrary"),
            vmem_limit_bytes=_vmem_limit(est)),
    )(z, z, z, lam_p, diff_norm)


def _attn_sample_kernel(*refs, n_pages, page, n_heads, hd, dec, col_q, col_k, col_v, lam_init):
    z_ref = refs[1]
    kp_refs = refs[2:2 + n_pages]
    vp_refs = refs[2 + n_pages:2 + 2 * n_pages]
    pitch = 2 * n_heads
    lam_ref, dn_ref, o_ref, q2_ref, kn_ref, vn_ref = refs[2 + 2 * n_pages:]
    cw = 2 * hd
    rows = 2 * dec
    scale = hd ** -0.5
    past = n_pages * page
    n_keys = past + page
    r_iota = lax.broadcasted_iota(jnp.int32, (rows, n_keys), 0)
    j_iota = lax.broadcasted_iota(jnp.int32, (rows, n_keys), 1)
    q_idx = jnp.where(r_iota >= dec, r_iota - dec, r_iota)
    dist = (past + q_idx - j_iota).astype(F32)
    valid = j_iota <= past + q_idx
    rr = lax.broadcasted_iota(jnp.int32, (rows, cw), 0)
    cc = lax.broadcasted_iota(jnp.int32, (rows, cw), 1)
    keep = (rr < dec) == (cc < hd)
    lam = _diff_lambda(lam_ref, lam_init)

    width = n_heads * cw
    kn_ref[...] = jnp.zeros_like(kn_ref)
    vn_ref[...] = jnp.zeros_like(vn_ref)
    kn_ref[0:dec, :] = z_ref[0, :, col_k:col_k + width]
    vn_ref[0:dec, :] = z_ref[0, :, col_v:col_v + width]

    for h in range(n_heads):
        slope = 2.0 ** (-8.0 * (h + 1) / n_heads)
        qh = z_ref[0, :, col_q + h * cw: col_q + (h + 1) * cw]
        q2_ref[0:dec, :] = qh
        q2_ref[dec:rows, :] = qh
        qbd = jnp.where(keep, q2_ref[...], 0.0).astype(BF16)
        parts = []
        for kp in kp_refs:
            kcat = jnp.concatenate([kp[0, pl.ds(2 * h + m, page, stride=pitch), :] for m in range(2)],
                                   axis=-1)
            parts.append(_dot_nt(qbd, kcat.astype(BF16)))
        parts.append(_dot_nt(qbd, kn_ref[:, h * cw:(h + 1) * cw].astype(BF16)))
        s = jnp.concatenate(parts, axis=-1) * scale - slope * dist
        s = jnp.where(valid, s, -jnp.inf)
        p = jnp.exp(s - jnp.max(s, axis=-1, keepdims=True))
        l = jnp.sum(p, axis=-1, keepdims=True)
        p16 = p.astype(BF16)
        acc = _dot(p16[:, past:], vn_ref[:, h * cw:(h + 1) * cw].astype(BF16))
        for i, vp in enumerate(vp_refs):
            vcat = jnp.concatenate(
                [vp[0, pl.ds(n_heads * half + h, page, stride=pitch), :] for half in range(2)], axis=-1)
            acc = acc + _dot(p16[:, i * page:(i + 1) * page], vcat.astype(BF16))
        on = acc / l
        o = on[:dec] - lam * on[dec:]
        o_ref[0, :, h * cw:(h + 1) * cw] = (_rms(o, dn_ref[...]) * (1.0 - lam_init)).astype(o_ref.dtype)


def _attn_sample(z3, cache_k, cache_v, page_table, layer, lam_p, diff_norm, *, n_heads, hd, vd,
                 col_q, col_k, col_v, lam_init):
    dec_b, dec, zw = z3.shape
    n_pages = page_table.shape[1]
    depth, n_pool, page = cache_k.shape[:3]
    assert vd == 2 * hd
    width = n_heads * 2 * hd
    prow = page * n_heads * 2
    kf = cache_k.reshape(depth * n_pool, prow, hd)
    vf = cache_v.reshape(depth * n_pool, page, n_heads, 2, hd).transpose(0, 1, 3, 2, 4).reshape(
        depth * n_pool, prow, hd)
    pt = (page_table + layer * n_pool).reshape(-1).astype(jnp.int32)
    rows = 2 * dec
    n_keys = (n_pages + 1) * page

    def page_spec(i):
        return pl.BlockSpec((1, prow, hd), lambda b, pt_ref: (pt_ref[b * n_pages + i], 0, 0))

    est = (2 * dec * zw * 4 + 2 * 2 * n_pages * page * width * 4 + 2 * dec * width * 2
           + rows * 2 * hd * 4 + 2 * page * width * 4 + 8 * rows * n_keys * 4)
    grid_spec = pltpu.PrefetchScalarGridSpec(
        num_scalar_prefetch=1,
        grid=(dec_b,),
        in_specs=([pl.BlockSpec((1, dec, zw), lambda b, pt_ref: (b, 0, 0))]
                  + [page_spec(i) for i in range(n_pages)]
                  + [page_spec(i) for i in range(n_pages)]
                  + [pl.BlockSpec((4, hd), lambda b, pt_ref: (0, 0)),
                     pl.BlockSpec((1, vd), lambda b, pt_ref: (0, 0))]),
        out_specs=pl.BlockSpec((1, dec, n_heads * vd), lambda b, pt_ref: (b, 0, 0)),
        scratch_shapes=[pltpu.VMEM((rows, 2 * hd), F32),
                        pltpu.VMEM((page, width), F32),
                        pltpu.VMEM((page, width), F32)],
    )
    return pl.pallas_call(
        functools.partial(_attn_sample_kernel, n_pages=n_pages, page=page, n_heads=n_heads, hd=hd,
                          dec=dec, col_q=col_q, col_k=col_k, col_v=col_v, lam_init=lam_init),
        name="attn_sample",
        grid_spec=grid_spec,
        out_shape=jax.ShapeDtypeStruct((dec_b, dec, n_heads * vd), BF16),
        compiler_params=pltpu.CompilerParams(
            dimension_semantics=("parallel",),
            vmem_limit_bytes=_vmem_limit(est)),
    )(pt, z3, *([kf] * n_pages), *([vf] * n_pages), lam_p, diff_norm)


def _log_decay(glr, w2_ref, b_ref):
    x = _dot(glr.astype(BF16), w2_ref[...]) + b_ref[...]
    return (jnp.minimum(x, 0.0) - jnp.log1p(jnp.exp(-jnp.abs(x)))) / GLA_TAU


def _split_bf16(x):
    hi = x.astype(BF16)
    r1 = x - hi.astype(F32)
    mid = r1.astype(BF16)
    lo = (r1 - mid.astype(F32)).astype(BF16)
    return hi, mid, lo


def _gla_out(o, gn_ref, gr):
    return _rms(o, gn_ref[...]) * (gr * jax.nn.sigmoid(gr))


def _gla_prompt_kernel(q_ref, k_ref, v_ref, gr_ref, glr_ref, w2_ref, b_ref, gn_ref,
                       o_ref, s_ref, st_ref, *, chunk, dk):
    c = pl.program_id(2)

    @pl.when(c == 0)
    def _():
        st_ref[...] = jnp.zeros_like(st_ref)

    log_a = _log_decay(glr_ref[...], w2_ref, b_ref)
    row = lax.broadcasted_iota(jnp.int32, (chunk, chunk), 0)
    col = lax.broadcasted_iota(jnp.int32, (chunk, chunk), 1)
    causal = row >= col
    tri = jnp.where(causal, 1.0, 0.0).astype(BF16)
    hi, mid, lo = _split_bf16(log_a)
    cum = _dot(tri, hi) + _dot(tri, mid) + _dot(tri, lo)
    last = cum[chunk - 1:chunk, :]

    q = q_ref[...] * (dk ** -0.5)
    k = k_ref[...]
    v = v_ref[...]
    v16 = v.astype(BF16)
    st = st_ref[...]

    qt = (q * jnp.exp(cum)).astype(BF16)
    kt = (k * jnp.exp(-cum)).astype(BF16)
    att = jnp.where(causal, _dot_nt(qt, kt), 0.0)
    o = _dot(att.astype(BF16), v16) + _dot_nt(qt, st.astype(BF16))

    kd = (k * jnp.exp(last - cum)).astype(BF16)
    st_new = st * jnp.exp(last) + _dot(v.T.astype(BF16), kd)
    st_ref[...] = st_new

    o_ref[...] = _gla_out(o, gn_ref, gr_ref[...]).astype(o_ref.dtype)

    @pl.when(c == pl.num_programs(2) - 1)
    def _():
        s_ref[0, 0] = st_new.T


def _gla_prompt(z, w2p, bias, gla_norm, *, batch, seq, n_heads, dk, dv, col_q, col_k, col_v,
                col_gr, col_glr, chunk=256):
    assert seq % chunk == 0
    nc = seq // chunk
    bq, bk, bv, bg, bl = col_q // dk, col_k // dk, col_v // dv, col_gr // dv, col_glr // LANE
    est = 2 * chunk * (2 * dk + 2 * dv + LANE) * 4 + 2 * chunk * dv * 2 + 3 * dk * dv * 4 \
        + 6 * chunk * chunk * 4 + 12 * chunk * dv * 4
    return pl.pallas_call(
        functools.partial(_gla_prompt_kernel, chunk=chunk, dk=dk),
        name="gla_prompt",
        grid=(batch, n_heads, nc),
        in_specs=[
            pl.BlockSpec((chunk, dk), lambda b, h, c: (b * nc + c, bq + h)),
            pl.BlockSpec((chunk, dk), lambda b, h, c: (b * nc + c, bk + h)),
            pl.BlockSpec((chunk, dv), lambda b, h, c: (b * nc + c, bv + h)),
            pl.BlockSpec((chunk, dv), lambda b, h, c: (b * nc + c, bg + h)),
            pl.BlockSpec((chunk, LANE), lambda b, h, c: (b * nc + c, bl)),
            pl.BlockSpec((LANE, dk), lambda b, h, c: (0, h)),
            pl.BlockSpec((1, dk), lambda b, h, c: (0, h)),
            pl.BlockSpec((1, dv), lambda b, h, c: (0, 0)),
        ],
        out_specs=[
            pl.BlockSpec((chunk, dv), lambda b, h, c: (b * nc + c, h)),
            pl.BlockSpec((1, 1, dk, dv), lambda b, h, c: (b, h, 0, 0)),
        ],
        out_shape=[jax.ShapeDtypeStruct((batch * seq, n_heads * dv), BF16),
                   jax.ShapeDtypeStruct((batch, n_heads, dk, dv), F32)],
        scratch_shapes=[pltpu.VMEM((dv, dk), F32)],
        compiler_params=pltpu.CompilerParams(
            dimension_semantics=("parallel", "parallel", "arbitrary"),
            vmem_limit_bytes=_vmem_limit(est)),
    )(z, z, z, z, z, w2p, bias, gla_norm)


def _gla_sample_kernel(z_ref, s0_ref, w2_ref, b_ref, gn_ref, o_ref, s_ref,
                       *, n_heads, dk, dv, dec, col_q, col_k, col_v, col_gr, col_glr):
    glr = z_ref[0, :, col_glr:col_glr + LANE]
    log_a_all = _log_decay(glr, w2_ref, b_ref)
    t_k = lax.broadcasted_iota(jnp.int32, (dec, dk), 0)
    t_v = lax.broadcasted_iota(jnp.int32, (dec, dv), 0)
    eye = (lax.broadcasted_iota(jnp.int32, (dk, dk), 0)
           == lax.broadcasted_iota(jnp.int32, (dk, dk), 1))

    def to_col(r):
        return jnp.sum(jnp.where(eye, jnp.broadcast_to(r, (dk, dk)), 0.0), axis=-1, keepdims=True)

    for h in range(n_heads):
        log_a = log_a_all[:, h * dk:(h + 1) * dk]
        cum = jnp.zeros((dec, dk), F32)
        for t in range(dec):
            cum = cum + jnp.where(t_k >= t, log_a[t:t + 1], 0.0)
        last = cum[dec - 1:dec]
        q = z_ref[0, :, col_q + h * dk: col_q + (h + 1) * dk] * (dk ** -0.5)
        k = z_ref[0, :, col_k + h * dk: col_k + (h + 1) * dk]
        v = z_ref[0, :, col_v + h * dv: col_v + (h + 1) * dv]
        gr = z_ref[0, :, col_gr + h * dv: col_gr + (h + 1) * dv]
        s0 = s0_ref[0, h]

        o = _dot((q * jnp.exp(cum)).astype(BF16), s0.astype(BF16))
        for t in range(dec):
            o_t = jnp.zeros((1, dv), F32)
            for j in range(t + 1):
                w = jnp.exp(cum[t:t + 1] - cum[j:j + 1])
                a = jnp.sum(q[t:t + 1] * k[j:j + 1] * w, axis=-1, keepdims=True)
                o_t = o_t + a * v[j:j + 1]
            o = o + jnp.where(t_v == t, o_t, 0.0)

        kd = k * jnp.exp(last - cum)
        s_new = s0 * to_col(jnp.exp(last))
        for t in range(dec):
            s_new = s_new + to_col(kd[t:t + 1]) * v[t:t + 1]
        s_ref[0, h] = s_new
        o_ref[0, :, h * dv:(h + 1) * dv] = _gla_out(o, gn_ref, gr).astype(o_ref.dtype)


def _gla_sample(z3, state, w2p, bias, gla_norm, *, n_heads, dk, dv, col_q, col_k, col_v,
                col_gr, col_glr):
    dec_b, dec, zw = z3.shape
    est = 2 * dec * zw * 4 + 2 * 2 * n_heads * dk * dv * 4 + LANE * n_heads * dk * 2 \
        + 8 * dk * dv * 4
    return pl.pallas_call(
        functools.partial(_gla_sample_kernel, n_heads=n_heads, dk=dk, dv=dv, dec=dec, col_q=col_q,
                          col_k=col_k, col_v=col_v, col_gr=col_gr, col_glr=col_glr),
        name="gla_sample",
        grid=(dec_b,),
        in_specs=[
            pl.BlockSpec((1, dec, zw), lambda b: (b, 0, 0)),
            pl.BlockSpec((1, n_heads, dk, dv), lambda b: (b, 0, 0, 0)),
            pl.BlockSpec((LANE, n_heads * dk), lambda b: (0, 0)),
            pl.BlockSpec((1, n_heads * dk), lambda b: (0, 0)),
            pl.BlockSpec((1, dv), lambda b: (0, 0)),
        ],
        out_specs=[
            pl.BlockSpec((1, dec, n_heads * dv), lambda b: (b, 0, 0)),
            pl.BlockSpec((1, n_heads, dk, dv), lambda b: (b, 0, 0, 0)),
        ],
        out_shape=[jax.ShapeDtypeStruct((dec_b, dec, n_heads * dv), BF16),
                   jax.ShapeDtypeStruct((dec_b, n_heads, dk, dv), F32)],
        compiler_params=pltpu.CompilerParams(
            dimension_semantics=("parallel",),
            vmem_limit_bytes=_vmem_limit(est)),
    )(z3, state, w2p, bias, gla_norm)


def _pad_cols(w, n):
    return jnp.pad(w, ((0, 0), (0, n - w.shape[1])))


def _round_up(x, m):
    return (x + m - 1) // m * m


def kernel(x_prompt, x_sample, cache_k, cache_v, state_gla, page_table, norm_ffn1, ffn1_w_gate, ffn1_w_up, ffn1_w_down, norm_mix, w_in, gla_gate_w2, gla_gate_b, lambda_q1, lambda_k1, lambda_q2, lambda_k2, diff_norm, gla_norm, w_out, norm_ffn2, ffn2_w_gate, ffn2_w_up, ffn2_w_down, norm_final):
    batch, seq, d = x_prompt.shape
    dec_b, dec, _ = x_sample.shape
    depth = norm_ffn1.shape[0]
    n_heads, hd = cache_k.shape[3], cache_k.shape[5]
    vd = cache_v.shape[4]
    g_heads, dk, dv = state_gla.shape[2:]
    rank = gla_gate_w2.shape[1]
    d_ff = ffn1_w_gate.shape[2]

    w_qk = n_heads * 2 * hd
    sizes = (w_qk, w_qk, n_heads * vd, g_heads * dk, g_heads * dk, g_heads * dv, g_heads * dv, rank)
    cols = [0]
    for s in sizes:
        cols.append(cols[-1] + s)
    col_dq, col_dk, col_dv, col_gq, col_gk, col_gv, col_gr, col_glr = cols[:8]
    assert w_in.shape[2] == cols[8] and rank <= LANE and col_glr % LANE == 0
    n_in_pad = _round_up(col_glr + LANE, 896)
    ff_pad = _round_up(d_ff, 512)
    diff_w = n_heads * vd

    yp = x_prompt.reshape(batch * seq, d)
    ys = x_sample.reshape(dec_b * dec, d)
    nf = norm_final.reshape(1, d)
    outs = [[] for _ in range(6)]

    for layer in range(depth):
        lam_init = _lambda_init(layer)
        ffn_w = []
        for wg, wu, wd in ((ffn1_w_gate, ffn1_w_up, ffn1_w_down), (ffn2_w_gate, ffn2_w_up, ffn2_w_down)):
            ffn_w.append((_pad_cols(wg[layer], ff_pad).astype(BF16),
                          _pad_cols(wu[layer], ff_pad).astype(BF16),
                          jnp.pad(wd[layer], ((0, ff_pad - d_ff), (0, 0))).astype(BF16)))
        w_in_p = _pad_cols(w_in[layer], n_in_pad).astype(BF16)
        w2p = jnp.pad(gla_gate_w2[layer], ((0, LANE - rank), (0, 0))).astype(BF16)
        gate_b = gla_gate_b[layer].reshape(1, -1)
        lam_p = jnp.stack([lambda_q1[layer], lambda_k1[layer], lambda_q2[layer], lambda_k2[layer]])
        dn = diff_norm[layer].reshape(1, vd)
        gn = gla_norm[layer].reshape(1, dv)
        wo = w_out[layer].astype(BF16)
        wo_a, wo_b = wo[:diff_w], wo[diff_w:]
        g1 = norm_ffn1[layer].reshape(1, d)
        gm = norm_mix[layer].reshape(1, d)
        g2 = norm_ffn2[layer].reshape(1, d)
        last = layer == depth - 1

        def pre(x):
            x = _ffn(x, g1, *ffn_w[0], nf, final_norm=False)
            return x, _inproj(x, gm, w_in_p)

        def post(x, mix_a, mix_b):
            x = _outproj(x, mix_a, mix_b, wo_a, wo_b)
            return _ffn(x, g2, *ffn_w[1], nf, final_norm=last)

        yp, zp = pre(yp)
        mix_a = _attn_prompt(zp, lam_p, dn, batch=batch, seq=seq, n_heads=n_heads, hd=hd, vd=vd,
                             col_q=col_dq, col_k=col_dk, col_v=col_dv, lam_init=lam_init)
        mix_b, s_p = _gla_prompt(zp, w2p, gate_b, gn, batch=batch, seq=seq, n_heads=g_heads, dk=dk,
                                 dv=dv, col_q=col_gq, col_k=col_gk, col_v=col_gv, col_gr=col_gr,
                                 col_glr=col_glr)
        yp = post(yp, mix_a, mix_b)
        outs[0].append(zp[:, col_dk:col_dk + w_qk].reshape(batch, seq, n_heads, 2, hd))
        outs[1].append(zp[:, col_dv:col_dv + diff_w].reshape(batch, seq, n_heads, vd))
        outs[2].append(s_p.astype(state_gla.dtype))

        ys, zs = pre(ys)
        zs3 = zs.reshape(dec_b, dec, n_in_pad)
        mix_a = _attn_sample(zs3, cache_k, cache_v, page_table, layer, lam_p, dn, n_heads=n_heads,
                             hd=hd, vd=vd, col_q=col_dq, col_k=col_dk, col_v=col_dv,
                             lam_init=lam_init)
        mix_b, s_s = _gla_sample(zs3, state_gla[layer], w2p, gate_b, gn, n_heads=g_heads, dk=dk,
                                 dv=dv, col_q=col_gq, col_k=col_gk, col_v=col_gv, col_gr=col_gr,
                                 col_glr=col_glr)
        ys = post(ys, mix_a.reshape(dec_b * dec, -1), mix_b.reshape(dec_b * dec, -1))
        outs[3].append(zs[:, col_dk:col_dk + w_qk].reshape(dec_b, dec, n_heads, 2, hd))
        outs[4].append(zs[:, col_dv:col_dv + diff_w].reshape(dec_b, dec, n_heads, vd))
        outs[5].append(s_s.astype(state_gla.dtype))

    if depth == 0:
        raise ValueError("depth must be positive")
    y_prompt = yp.reshape(batch, seq, d)
    y_sample = ys.reshape(dec_b, dec, d)
    return (y_prompt, y_sample) + tuple(jnp.stack(o) for o in outs)
```

```python
import functools
import math

import jax
import jax.numpy as jnp
from jax import lax
from jax.experimental import pallas as pl
from jax.experimental.pallas import tpu as pltpu

F32 = jnp.float32
BF16 = jnp.bfloat16

NORM_EPS = 1e-6
GLA_TAU = 16.0
LANE = 128
VMEM_PHYSICAL = 64 * 2**20


def _lambda_init(layer):
    return 0.8 - 0.6 * math.exp(-0.3 * layer)


def _vmem_limit(nbytes):
    return int(min(nbytes * 5 // 4 + (4 << 20), VMEM_PHYSICAL - (8 << 20)))


def _rms(x, gain):
    return x * lax.rsqrt(jnp.mean(x * x, axis=-1, keepdims=True) + NORM_EPS) * gain


def _dot(a, b):
    return jnp.dot(a, b, preferred_element_type=F32)


def _dot_nt(a, b):
    return lax.dot_general(a, b, (((1,), (1,)), ((), ())), preferred_element_type=F32)


def _ffn_kernel(x_ref, g_ref, wg_ref, wu_ref, wd_ref, gt_ref, *rest, tail):
    o_ref = rest[0]
    h_ref, acc_ref = rest[-2:]
    j = pl.program_id(1)

    @pl.when(j == 0)
    def _():
        h_ref[...] = _rms(x_ref[...], g_ref[...]).astype(BF16)
        acc_ref[...] = jnp.zeros_like(acc_ref)

    h = h_ref[...]
    gate = _dot(h, wg_ref[...])
    up = _dot(h, wu_ref[...])
    act = (gate * jax.nn.sigmoid(gate)) * up
    acc_ref[...] += _dot(act.astype(BF16), wd_ref[...])

    @pl.when(j == pl.num_programs(1) - 1)
    def _():
        y = x_ref[...] + 0.5 * acc_ref[...]
        if tail == "final":
            y = _rms(y, gt_ref[...])
        o_ref[...] = y
        if tail == "next":
            rest[1][...] = _rms(y, gt_ref[...]).astype(BF16)


def _ffn(x, gain, wg, wu, wd, gain_tail, *, tail, tm=512, tf=512):
    m, d = x.shape
    fpad = wg.shape[1]
    assert m % tm == 0 and fpad % tf == 0 and tail in ("final", "next", "none")
    est = (2 * 2 * tm * d * 4
           + 2 * tm * d * 2
           + tm * d * (2 + 4)
           + 2 * 3 * d * tf * 2
           + 3 * tm * tf * 4)
    row_spec = pl.BlockSpec((tm, d), lambda i, j: (i, 0))
    out_specs, out_shape = row_spec, jax.ShapeDtypeStruct((m, d), F32)
    if tail == "next":
        out_specs, out_shape = [row_spec, row_spec], [out_shape, jax.ShapeDtypeStruct((m, d), BF16)]
    return pl.pallas_call(
        functools.partial(_ffn_kernel, tail=tail),
        name="ffn_" + tail,
        grid=(m // tm, fpad // tf),
        in_specs=[
            row_spec,
            pl.BlockSpec((1, d), lambda i, j: (0, 0)),
            pl.BlockSpec((d, tf), lambda i, j: (0, j)),
            pl.BlockSpec((d, tf), lambda i, j: (0, j)),
            pl.BlockSpec((tf, d), lambda i, j: (j, 0)),
            pl.BlockSpec((1, d), lambda i, j: (0, 0)),
        ],
        out_specs=out_specs,
        out_shape=out_shape,
        scratch_shapes=[pltpu.VMEM((tm, d), BF16), pltpu.VMEM((tm, d), F32)],
        compiler_params=pltpu.CompilerParams(
            dimension_semantics=("parallel", "arbitrary"),
            vmem_limit_bytes=_vmem_limit(est)),
    )(x, gain, wg, wu, wd, gain_tail)


def _inproj_kernel(h_ref, w_ref, wl_ref, z_ref, glr_ref, kr_ref, vr_ref,
                   *, tm, n_heads, hd, k_tile, v_tile):
    j = pl.program_id(1)
    pitch = 2 * n_heads
    z_ref[...] = _dot(h_ref[...], w_ref[...])

    @pl.when(j == 0)
    def _():
        glr_ref[...] = _dot(h_ref[...], wl_ref[...])

    @pl.when(j == k_tile)
    def _():
        for c in range(pitch):
            kr_ref[pl.ds(c, tm, stride=pitch), :] = z_ref[:, c * hd:(c + 1) * hd]

    @pl.when(j == v_tile)
    def _():
        for h in range(n_heads):
            for half in range(2):
                c = 2 * h + half
                vr_ref[pl.ds(n_heads * half + h, tm, stride=pitch), :] = z_ref[:, c * hd:(c + 1) * hd]


def _inproj(h, w, wl, *, n_heads, hd, col_k, col_v, tm=1024):
    m, d = h.shape
    n = w.shape[1]
    tn = n_heads * 2 * hd
    tm = min(tm, m)
    pitch = 2 * n_heads
    assert m % tm == 0 and n % tn == 0 and col_k % tn == 0 and col_v % tn == 0 and hd == LANE
    est = (2 * tm * d * 2 + 2 * d * (tn + LANE) * 2 + 2 * tm * (tn + LANE) * 4
           + 2 * 2 * tm * pitch * hd * 4 + tm * tn * 4)
    return pl.pallas_call(
        functools.partial(_inproj_kernel, tm=tm, n_heads=n_heads, hd=hd, k_tile=col_k // tn,
                          v_tile=col_v // tn),
        name="in_proj",
        grid=(m // tm, n // tn),
        in_specs=[
            pl.BlockSpec((tm, d), lambda i, j: (i, 0)),
            pl.BlockSpec((d, tn), lambda i, j: (0, j)),
            pl.BlockSpec((d, LANE), lambda i, j: (0, 0)),
        ],
        out_specs=[
            pl.BlockSpec((tm, tn), lambda i, j: (i, j)),
            pl.BlockSpec((tm, LANE), lambda i, j: (i, 0)),
            pl.BlockSpec((tm * pitch, hd), lambda i, j: (i, 0)),
            pl.BlockSpec((tm * pitch, hd), lambda i, j: (i, 0)),
        ],
        out_shape=[jax.ShapeDtypeStruct((m, n), F32),
                   jax.ShapeDtypeStruct((m, LANE), F32),
                   jax.ShapeDtypeStruct((m * pitch, hd), F32),
                   jax.ShapeDtypeStruct((m * pitch, hd), F32)],
        compiler_params=pltpu.CompilerParams(
            dimension_semantics=("parallel", "arbitrary"),
            vmem_limit_bytes=_vmem_limit(est)),
    )(h, w, wl)


def _outproj_kernel(x_ref, a_ref, b_ref, wa_ref, wb_ref, o_ref):
    o_ref[...] = x_ref[...] + _dot(a_ref[...], wa_ref[...]) + _dot(b_ref[...], wb_ref[...])


def _outproj(x, mix_a, mix_b, w_a, w_b, *, tm=512):
    m, d = x.shape
    ka, kb = mix_a.shape[1], mix_b.shape[1]
    assert m % tm == 0
    est = 2 * 2 * tm * d * 4 + 2 * tm * (ka + kb) * 2 + 2 * (ka + kb) * d * 2 + tm * d * 4
    return pl.pallas_call(
        _outproj_kernel,
        name="out_proj",
        grid=(m // tm,),
        in_specs=[
            pl.BlockSpec((tm, d), lambda i: (i, 0)),
            pl.BlockSpec((tm, ka), lambda i: (i, 0)),
            pl.BlockSpec((tm, kb), lambda i: (i, 0)),
            pl.BlockSpec((ka, d), lambda i: (0, 0)),
            pl.BlockSpec((kb, d), lambda i: (0, 0)),
        ],
        out_specs=pl.BlockSpec((tm, d), lambda i: (i, 0)),
        out_shape=jax.ShapeDtypeStruct((m, d), F32),
        compiler_params=pltpu.CompilerParams(
            dimension_semantics=("parallel",),
            vmem_limit_bytes=_vmem_limit(est)),
    )(x, mix_a, mix_b, w_a, w_b)


def _diff_lambda(lam_ref, lam_init):
    lp = lam_ref[...]
    e1 = jnp.exp(jnp.sum(lp[0:1] * lp[1:2], axis=-1, keepdims=True))
    e2 = jnp.exp(jnp.sum(lp[2:3] * lp[3:4], axis=-1, keepdims=True))
    return e1 - e2 + lam_init


def _alibi_slope(h, n_heads):
    slope = F32(0.0)
    for i in range(n_heads):
        slope = jnp.where(h == i, F32(2.0 ** (-8.0 * (i + 1) / n_heads)), slope)
    return slope


def _lane_fold(x, op):
    parts = [x[:, i * LANE:(i + 1) * LANE] for i in range(x.shape[1] // LANE)]
    return functools.reduce(op, parts)


def _attn_prompt_kernel(q_ref, k_ref, v_ref, lam_ref, dn_ref, o_ref, s_ref, m_ref, l_ref, acc_ref,
                        *, tq, tk, hd, n_heads, lam_init):
    h = pl.program_id(1)
    qi = pl.program_id(2)
    scale = hd ** -0.5
    slope = _alibi_slope(h, n_heads)
    n_diag = tq // tk
    n_full = qi * n_diag
    q = q_ref[...]
    zero = jnp.zeros((tq, hd), F32)
    qbd = jnp.concatenate([jnp.concatenate([q[:, :hd], zero], axis=1),
                           jnp.concatenate([zero, q[:, hd:]], axis=1)], axis=0).astype(BF16)
    row = lax.broadcasted_iota(jnp.int32, (tq, tk), 0)
    col = lax.broadcasted_iota(jnp.int32, (tq, tk), 1)
    d0 = (row - col).astype(F32)

    m_ref[...] = jnp.full_like(m_ref, -jnp.inf)

    def scores(c, masked):
        ks = pl.multiple_of(c * tk, tk)
        kc = k_ref[pl.ds(ks, tk), :].astype(BF16)
        off = jnp.full((1, 1), qi * tq - c * tk, jnp.int32).astype(F32)
        dist = d0 + off
        bias = slope * dist
        s = _dot_nt(qbd, kc) * scale - jnp.concatenate([bias, bias], axis=0)
        if masked:
            s = jnp.where(jnp.concatenate([dist, dist], axis=0) >= 0, s, -jnp.inf)
        s_ref[c] = s
        m_ref[...] = jnp.maximum(m_ref[...], _lane_fold(s, jnp.maximum))

    def scores_body(c, carry):
        scores(c, False)
        return carry

    lax.fori_loop(0, n_full, scores_body, 0)
    for j in range(n_diag):
        scores(n_full + j, True)

    m_row = jnp.max(m_ref[...], axis=-1, keepdims=True)
    m_ref[...] = jnp.broadcast_to(m_row, m_ref.shape)
    l_ref[...] = jnp.zeros_like(l_ref)
    acc_ref[...] = jnp.zeros_like(acc_ref)

    def pv_body(c, carry):
        ks = pl.multiple_of(c * tk, tk)
        mb = m_ref[...]
        p = jnp.exp(s_ref[c] - jnp.concatenate([mb] * (tk // LANE), axis=1))
        l_ref[...] += _lane_fold(p, jnp.add)
        acc_ref[...] += _dot(p.astype(BF16), v_ref[pl.ds(ks, tk), :].astype(BF16))
        return carry

    lax.fori_loop(0, n_full + n_diag, pv_body, 0)

    lam = _diff_lambda(lam_ref, lam_init)
    on = acc_ref[...] / jnp.sum(l_ref[...], axis=-1, keepdims=True)
    o = on[:tq] - lam * on[tq:]
    o_ref[...] = (_rms(o, dn_ref[...]) * (1.0 - lam_init)).astype(o_ref.dtype)


def _attn_prompt(z, lam_p, diff_norm, *, batch, seq, n_heads, hd, vd, col_q, col_k, col_v,
                 lam_init, tq=512, tk=256):
    assert vd == 2 * hd and seq % tq == 0 and tq % tk == 0 and tk % LANE == 0
    nq = seq // tq
    cw = 2 * hd
    bq, bk, bv = col_q // cw, col_k // cw, col_v // cw
    est = (2 * tq * cw * 4 + 2 * 2 * seq * cw * 4 + 2 * tq * vd * 2
           + (seq // tk) * 2 * tq * tk * 4 + 2 * 2 * tq * LANE * 4 + 2 * tq * vd * 4
           + 6 * 2 * tq * tk * 4)
    return pl.pallas_call(
        functools.partial(_attn_prompt_kernel, tq=tq, tk=tk, hd=hd, n_heads=n_heads,
                          lam_init=lam_init),
        name="attn_prompt",
        grid=(batch, n_heads, nq),
        in_specs=[
            pl.BlockSpec((tq, cw), lambda b, h, i: (b * nq + i, bq + h)),
            pl.BlockSpec((seq, cw), lambda b, h, i: (b, bk + h)),
            pl.BlockSpec((seq, cw), lambda b, h, i: (b, bv + h)),
            pl.BlockSpec((4, hd), lambda b, h, i: (0, 0)),
            pl.BlockSpec((1, vd), lambda b, h, i: (0, 0)),
        ],
        out_specs=pl.BlockSpec((tq, vd), lambda b, h, i: (b * nq + i, h)),
        out_shape=jax.ShapeDtypeStruct((batch * seq, n_heads * vd), BF16),
        scratch_shapes=[pltpu.VMEM((seq // tk, 2 * tq, tk), F32),
                        pltpu.VMEM((2 * tq, LANE), F32), pltpu.VMEM((2 * tq, LANE), F32),
                        pltpu.VMEM((2 * tq, vd), F32)],
        compiler_params=pltpu.CompilerParams(
            dimension_semantics=("parallel", "parallel", "arbitrary"),
            vmem_limit_bytes=_vmem_limit(est)),
    )(z, z, z, lam_p, diff_norm)


def _attn_sample_kernel(*refs, n_pages, page, n_heads, hd, dec, col_q, col_k, col_v, lam_init):
    z_ref = refs[1]
    kp_refs = refs[2:2 + n_pages]
    vp_refs = refs[2 + n_pages:2 + 2 * n_pages]
    pitch = 2 * n_heads
    lam_ref, dn_ref, o_ref, q2_ref, kn_ref, vn_ref = refs[2 + 2 * n_pages:]
    cw = 2 * hd
    rows = 2 * dec
    scale = hd ** -0.5
    past = n_pages * page
    n_keys = past + page
    r_iota = lax.broadcasted_iota(jnp.int32, (rows, n_keys), 0)
    j_iota = lax.broadcasted_iota(jnp.int32, (rows, n_keys), 1)
    q_idx = jnp.where(r_iota >= dec, r_iota - dec, r_iota)
    dist = (past + q_idx - j_iota).astype(F32)
    valid = j_iota <= past + q_idx
    rr = lax.broadcasted_iota(jnp.int32, (rows, cw), 0)
    cc = lax.broadcasted_iota(jnp.int32, (rows, cw), 1)
    keep = (rr < dec) == (cc < hd)
    lam = _diff_lambda(lam_ref, lam_init)

    width = n_heads * cw
    kn_ref[...] = jnp.zeros_like(kn_ref)
    vn_ref[...] = jnp.zeros_like(vn_ref)
    kn_ref[0:dec, :] = z_ref[0, :, col_k:col_k + width]
    vn_ref[0:dec, :] = z_ref[0, :, col_v:col_v + width]

    for h in range(n_heads):
        slope = 2.0 ** (-8.0 * (h + 1) / n_heads)
        qh = z_ref[0, :, col_q + h * cw: col_q + (h + 1) * cw]
        q2_ref[0:dec, :] = qh
        q2_ref[dec:rows, :] = qh
        qbd = jnp.where(keep, q2_ref[...], 0.0).astype(BF16)
        parts = []
        for kp in kp_refs:
            kcat = jnp.concatenate([kp[0, pl.ds(2 * h + m, page, stride=pitch), :] for m in range(2)],
                                   axis=-1)
            parts.append(_dot_nt(qbd, kcat.astype(BF16)))
        parts.append(_dot_nt(qbd, kn_ref[:, h * cw:(h + 1) * cw].astype(BF16)))
        s = jnp.concatenate(parts, axis=-1) * scale - slope * dist
        s = jnp.where(valid, s, -jnp.inf)
        p = jnp.exp(s - jnp.max(s, axis=-1, keepdims=True))
        l = jnp.sum(p, axis=-1, keepdims=True)
        p16 = p.astype(BF16)
        acc = _dot(p16[:, past:], vn_ref[:, h * cw:(h + 1) * cw].astype(BF16))
        for i, vp in enumerate(vp_refs):
            vcat = jnp.concatenate(
                [vp[0, pl.ds(n_heads * half + h, page, stride=pitch), :] for half in range(2)], axis=-1)
            acc = acc + _dot(p16[:, i * page:(i + 1) * page], vcat.astype(BF16))
        on = acc / l
        o = on[:dec] - lam * on[dec:]
        o_ref[0, :, h * cw:(h + 1) * cw] = (_rms(o, dn_ref[...]) * (1.0 - lam_init)).astype(o_ref.dtype)


def _attn_sample(z3, cache_k, cache_v, page_table, layer, lam_p, diff_norm, *, n_heads, hd, vd,
                 col_q, col_k, col_v, lam_init):
    dec_b, dec, zw = z3.shape
    n_pages = page_table.shape[1]
    depth, n_pool, page = cache_k.shape[:3]
    assert vd == 2 * hd
    width = n_heads * 2 * hd
    prow = page * n_heads * 2
    kf = cache_k.reshape(depth * n_pool, prow, hd)
    vf = cache_v.reshape(depth * n_pool, page, n_heads, 2, hd).transpose(0, 1, 3, 2, 4).reshape(
        depth * n_pool, prow, hd)
    pt = (page_table + layer * n_pool).reshape(-1).astype(jnp.int32)
    rows = 2 * dec
    n_keys = (n_pages + 1) * page

    def page_spec(i):
        return pl.BlockSpec((1, prow, hd), lambda b, pt_ref: (pt_ref[b * n_pages + i], 0, 0))

    est = (2 * dec * zw * 4 + 2 * 2 * n_pages * page * width * 4 + 2 * dec * width * 2
           + rows * 2 * hd * 4 + 2 * page * width * 4 + 8 * rows * n_keys * 4)
    grid_spec = pltpu.PrefetchScalarGridSpec(
        num_scalar_prefetch=1,
        grid=(dec_b,),
        in_specs=([pl.BlockSpec((1, dec, zw), lambda b, pt_ref: (b, 0, 0))]
                  + [page_spec(i) for i in range(n_pages)]
                  + [page_spec(i) for i in range(n_pages)]
                  + [pl.BlockSpec((4, hd), lambda b, pt_ref: (0, 0)),
                     pl.BlockSpec((1, vd), lambda b, pt_ref: (0, 0))]),
        out_specs=pl.BlockSpec((1, dec, n_heads * vd), lambda b, pt_ref: (b, 0, 0)),
        scratch_shapes=[pltpu.VMEM((rows, 2 * hd), F32),
                        pltpu.VMEM((page, width), F32),
                        pltpu.VMEM((page, width), F32)],
    )
    return pl.pallas_call(
        functools.partial(_attn_sample_kernel, n_pages=n_pages, page=page, n_heads=n_heads, hd=hd,
                          dec=dec, col_q=col_q, col_k=col_k, col_v=col_v, lam_init=lam_init),
        name="attn_sample",
        grid_spec=grid_spec,
        out_shape=jax.ShapeDtypeStruct((dec_b, dec, n_heads * vd), BF16),
        compiler_params=pltpu.CompilerParams(
            dimension_semantics=("parallel",),
            vmem_limit_bytes=_vmem_limit(est)),
    )(pt, z3, *([kf] * n_pages), *([vf] * n_pages), lam_p, diff_norm)


def _log_decay(glr, w2_ref, b_ref):
    x = _dot(glr.astype(BF16), w2_ref[...]) + b_ref[...]
    return (jnp.minimum(x, 0.0) - jnp.log1p(jnp.exp(-jnp.abs(x)))) / GLA_TAU


def _split_bf16(x):
    hi = x.astype(BF16)
    r1 = x - hi.astype(F32)
    mid = r1.astype(BF16)
    lo = (r1 - mid.astype(F32)).astype(BF16)
    return hi, mid, lo


def _gla_out(o, gn_ref, gr):
    return _rms(o, gn_ref[...]) * (gr * jax.nn.sigmoid(gr))


def _gla_prompt_kernel(q_ref, k_ref, v_ref, gr_ref, glr_ref, w2_ref, b_ref, gn_ref,
                       o_ref, s_ref, st_ref, *, chunk, dk):
    c = pl.program_id(2)

    @pl.when(c == 0)
    def _():
        st_ref[...] = jnp.zeros_like(st_ref)

    log_a = _log_decay(glr_ref[...], w2_ref, b_ref)
    row = lax.broadcasted_iota(jnp.int32, (chunk, chunk), 0)
    col = lax.broadcasted_iota(jnp.int32, (chunk, chunk), 1)
    causal = row >= col
    tri = jnp.where(causal, 1.0, 0.0).astype(BF16)
    hi, mid, lo = _split_bf16(log_a)
    cum = _dot(tri, hi) + _dot(tri, mid) + _dot(tri, lo)
    last = cum[chunk - 1:chunk, :]

    q = q_ref[...] * (dk ** -0.5)
    k = k_ref[...]
    v = v_ref[...]
    v16 = v.astype(BF16)
    st = st_ref[...]

    qt = (q * jnp.exp(cum)).astype(BF16)
    kt = (k * jnp.exp(-cum)).astype(BF16)
    att = jnp.where(causal, _dot_nt(qt, kt), 0.0)
    o = _dot(att.astype(BF16), v16) + _dot_nt(qt, st.astype(BF16))

    kd = (k * jnp.exp(last - cum)).astype(BF16)
    st_new = st * jnp.exp(last) + _dot(v.T.astype(BF16), kd)
    st_ref[...] = st_new

    o_ref[...] = _gla_out(o, gn_ref, gr_ref[...]).astype(o_ref.dtype)

    @pl.when(c == pl.num_programs(2) - 1)
    def _():
        s_ref[0, 0] = st_new.T


def _gla_prompt(z, glr, w2p, bias, gla_norm, *, batch, seq, n_heads, dk, dv, col_q, col_k, col_v,
                col_gr, chunk=256):
    assert seq % chunk == 0
    nc = seq // chunk
    bq, bk, bv, bg = col_q // dk, col_k // dk, col_v // dv, col_gr // dv
    est = 2 * chunk * (2 * dk + 2 * dv + LANE) * 4 + 2 * chunk * dv * 2 + 3 * dk * dv * 4 \
        + 6 * chunk * chunk * 4 + 12 * chunk * dv * 4
    return pl.pallas_call(
        functools.partial(_gla_prompt_kernel, chunk=chunk, dk=dk),
        name="gla_prompt",
        grid=(batch, n_heads, nc),
        in_specs=[
            pl.BlockSpec((chunk, dk), lambda b, h, c: (b * nc + c, bq + h)),
            pl.BlockSpec((chunk, dk), lambda b, h, c: (b * nc + c, bk + h)),
            pl.BlockSpec((chunk, dv), lambda b, h, c: (b * nc + c, bv + h)),
            pl.BlockSpec((chunk, dv), lambda b, h, c: (b * nc + c, bg + h)),
            pl.BlockSpec((chunk, LANE), lambda b, h, c: (b * nc + c, 0)),
            pl.BlockSpec((LANE, dk), lambda b, h, c: (0, h)),
            pl.BlockSpec((1, dk), lambda b, h, c: (0, h)),
            pl.BlockSpec((1, dv), lambda b, h, c: (0, 0)),
        ],
        out_specs=[
            pl.BlockSpec((chunk, dv), lambda b, h, c: (b * nc + c, h)),
            pl.BlockSpec((1, 1, dk, dv), lambda b, h, c: (b, h, 0, 0)),
        ],
        out_shape=[jax.ShapeDtypeStruct((batch * seq, n_heads * dv), BF16),
                   jax.ShapeDtypeStruct((batch, n_heads, dk, dv), F32)],
        scratch_shapes=[pltpu.VMEM((dv, dk), F32)],
        compiler_params=pltpu.CompilerParams(
            dimension_semantics=("parallel", "parallel", "arbitrary"),
            vmem_limit_bytes=_vmem_limit(est)),
    )(z, z, z, z, glr, w2p, bias, gla_norm)


def _gla_sample_kernel(z_ref, glr_ref, s0_ref, w2_ref, b_ref, gn_ref, o_ref, s_ref,
                       *, n_heads, dk, dv, dec, col_q, col_k, col_v, col_gr):
    log_a_all = _log_decay(glr_ref[0], w2_ref, b_ref)
    t_k = lax.broadcasted_iota(jnp.int32, (dec, dk), 0)
    t_v = lax.broadcasted_iota(jnp.int32, (dec, dv), 0)
    eye = (lax.broadcasted_iota(jnp.int32, (dk, dk), 0)
           == lax.broadcasted_iota(jnp.int32, (dk, dk), 1))

    def to_col(r):
        return jnp.sum(jnp.where(eye, jnp.broadcast_to(r, (dk, dk)), 0.0), axis=-1, keepdims=True)

    for h in range(n_heads):
        log_a = log_a_all[:, h * dk:(h + 1) * dk]
        cum = jnp.zeros((dec, dk), F32)
        for t in range(dec):
            cum = cum + jnp.where(t_k >= t, log_a[t:t + 1], 0.0)
        last = cum[dec - 1:dec]
        q = z_ref[0, :, col_q + h * dk: col_q + (h + 1) * dk] * (dk ** -0.5)
        k = z_ref[0, :, col_k + h * dk: col_k + (h + 1) * dk]
        v = z_ref[0, :, col_v + h * dv: col_v + (h + 1) * dv]
        gr = z_ref[0, :, col_gr + h * dv: col_gr + (h + 1) * dv]
        s0 = s0_ref[0, h]

        o = _dot((q * jnp.exp(cum)).astype(BF16), s0.astype(BF16))
        for t in range(dec):
            o_t = jnp.zeros((1, dv), F32)
            for j in range(t + 1):
                w = jnp.exp(cum[t:t + 1] - cum[j:j + 1])
                a = jnp.sum(q[t:t + 1] * k[j:j + 1] * w, axis=-1, keepdims=True)
                o_t = o_t + a * v[j:j + 1]
            o = o + jnp.where(t_v == t, o_t, 0.0)

        kd = k * jnp.exp(last - cum)
        s_new = s0 * to_col(jnp.exp(last))
        for t in range(dec):
            s_new = s_new + to_col(kd[t:t + 1]) * v[t:t + 1]
        s_ref[0, h] = s_new
        o_ref[0, :, h * dv:(h + 1) * dv] = _gla_out(o, gn_ref, gr).astype(o_ref.dtype)


def _gla_sample(z3, glr3, state, w2p, bias, gla_norm, *, n_heads, dk, dv, col_q, col_k, col_v,
                col_gr):
    dec_b, dec, zw = z3.shape
    est = 2 * dec * zw * 4 + 2 * 2 * n_heads * dk * dv * 4 + LANE * n_heads * dk * 2 \
        + 8 * dk * dv * 4
    return pl.pallas_call(
        functools.partial(_gla_sample_kernel, n_heads=n_heads, dk=dk, dv=dv, dec=dec, col_q=col_q,
                          col_k=col_k, col_v=col_v, col_gr=col_gr),
        name="gla_sample",
        grid=(dec_b,),
        in_specs=[
            pl.BlockSpec((1, dec, zw), lambda b: (b, 0, 0)),
            pl.BlockSpec((1, dec, LANE), lambda b: (b, 0, 0)),
            pl.BlockSpec((1, n_heads, dk, dv), lambda b: (b, 0, 0, 0)),
            pl.BlockSpec((LANE, n_heads * dk), lambda b: (0, 0)),
            pl.BlockSpec((1, n_heads * dk), lambda b: (0, 0)),
            pl.BlockSpec((1, dv), lambda b: (0, 0)),
        ],
        out_specs=[
            pl.BlockSpec((1, dec, n_heads * dv), lambda b: (b, 0, 0)),
            pl.BlockSpec((1, n_heads, dk, dv), lambda b: (b, 0, 0, 0)),
        ],
        out_shape=[jax.ShapeDtypeStruct((dec_b, dec, n_heads * dv), BF16),
                   jax.ShapeDtypeStruct((dec_b, n_heads, dk, dv), F32)],
        compiler_params=pltpu.CompilerParams(
            dimension_semantics=("parallel",),
            vmem_limit_bytes=_vmem_limit(est)),
    )(z3, glr3, state, w2p, bias, gla_norm)


def _pad_cols(w, n):
    return jnp.pad(w, ((0, 0), (0, n - w.shape[1])))


def _round_up(x, m):
    return (x + m - 1) // m * m


def kernel(x_prompt, x_sample, cache_k, cache_v, state_gla, page_table, norm_ffn1, ffn1_w_gate, ffn1_w_up, ffn1_w_down, norm_mix, w_in, gla_gate_w2, gla_gate_b, lambda_q1, lambda_k1, lambda_q2, lambda_k2, diff_norm, gla_norm, w_out, norm_ffn2, ffn2_w_gate, ffn2_w_up, ffn2_w_down, norm_final):
    batch, seq, d = x_prompt.shape
    dec_b, dec, _ = x_sample.shape
    depth = norm_ffn1.shape[0]
    n_heads, hd = cache_k.shape[3], cache_k.shape[5]
    vd = cache_v.shape[4]
    g_heads, dk, dv = state_gla.shape[2:]
    rank = gla_gate_w2.shape[1]
    d_ff = ffn1_w_gate.shape[2]

    w_qk = n_heads * 2 * hd
    sizes = (w_qk, w_qk, n_heads * vd, g_heads * dk, g_heads * dk, g_heads * dv, g_heads * dv, rank)
    cols = [0]
    for s in sizes:
        cols.append(cols[-1] + s)
    col_dq, col_dk, col_dv, col_gq, col_gk, col_gv, col_gr, col_glr = cols[:8]
    assert w_in.shape[2] == cols[8] and rank <= LANE
    ff_pad = _round_up(d_ff, 512)
    diff_w = n_heads * vd

    yp = x_prompt.reshape(batch * seq, d)
    ys = x_sample.reshape(dec_b * dec, d)
    nf = norm_final.reshape(1, d)
    outs = [[] for _ in range(6)]

    for layer in range(depth):
        lam_init = _lambda_init(layer)
        ffn_w = []
        for wg, wu, wd in ((ffn1_w_gate, ffn1_w_up, ffn1_w_down), (ffn2_w_gate, ffn2_w_up, ffn2_w_down)):
            ffn_w.append((_pad_cols(wg[layer], ff_pad).astype(BF16),
                          _pad_cols(wu[layer], ff_pad).astype(BF16),
                          jnp.pad(wd[layer], ((0, ff_pad - d_ff), (0, 0))).astype(BF16)))
        w_main = w_in[layer][:, :col_glr].astype(BF16)
        w_glr = _pad_cols(w_in[layer][:, col_glr:], LANE).astype(BF16)
        w2p = jnp.pad(gla_gate_w2[layer], ((0, LANE - rank), (0, 0))).astype(BF16)
        gate_b = gla_gate_b[layer].reshape(1, -1)
        lam_p = jnp.stack([lambda_q1[layer], lambda_k1[layer], lambda_q2[layer], lambda_k2[layer]])
        dn = diff_norm[layer].reshape(1, vd)
        gn = gla_norm[layer].reshape(1, dv)
        wo = w_out[layer].astype(BF16)
        wo_a, wo_b = wo[:diff_w], wo[diff_w:]
        g1 = norm_ffn1[layer].reshape(1, d)
        gm = norm_mix[layer].reshape(1, d)
        g2 = norm_ffn2[layer].reshape(1, d)
        last = layer == depth - 1

        def pre(x):
            x, hmix = _ffn(x, g1, *ffn_w[0], gm, tail="next")
            return (x,) + tuple(_inproj(hmix, w_main, w_glr, n_heads=n_heads, hd=hd, col_k=col_dk,
                                        col_v=col_dv))

        def post(x, mix_a, mix_b):
            x = _outproj(x, mix_a, mix_b, wo_a, wo_b)
            return _ffn(x, g2, *ffn_w[1], nf, tail="final" if last else "none")

        def kv_out(k_rows, v_rows, b, l):
            k = k_rows.reshape(b, l, n_heads, 2, hd)
            v = v_rows.reshape(b, l, 2, n_heads, hd).transpose(0, 1, 3, 2, 4).reshape(b, l, n_heads, vd)
            return k, v

        yp, zp, glr_p, k_rows, v_rows = pre(yp)
        mix_a = _attn_prompt(zp, lam_p, dn, batch=batch, seq=seq, n_heads=n_heads, hd=hd, vd=vd,
                             col_q=col_dq, col_k=col_dk, col_v=col_dv, lam_init=lam_init)
        mix_b, s_p = _gla_prompt(zp, glr_p, w2p, gate_b, gn, batch=batch, seq=seq, n_heads=g_heads,
                                 dk=dk, dv=dv, col_q=col_gq, col_k=col_gk, col_v=col_gv, col_gr=col_gr)
        yp = post(yp, mix_a, mix_b)
        k_new, v_new = kv_out(k_rows, v_rows, batch, seq)
        outs[0].append(k_new)
        outs[1].append(v_new)
        outs[2].append(s_p.astype(state_gla.dtype))

        ys, zs, glr_s, k_rows, v_rows = pre(ys)
        zs3 = zs.reshape(dec_b, dec, col_glr)
        mix_a = _attn_sample(zs3, cache_k, cache_v, page_table, layer, lam_p, dn, n_heads=n_heads,
                             hd=hd, vd=vd, col_q=col_dq, col_k=col_dk, col_v=col_dv,
                             lam_init=lam_init)
        mix_b, s_s = _gla_sample(zs3, glr_s.reshape(dec_b, dec, LANE), state_gla[layer], w2p, gate_b, gn,
                                 n_heads=g_heads, dk=dk, dv=dv, col_q=col_gq, col_k=col_gk,
                                 col_v=col_gv, col_gr=col_gr)
        ys = post(ys, mix_a.reshape(dec_b * dec, -1), mix_b.reshape(dec_b * dec, -1))
        k_new, v_new = kv_out(k_rows, v_rows, dec_b, dec)
        outs[3].append(k_new)
        outs[4].append(v_new)
        outs[5].append(s_s.astype(state_gla.dtype))

    if depth == 0:
        raise ValueError("depth must be positive")
    y_prompt = yp.reshape(batch, seq, d)
    y_sample = ys.reshape(dec_b, dec, d)
    return (y_prompt, y_sample) + tuple(jnp.stack(o) for o in outs)
```

```python
import functools
import math

import jax
import jax.numpy as jnp
from jax import lax
from jax.experimental import pallas as pl
from jax.experimental.pallas import tpu as pltpu

F32 = jnp.float32
BF16 = jnp.bfloat16

NORM_EPS = 1e-6
GLA_TAU = 16.0
LANE = 128
VMEM_PHYSICAL = 64 * 2**20


def _lambda_init(layer):
    return 0.8 - 0.6 * math.exp(-0.3 * layer)


def _vmem_limit(nbytes):
    return int(min(nbytes * 5 // 4 + (4 << 20), VMEM_PHYSICAL - (8 << 20)))


def _rms(x, gain):
    return x * lax.rsqrt(jnp.mean(x * x, axis=-1, keepdims=True) + NORM_EPS) * gain


def _dot(a, b):
    return jnp.dot(a, b, preferred_element_type=F32)


def _dot_nt(a, b):
    return lax.dot_general(a, b, (((1,), (1,)), ((), ())), preferred_element_type=F32)


def _to_bf16_kernel(w_ref, o_ref, *, rows, cols, tr):
    r_out, c_out = o_ref.shape
    c_copy = min(cols, c_out)
    w = w_ref[0][:, :c_copy]
    if rows % tr:
        r_idx = pl.program_id(0) * tr + lax.broadcasted_iota(jnp.int32, (r_out, c_copy), 0)
        w = jnp.where(r_idx < rows, w, 0.0)
    o_ref[:, :c_copy] = w.astype(BF16)
    if c_copy < c_out:
        o_ref[:, c_copy:] = jnp.zeros((r_out, c_out - c_copy), BF16)


def _to_bf16(w, layer, rows_out, cols_out, *, tr=256):
    _, rows, cols = w.shape
    assert rows_out % tr == 0 and rows_out >= rows and min(cols, cols_out) % LANE == 0
    est = 2 * tr * cols * 4 + 2 * tr * cols_out * 2 + tr * cols * 4
    return pl.pallas_call(
        functools.partial(_to_bf16_kernel, rows=rows, cols=cols, tr=tr),
        name="to_bf16",
        grid=(rows_out // tr,),
        in_specs=[pl.BlockSpec((1, tr, cols), lambda i: (layer, i, 0))],
        out_specs=pl.BlockSpec((tr, cols_out), lambda i: (i, 0)),
        out_shape=jax.ShapeDtypeStruct((rows_out, cols_out), BF16),
        compiler_params=pltpu.CompilerParams(
            dimension_semantics=("parallel",),
            vmem_limit_bytes=_vmem_limit(est)),
    )(w)


def _ffn_kernel(x_ref, g_ref, wg_ref, wu_ref, wd_ref, gt_ref, *rest, tail):
    o_ref = rest[0]
    h_ref, acc_ref = rest[-2:]
    j = pl.program_id(1)

    @pl.when(j == 0)
    def _():
        h_ref[...] = _rms(x_ref[...], g_ref[...]).astype(BF16)
        acc_ref[...] = jnp.zeros_like(acc_ref)

    h = h_ref[...]
    gate = _dot(h, wg_ref[...])
    up = _dot(h, wu_ref[...])
    act = (gate * jax.nn.sigmoid(gate)) * up
    acc_ref[...] += _dot(act.astype(BF16), wd_ref[...])

    @pl.when(j == pl.num_programs(1) - 1)
    def _():
        y = x_ref[...] + 0.5 * acc_ref[...]
        if tail == "final":
            y = _rms(y, gt_ref[...])
        o_ref[...] = y
        if tail == "next":
            rest[1][...] = _rms(y, gt_ref[...]).astype(BF16)


def _ffn(x, gain, wg, wu, wd, gain_tail, *, tail, tm=512, tf=512):
    m, d = x.shape
    fpad = wg.shape[1]
    assert m % tm == 0 and fpad % tf == 0 and tail in ("final", "next", "none")
    est = (2 * 2 * tm * d * 4
           + 2 * tm * d * 2
           + tm * d * (2 + 4)
           + 2 * 3 * d * tf * 2
           + 3 * tm * tf * 4)
    row_spec = pl.BlockSpec((tm, d), lambda i, j: (i, 0))
    out_specs, out_shape = row_spec, jax.ShapeDtypeStruct((m, d), F32)
    if tail == "next":
        out_specs, out_shape = [row_spec, row_spec], [out_shape, jax.ShapeDtypeStruct((m, d), BF16)]
    return pl.pallas_call(
        functools.partial(_ffn_kernel, tail=tail),
        name="ffn_" + tail,
        grid=(m // tm, fpad // tf),
        in_specs=[
            row_spec,
            pl.BlockSpec((1, d), lambda i, j: (0, 0)),
            pl.BlockSpec((d, tf), lambda i, j: (0, j)),
            pl.BlockSpec((d, tf), lambda i, j: (0, j)),
            pl.BlockSpec((tf, d), lambda i, j: (j, 0)),
            pl.BlockSpec((1, d), lambda i, j: (0, 0)),
        ],
        out_specs=out_specs,
        out_shape=out_shape,
        scratch_shapes=[pltpu.VMEM((tm, d), BF16), pltpu.VMEM((tm, d), F32)],
        compiler_params=pltpu.CompilerParams(
            dimension_semantics=("parallel", "arbitrary"),
            vmem_limit_bytes=_vmem_limit(est)),
    )(x, gain, wg, wu, wd, gain_tail)


def _inproj_kernel(h_ref, w_ref, wl_ref, z_ref, glr_ref, kr_ref, vr_ref,
                   *, tm, n_heads, hd, k_tile, v_tile):
    j = pl.program_id(1)
    pitch = 2 * n_heads
    z_ref[...] = _dot(h_ref[...], w_ref[...])

    @pl.when(j == 0)
    def _():
        glr_ref[...] = _dot(h_ref[...], wl_ref[...])

    @pl.when(j == k_tile)
    def _():
        for c in range(pitch):
            kr_ref[pl.ds(c, tm, stride=pitch), :] = z_ref[:, c * hd:(c + 1) * hd]

    @pl.when(j == v_tile)
    def _():
        for h in range(n_heads):
            for half in range(2):
                c = 2 * h + half
                vr_ref[pl.ds(n_heads * half + h, tm, stride=pitch), :] = z_ref[:, c * hd:(c + 1) * hd]


def _inproj(h, w, wl, *, n_heads, hd, col_k, col_v, tm=1024):
    m, d = h.shape
    n = w.shape[1]
    tn = n_heads * 2 * hd
    tm = min(tm, m)
    pitch = 2 * n_heads
    assert m % tm == 0 and n % tn == 0 and col_k % tn == 0 and col_v % tn == 0 and hd == LANE
    est = (2 * tm * d * 2 + 2 * d * (tn + LANE) * 2 + 2 * tm * (tn + LANE) * 4
           + 2 * 2 * tm * pitch * hd * 4 + tm * tn * 4)
    return pl.pallas_call(
        functools.partial(_inproj_kernel, tm=tm, n_heads=n_heads, hd=hd, k_tile=col_k // tn,
                          v_tile=col_v // tn),
        name="in_proj",
        grid=(m // tm, n // tn),
        in_specs=[
            pl.BlockSpec((tm, d), lambda i, j: (i, 0)),
            pl.BlockSpec((d, tn), lambda i, j: (0, j)),
            pl.BlockSpec((d, LANE), lambda i, j: (0, 0)),
        ],
        out_specs=[
            pl.BlockSpec((tm, tn), lambda i, j: (i, j)),
            pl.BlockSpec((tm, LANE), lambda i, j: (i, 0)),
            pl.BlockSpec((tm * pitch, hd), lambda i, j: (i, 0)),
            pl.BlockSpec((tm * pitch, hd), lambda i, j: (i, 0)),
        ],
        out_shape=[jax.ShapeDtypeStruct((m, n), F32),
                   jax.ShapeDtypeStruct((m, LANE), F32),
                   jax.ShapeDtypeStruct((m * pitch, hd), F32),
                   jax.ShapeDtypeStruct((m * pitch, hd), F32)],
        compiler_params=pltpu.CompilerParams(
            dimension_semantics=("parallel", "arbitrary"),
            vmem_limit_bytes=_vmem_limit(est)),
    )(h, w, wl)


def _outproj_kernel(x_ref, a_ref, b_ref, wa_ref, wb_ref, o_ref):
    o_ref[...] = x_ref[...] + _dot(a_ref[...], wa_ref[...]) + _dot(b_ref[...], wb_ref[...])


def _outproj(x, mix_a, mix_b, w, *, tm=512):
    m, d = x.shape
    ka, kb = mix_a.shape[1], mix_b.shape[1]
    assert m % tm == 0 and ka == kb and w.shape == (ka + kb, d)
    est = 2 * 2 * tm * d * 4 + 2 * tm * (ka + kb) * 2 + 2 * (ka + kb) * d * 2 + tm * d * 4
    return pl.pallas_call(
        _outproj_kernel,
        name="out_proj",
        grid=(m // tm,),
        in_specs=[
            pl.BlockSpec((tm, d), lambda i: (i, 0)),
            pl.BlockSpec((tm, ka), lambda i: (i, 0)),
            pl.BlockSpec((tm, kb), lambda i: (i, 0)),
            pl.BlockSpec((ka, d), lambda i: (0, 0)),
            pl.BlockSpec((kb, d), lambda i: (1, 0)),
        ],
        out_specs=pl.BlockSpec((tm, d), lambda i: (i, 0)),
        out_shape=jax.ShapeDtypeStruct((m, d), F32),
        compiler_params=pltpu.CompilerParams(
            dimension_semantics=("parallel",),
            vmem_limit_bytes=_vmem_limit(est)),
    )(x, mix_a, mix_b, w, w)


def _diff_lambda(lam_ref, lam_init):
    lp = lam_ref[...]
    e1 = jnp.exp(jnp.sum(lp[0:1] * lp[1:2], axis=-1, keepdims=True))
    e2 = jnp.exp(jnp.sum(lp[2:3] * lp[3:4], axis=-1, keepdims=True))
    return e1 - e2 + lam_init


def _alibi_slope(h, n_heads):
    slope = F32(0.0)
    for i in range(n_heads):
        slope = jnp.where(h == i, F32(2.0 ** (-8.0 * (i + 1) / n_heads)), slope)
    return slope


def _lane_fold(x, op):
    parts = [x[:, i * LANE:(i + 1) * LANE] for i in range(x.shape[1] // LANE)]
    return functools.reduce(op, parts)


def _attn_prompt_kernel(q_ref, k_ref, v_ref, lam_ref, dn_ref, o_ref, s_ref, m_ref, l_ref, acc_ref,
                        *, tq, tk, hd, n_heads, lam_init):
    h = pl.program_id(1)
    qi = pl.program_id(2)
    scale = hd ** -0.5
    slope = _alibi_slope(h, n_heads)
    n_diag = tq // tk
    n_full = qi * n_diag
    q = q_ref[...]
    zero = jnp.zeros((tq, hd), F32)
    qbd = jnp.concatenate([jnp.concatenate([q[:, :hd], zero], axis=1),
                           jnp.concatenate([zero, q[:, hd:]], axis=1)], axis=0).astype(BF16)
    row = lax.broadcasted_iota(jnp.int32, (tq, tk), 0)
    col = lax.broadcasted_iota(jnp.int32, (tq, tk), 1)
    d0 = (row - col).astype(F32)

    m_ref[...] = jnp.full_like(m_ref, -jnp.inf)

    def scores(c, masked):
        ks = pl.multiple_of(c * tk, tk)
        kc = k_ref[pl.ds(ks, tk), :].astype(BF16)
        off = jnp.full((1, 1), qi * tq - c * tk, jnp.int32).astype(F32)
        dist = d0 + off
        bias = slope * dist
        s = _dot_nt(qbd, kc) * scale - jnp.concatenate([bias, bias], axis=0)
        if masked:
            s = jnp.where(jnp.concatenate([dist, dist], axis=0) >= 0, s, -jnp.inf)
        s_ref[c] = s
        m_ref[...] = jnp.maximum(m_ref[...], _lane_fold(s, jnp.maximum))

    def scores_body(c, carry):
        scores(c, False)
        return carry

    lax.fori_loop(0, n_full, scores_body, 0)
    for j in range(n_diag):
        scores(n_full + j, True)

    m_row = jnp.max(m_ref[...], axis=-1, keepdims=True)
    m_ref[...] = jnp.broadcast_to(m_row, m_ref.shape)
    l_ref[...] = jnp.zeros_like(l_ref)
    acc_ref[...] = jnp.zeros_like(acc_ref)

    def pv_body(c, carry):
        ks = pl.multiple_of(c * tk, tk)
        mb = m_ref[...]
        p = jnp.exp(s_ref[c] - jnp.concatenate([mb] * (tk // LANE), axis=1))
        l_ref[...] += _lane_fold(p, jnp.add)
        acc_ref[...] += _dot(p.astype(BF16), v_ref[pl.ds(ks, tk), :].astype(BF16))
        return carry

    lax.fori_loop(0, n_full + n_diag, pv_body, 0)

    lam = _diff_lambda(lam_ref, lam_init)
    on = acc_ref[...] / jnp.sum(l_ref[...], axis=-1, keepdims=True)
    o = on[:tq] - lam * on[tq:]
    o_ref[...] = (_rms(o, dn_ref[...]) * (1.0 - lam_init)).astype(o_ref.dtype)


def _attn_prompt(z, lam_p, diff_norm, *, batch, seq, n_heads, hd, vd, col_q, col_k, col_v,
                 lam_init, tq=512, tk=256):
    assert vd == 2 * hd and seq % tq == 0 and tq % tk == 0 and tk % LANE == 0
    nq = seq // tq
    cw = 2 * hd
    bq, bk, bv = col_q // cw, col_k // cw, col_v // cw
    est = (2 * tq * cw * 4 + 2 * 2 * seq * cw * 4 + 2 * tq * vd * 2
           + (seq // tk) * 2 * tq * tk * 4 + 2 * 2 * tq * LANE * 4 + 2 * tq * vd * 4
           + 6 * 2 * tq * tk * 4)
    return pl.pallas_call(
        functools.partial(_attn_prompt_kernel, tq=tq, tk=tk, hd=hd, n_heads=n_heads,
                          lam_init=lam_init),
        name="attn_prompt",
        grid=(batch, n_heads, nq),
        in_specs=[
            pl.BlockSpec((tq, cw), lambda b, h, i: (b * nq + i, bq + h)),
            pl.BlockSpec((seq, cw), lambda b, h, i: (b, bk + h)),
            pl.BlockSpec((seq, cw), lambda b, h, i: (b, bv + h)),
            pl.BlockSpec((4, hd), lambda b, h, i: (0, 0)),
            pl.BlockSpec((1, vd), lambda b, h, i: (0, 0)),
        ],
        out_specs=pl.BlockSpec((tq, vd), lambda b, h, i: (b * nq + i, h)),
        out_shape=jax.ShapeDtypeStruct((batch * seq, n_heads * vd), BF16),
        scratch_shapes=[pltpu.VMEM((seq // tk, 2 * tq, tk), F32),
                        pltpu.VMEM((2 * tq, LANE), F32), pltpu.VMEM((2 * tq, LANE), F32),
                        pltpu.VMEM((2 * tq, vd), F32)],
        compiler_params=pltpu.CompilerParams(
            dimension_semantics=("parallel", "parallel", "arbitrary"),
            vmem_limit_bytes=_vmem_limit(est)),
    )(z, z, z, lam_p, diff_norm)


def _attn_sample_kernel(*refs, n_pages, page, n_heads, hd, dec, col_q, col_k, col_v, lam_init):
    z_ref = refs[1]
    kp_refs = refs[2:2 + n_pages]
    vp_refs = refs[2 + n_pages:2 + 2 * n_pages]
    pitch = 2 * n_heads
    lam_ref, dn_ref, o_ref, q2_ref, kn_ref, vn_ref = refs[2 + 2 * n_pages:]
    cw = 2 * hd
    rows = 2 * dec
    scale = hd ** -0.5
    past = n_pages * page
    n_keys = past + page
    r_iota = lax.broadcasted_iota(jnp.int32, (rows, n_keys), 0)
    j_iota = lax.broadcasted_iota(jnp.int32, (rows, n_keys), 1)
    q_idx = jnp.where(r_iota >= dec, r_iota - dec, r_iota)
    dist = (past + q_idx - j_iota).astype(F32)
    valid = j_iota <= past + q_idx
    rr = lax.broadcasted_iota(jnp.int32, (rows, cw), 0)
    cc = lax.broadcasted_iota(jnp.int32, (rows, cw), 1)
    keep = (rr < dec) == (cc < hd)
    lam = _diff_lambda(lam_ref, lam_init)

    width = n_heads * cw
    kn_ref[...] = jnp.zeros_like(kn_ref)
    vn_ref[...] = jnp.zeros_like(vn_ref)
    kn_ref[0:dec, :] = z_ref[0, :, col_k:col_k + width]
    vn_ref[0:dec, :] = z_ref[0, :, col_v:col_v + width]

    for h in range(n_heads):
        slope = 2.0 ** (-8.0 * (h + 1) / n_heads)
        qh = z_ref[0, :, col_q + h * cw: col_q + (h + 1) * cw]
        q2_ref[0:dec, :] = qh
        q2_ref[dec:rows, :] = qh
        qbd = jnp.where(keep, q2_ref[...], 0.0).astype(BF16)
        parts = []
        for kp in kp_refs:
            kcat = jnp.concatenate([kp[0, pl.ds(2 * h + m, page, stride=pitch), :] for m in range(2)],
                                   axis=-1)
            parts.append(_dot_nt(qbd, kcat.astype(BF16)))
        parts.append(_dot_nt(qbd, kn_ref[:, h * cw:(h + 1) * cw].astype(BF16)))
        s = jnp.concatenate(parts, axis=-1) * scale - slope * dist
        s = jnp.where(valid, s, -jnp.inf)
        p = jnp.exp(s - jnp.max(s, axis=-1, keepdims=True))
        l = jnp.sum(p, axis=-1, keepdims=True)
        p16 = p.astype(BF16)
        acc = _dot(p16[:, past:], vn_ref[:, h * cw:(h + 1) * cw].astype(BF16))
        for i, vp in enumerate(vp_refs):
            vcat = jnp.concatenate(
                [vp[0, pl.ds(n_heads * half + h, page, stride=pitch), :] for half in range(2)], axis=-1)
            acc = acc + _dot(p16[:, i * page:(i + 1) * page], vcat.astype(BF16))
        on = acc / l
        o = on[:dec] - lam * on[dec:]
        o_ref[0, :, h * cw:(h + 1) * cw] = (_rms(o, dn_ref[...]) * (1.0 - lam_init)).astype(o_ref.dtype)


def _attn_sample(z3, cache_k, cache_v, page_table, layer, lam_p, diff_norm, *, n_heads, hd, vd,
                 col_q, col_k, col_v, lam_init):
    dec_b, dec, zw = z3.shape
    n_pages = page_table.shape[1]
    depth, n_pool, page = cache_k.shape[:3]
    assert vd == 2 * hd
    width = n_heads * 2 * hd
    prow = page * n_heads * 2
    kf = cache_k.reshape(depth * n_pool, prow, hd)
    vf = cache_v.reshape(depth * n_pool, page, n_heads, 2, hd).transpose(0, 1, 3, 2, 4).reshape(
        depth * n_pool, prow, hd)
    pt = (page_table + layer * n_pool).reshape(-1).astype(jnp.int32)
    rows = 2 * dec
    n_keys = (n_pages + 1) * page

    def page_spec(i):
        return pl.BlockSpec((1, prow, hd), lambda b, pt_ref: (pt_ref[b * n_pages + i], 0, 0))

    est = (2 * dec * zw * 4 + 2 * 2 * n_pages * page * width * 4 + 2 * dec * width * 2
           + rows * 2 * hd * 4 + 2 * page * width * 4 + 8 * rows * n_keys * 4)
    grid_spec = pltpu.PrefetchScalarGridSpec(
        num_scalar_prefetch=1,
        grid=(dec_b,),
        in_specs=([pl.BlockSpec((1, dec, zw), lambda b, pt_ref: (b, 0, 0))]
                  + [page_spec(i) for i in range(n_pages)]
                  + [page_spec(i) for i in range(n_pages)]
                  + [pl.BlockSpec((4, hd), lambda b, pt_ref: (0, 0)),
                     pl.BlockSpec((1, vd), lambda b, pt_ref: (0, 0))]),
        out_specs=pl.BlockSpec((1, dec, n_heads * vd), lambda b, pt_ref: (b, 0, 0)),
        scratch_shapes=[pltpu.VMEM((rows, 2 * hd), F32),
                        pltpu.VMEM((page, width), F32),
                        pltpu.VMEM((page, width), F32)],
    )
    return pl.pallas_call(
        functools.partial(_attn_sample_kernel, n_pages=n_pages, page=page, n_heads=n_heads, hd=hd,
                          dec=dec, col_q=col_q, col_k=col_k, col_v=col_v, lam_init=lam_init),
        name="attn_sample",
        grid_spec=grid_spec,
        out_shape=jax.ShapeDtypeStruct((dec_b, dec, n_heads * vd), BF16),
        compiler_params=pltpu.CompilerParams(
            dimension_semantics=("parallel",),
            vmem_limit_bytes=_vmem_limit(est)),
    )(pt, z3, *([kf] * n_pages), *([vf] * n_pages), lam_p, diff_norm)


def _log_decay(glr, w2_ref, b_ref):
    x = _dot(glr.astype(BF16), w2_ref[...]) + b_ref[...]
    return (jnp.minimum(x, 0.0) - jnp.log(1.0 + jnp.exp(-jnp.abs(x)))) / GLA_TAU


def _split_bf16(x):
    hi = x.astype(BF16)
    r1 = x - hi.astype(F32)
    mid = r1.astype(BF16)
    lo = (r1 - mid.astype(F32)).astype(BF16)
    return hi, mid, lo


def _gla_out(o, gn_ref, gr):
    return _rms(o, gn_ref[...]) * (gr * jax.nn.sigmoid(gr))


GLA_MAX_CHUNK_DECAY = 60.0


def _gla_prompt_kernel(q_ref, k_ref, v_ref, gr_ref, glr_ref, w2_ref, b_ref, gn_ref,
                       o_ref, s_ref, st_ref, la_ref, of_ref, *, chunk, n_heads, dk, dv):
    c = pl.program_id(1)
    scale = dk ** -0.5

    @pl.when(c == 0)
    def _():
        st_ref[...] = jnp.zeros_like(st_ref)

    log_a = _log_decay(glr_ref[...], w2_ref, b_ref)
    row = lax.broadcasted_iota(jnp.int32, (chunk, chunk), 0)
    col = lax.broadcasted_iota(jnp.int32, (chunk, chunk), 1)
    causal = row >= col
    tri = jnp.where(causal, 1.0, 0.0).astype(BF16)
    hi, mid, lo = _split_bf16(log_a)
    cum_all = _dot(tri, hi) + _dot(tri, mid) + _dot(tri, lo)
    steep = jnp.max(-cum_all[chunk - 1:chunk, :]) > GLA_MAX_CHUNK_DECAY

    @pl.when(jnp.logical_not(steep))
    def _():
        for h in range(n_heads):
            cum = cum_all[:, h * dk:(h + 1) * dk]
            last = cum[chunk - 1:chunk, :]
            q = q_ref[:, h * dk:(h + 1) * dk] * scale
            k = k_ref[:, h * dk:(h + 1) * dk]
            v = v_ref[:, h * dv:(h + 1) * dv]
            st = st_ref[h]
            qt = (q * jnp.exp(cum)).astype(BF16)
            kt = (k * jnp.exp(-cum)).astype(BF16)
            att = jnp.where(causal, _dot_nt(qt, kt), 0.0)
            of_ref[:, h * dv:(h + 1) * dv] = (_dot(att.astype(BF16), v.astype(BF16))
                                              + _dot_nt(qt, st.astype(BF16)))
            kd = (k * jnp.exp(last - cum)).astype(BF16)
            st_ref[h] = st * jnp.exp(last) + _dot(v.T.astype(BF16), kd)

    @pl.when(steep)
    def _():
        la_ref[...] = log_a
        eye = (lax.broadcasted_iota(jnp.int32, (dv, dv), 0)
               == lax.broadcasted_iota(jnp.int32, (dv, dv), 1))

        sub = lax.broadcasted_iota(jnp.int32, (8, dv), 0)

        def tokens(g, carry):
            r0 = pl.multiple_of(g * 8, 8)
            a8 = jnp.exp(la_ref[pl.ds(r0, 8), :])
            q8 = q_ref[pl.ds(r0, 8), :] * scale
            k8 = k_ref[pl.ds(r0, 8), :]
            v8 = v_ref[pl.ds(r0, 8), :]
            for h in range(n_heads):
                st = st_ref[h]
                o8 = jnp.zeros((8, dv), F32)
                for r in range(8):
                    v_t = v8[r:r + 1, h * dv:(h + 1) * dv]
                    v_col = jnp.sum(jnp.where(eye, jnp.broadcast_to(v_t, (dv, dv)), 0.0),
                                    axis=-1, keepdims=True)
                    st = st * a8[r:r + 1, h * dk:(h + 1) * dk] + v_col * k8[r:r + 1, h * dk:(h + 1) * dk]
                    o_col = jnp.sum(st * q8[r:r + 1, h * dk:(h + 1) * dk], axis=-1, keepdims=True)
                    o_row = jnp.sum(jnp.where(eye, jnp.broadcast_to(o_col, (dv, dv)), 0.0),
                                    axis=0, keepdims=True)
                    o8 = jnp.where(sub == r, o_row, o8)
                st_ref[h] = st
                of_ref[pl.ds(r0, 8), h * dv:(h + 1) * dv] = o8
            return carry

        lax.fori_loop(0, chunk // 8, tokens, 0)

    for h in range(n_heads):
        o_ref[:, h * dv:(h + 1) * dv] = _gla_out(
            of_ref[:, h * dv:(h + 1) * dv], gn_ref, gr_ref[:, h * dv:(h + 1) * dv]).astype(o_ref.dtype)

    @pl.when(c == pl.num_programs(1) - 1)
    def _():
        for h in range(n_heads):
            s_ref[0, h] = st_ref[h].T


def _gla_prompt(z, glr, w2p, bias, gla_norm, *, batch, seq, n_heads, dk, dv, col_q, col_k, col_v,
                col_gr, chunk=256):
    assert seq % chunk == 0
    nc = seq // chunk
    wk, wv = n_heads * dk, n_heads * dv
    assert col_q % wk == 0 and col_k % wk == 0 and col_v % wv == 0 and col_gr % wv == 0
    bq, bk, bv, bg = col_q // wk, col_k // wk, col_v // wv, col_gr // wv
    est = (2 * chunk * (2 * wk + 2 * wv + LANE) * 4 + 2 * chunk * wv * 2 + 3 * n_heads * dk * dv * 4
           + chunk * (wk + wv) * 4 + 4 * chunk * wk * 4 + 8 * chunk * chunk * 4 + 12 * chunk * dv * 4)
    return pl.pallas_call(
        functools.partial(_gla_prompt_kernel, chunk=chunk, n_heads=n_heads, dk=dk, dv=dv),
        name="gla_prompt",
        grid=(batch, nc),
        in_specs=[
            pl.BlockSpec((chunk, wk), lambda b, c: (b * nc + c, bq)),
            pl.BlockSpec((chunk, wk), lambda b, c: (b * nc + c, bk)),
            pl.BlockSpec((chunk, wv), lambda b, c: (b * nc + c, bv)),
            pl.BlockSpec((chunk, wv), lambda b, c: (b * nc + c, bg)),
            pl.BlockSpec((chunk, LANE), lambda b, c: (b * nc + c, 0)),
            pl.BlockSpec((LANE, wk), lambda b, c: (0, 0)),
            pl.BlockSpec((1, wk), lambda b, c: (0, 0)),
            pl.BlockSpec((1, dv), lambda b, c: (0, 0)),
        ],
        out_specs=[
            pl.BlockSpec((chunk, wv), lambda b, c: (b * nc + c, 0)),
            pl.BlockSpec((1, n_heads, dk, dv), lambda b, c: (b, 0, 0, 0)),
        ],
        out_shape=[jax.ShapeDtypeStruct((batch * seq, wv), BF16),
                   jax.ShapeDtypeStruct((batch, n_heads, dk, dv), F32)],
        scratch_shapes=[pltpu.VMEM((n_heads, dv, dk), F32),
                        pltpu.VMEM((chunk, wk), F32),
                        pltpu.VMEM((chunk, wv), F32)],
        compiler_params=pltpu.CompilerParams(
            dimension_semantics=("parallel", "arbitrary"),
            vmem_limit_bytes=_vmem_limit(est)),
    )(z, z, z, z, glr, w2p, bias, gla_norm)


def _gla_sample_kernel(z_ref, glr_ref, s0_ref, w2_ref, b_ref, gn_ref, o_ref, s_ref,
                       *, n_heads, dk, dv, dec, col_q, col_k, col_v, col_gr):
    log_a_all = _log_decay(glr_ref[0], w2_ref, b_ref)
    t_k = lax.broadcasted_iota(jnp.int32, (dec, dk), 0)
    t_v = lax.broadcasted_iota(jnp.int32, (dec, dv), 0)
    eye = (lax.broadcasted_iota(jnp.int32, (dk, dk), 0)
           == lax.broadcasted_iota(jnp.int32, (dk, dk), 1))

    def to_col(r):
        return jnp.sum(jnp.where(eye, jnp.broadcast_to(r, (dk, dk)), 0.0), axis=-1, keepdims=True)

    for h in range(n_heads):
        log_a = log_a_all[:, h * dk:(h + 1) * dk]
        cum = jnp.zeros((dec, dk), F32)
        for t in range(dec):
            cum = cum + jnp.where(t_k >= t, log_a[t:t + 1], 0.0)
        last = cum[dec - 1:dec]
        q = z_ref[0, :, col_q + h * dk: col_q + (h + 1) * dk] * (dk ** -0.5)
        k = z_ref[0, :, col_k + h * dk: col_k + (h + 1) * dk]
        v = z_ref[0, :, col_v + h * dv: col_v + (h + 1) * dv]
        gr = z_ref[0, :, col_gr + h * dv: col_gr + (h + 1) * dv]
        s0 = s0_ref[0, h]

        o = _dot((q * jnp.exp(cum)).astype(BF16), s0.astype(BF16))
        for t in range(dec):
            o_t = jnp.zeros((1, dv), F32)
            for j in range(t + 1):
                w = jnp.exp(cum[t:t + 1] - cum[j:j + 1])
                a = jnp.sum(q[t:t + 1] * k[j:j + 1] * w, axis=-1, keepdims=True)
                o_t = o_t + a * v[j:j + 1]
            o = o + jnp.where(t_v == t, o_t, 0.0)

        kd = (k * jnp.exp(last - cum)).astype(BF16)
        upd = lax.dot_general(kd, v.astype(BF16), (((0,), (0,)), ((), ())),
                              preferred_element_type=F32)
        s_ref[0, h] = s0 * to_col(jnp.exp(last)) + upd
        o_ref[0, :, h * dv:(h + 1) * dv] = _gla_out(o, gn_ref, gr).astype(o_ref.dtype)


def _gla_sample(z3, glr3, state, w2p, bias, gla_norm, *, n_heads, dk, dv, col_q, col_k, col_v,
                col_gr):
    dec_b, dec, zw = z3.shape
    est = 2 * dec * zw * 4 + 2 * 2 * n_heads * dk * dv * 4 + LANE * n_heads * dk * 2 \
        + 8 * dk * dv * 4
    return pl.pallas_call(
        functools.partial(_gla_sample_kernel, n_heads=n_heads, dk=dk, dv=dv, dec=dec, col_q=col_q,
                          col_k=col_k, col_v=col_v, col_gr=col_gr),
        name="gla_sample",
        grid=(dec_b,),
        in_specs=[
            pl.BlockSpec((1, dec, zw), lambda b: (b, 0, 0)),
            pl.BlockSpec((1, dec, LANE), lambda b: (b, 0, 0)),
            pl.BlockSpec((1, n_heads, dk, dv), lambda b: (b, 0, 0, 0)),
            pl.BlockSpec((LANE, n_heads * dk), lambda b: (0, 0)),
            pl.BlockSpec((1, n_heads * dk), lambda b: (0, 0)),
            pl.BlockSpec((1, dv), lambda b: (0, 0)),
        ],
        out_specs=[
            pl.BlockSpec((1, dec, n_heads * dv), lambda b: (b, 0, 0)),
            pl.BlockSpec((1, n_heads, dk, dv), lambda b: (b, 0, 0, 0)),
        ],
        out_shape=[jax.ShapeDtypeStruct((dec_b, dec, n_heads * dv), BF16),
                   jax.ShapeDtypeStruct((dec_b, n_heads, dk, dv), F32)],
        compiler_params=pltpu.CompilerParams(
            dimension_semantics=("parallel",),
            vmem_limit_bytes=_vmem_limit(est)),
    )(z3, glr3, state, w2p, bias, gla_norm)


def _pad_cols(w, n):
    return jnp.pad(w, ((0, 0), (0, n - w.shape[1])))


def _round_up(x, m):
    return (x + m - 1) // m * m


def kernel(x_prompt, x_sample, cache_k, cache_v, state_gla, page_table, norm_ffn1, ffn1_w_gate, ffn1_w_up, ffn1_w_down, norm_mix, w_in, gla_gate_w2, gla_gate_b, lambda_q1, lambda_k1, lambda_q2, lambda_k2, diff_norm, gla_norm, w_out, norm_ffn2, ffn2_w_gate, ffn2_w_up, ffn2_w_down, norm_final):
    batch, seq, d = x_prompt.shape
    dec_b, dec, _ = x_sample.shape
    depth = norm_ffn1.shape[0]
    n_heads, hd = cache_k.shape[3], cache_k.shape[5]
    vd = cache_v.shape[4]
    g_heads, dk, dv = state_gla.shape[2:]
    rank = gla_gate_w2.shape[1]
    d_ff = ffn1_w_gate.shape[2]

    w_qk = n_heads * 2 * hd
    sizes = (w_qk, w_qk, n_heads * vd, g_heads * dk, g_heads * dk, g_heads * dv, g_heads * dv, rank)
    cols = [0]
    for s in sizes:
        cols.append(cols[-1] + s)
    col_dq, col_dk, col_dv, col_gq, col_gk, col_gv, col_gr, col_glr = cols[:8]
    assert w_in.shape[2] == cols[8] and rank <= LANE
    ff_pad = _round_up(d_ff, 512)
    diff_w = n_heads * vd

    yp = x_prompt.reshape(batch * seq, d)
    ys = x_sample.reshape(dec_b * dec, d)
    nf = norm_final.reshape(1, d)
    outs = [[] for _ in range(6)]

    for layer in range(depth):
        lam_init = _lambda_init(layer)
        ffn_w = []
        for wg, wu, wd in ((ffn1_w_gate, ffn1_w_up, ffn1_w_down), (ffn2_w_gate, ffn2_w_up, ffn2_w_down)):
            ffn_w.append((_to_bf16(wg, layer, d, ff_pad), _to_bf16(wu, layer, d, ff_pad),
                          _to_bf16(wd, layer, ff_pad, d)))
        w_main = _to_bf16(w_in, layer, d, col_glr)
        w_glr = _pad_cols(w_in[layer][:, col_glr:], LANE).astype(BF16)
        w2p = jnp.pad(gla_gate_w2[layer], ((0, LANE - rank), (0, 0))).astype(BF16)
        gate_b = gla_gate_b[layer].reshape(1, -1)
        lam_p = jnp.stack([lambda_q1[layer], lambda_k1[layer], lambda_q2[layer], lambda_k2[layer]])
        dn = diff_norm[layer].reshape(1, vd)
        gn = gla_norm[layer].reshape(1, dv)
        wo = _to_bf16(w_out, layer, w_out.shape[1], d)
        g1 = norm_ffn1[layer].reshape(1, d)
        gm = norm_mix[layer].reshape(1, d)
        g2 = norm_ffn2[layer].reshape(1, d)
        last = layer == depth - 1

        def pre(x):
            x, hmix = _ffn(x, g1, *ffn_w[0], gm, tail="next")
            return (x,) + tuple(_inproj(hmix, w_main, w_glr, n_heads=n_heads, hd=hd, col_k=col_dk,
                                        col_v=col_dv))

        def post(x, mix_a, mix_b):
            x = _outproj(x, mix_a, mix_b, wo)
            return _ffn(x, g2, *ffn_w[1], nf, tail="final" if last else "none")

        def kv_out(k_rows, v_rows, b, l):
            k = k_rows.reshape(b, l, n_heads, 2, hd)
            v = v_rows.reshape(b, l, 2, n_heads, hd).transpose(0, 1, 3, 2, 4).reshape(b, l, n_heads, vd)
            return k, v

        yp, zp, glr_p, k_rows, v_rows = pre(yp)
        mix_a = _attn_prompt(zp, lam_p, dn, batch=batch, seq=seq, n_heads=n_heads, hd=hd, vd=vd,
                             col_q=col_dq, col_k=col_dk, col_v=col_dv, lam_init=lam_init)
        mix_b, s_p = _gla_prompt(zp, glr_p, w2p, gate_b, gn, batch=batch, seq=seq, n_heads=g_heads,
                                 dk=dk, dv=dv, col_q=col_gq, col_k=col_gk, col_v=col_gv, col_gr=col_gr)
        yp = post(yp, mix_a, mix_b)
        k_new, v_new = kv_out(k_rows, v_rows, batch, seq)
        outs[0].append(k_new)
        outs[1].append(v_new)
        outs[2].append(s_p.astype(state_gla.dtype))

        ys, zs, glr_s, k_rows, v_rows = pre(ys)
        zs3 = zs.reshape(dec_b, dec, col_glr)
        mix_a = _attn_sample(zs3, cache_k, cache_v, page_table, layer, lam_p, dn, n_heads=n_heads,
                             hd=hd, vd=vd, col_q=col_dq, col_k=col_dk, col_v=col_dv,
                             lam_init=lam_init)
        mix_b, s_s = _gla_sample(zs3, glr_s.reshape(dec_b, dec, LANE), state_gla[layer], w2p, gate_b, gn,
                                 n_heads=g_heads, dk=dk, dv=dv, col_q=col_gq, col_k=col_gk,
                                 col_v=col_gv, col_gr=col_gr)
        ys = post(ys, mix_a.reshape(dec_b * dec, -1), mix_b.reshape(dec_b * dec, -1))
        k_new, v_new = kv_out(k_rows, v_rows, dec_b, dec)
        outs[3].append(k_new)
        outs[4].append(v_new)
        outs[5].append(s_s.astype(state_gla.dtype))

    if depth == 0:
        raise ValueError("depth must be positive")
    y_prompt = yp.reshape(batch, seq, d)
    y_sample = ys.reshape(dec_b, dec, d)
    return (y_prompt, y_sample) + tuple(jnp.stack(o) for o in outs)
```

```python
import functools
import math

import jax
import jax.numpy as jnp
from jax import lax
from jax.experimental import pallas as pl
from jax.experimental.pallas import tpu as pltpu

F32 = jnp.float32
BF16 = jnp.bfloat16

NORM_EPS = 1e-6
GLA_TAU = 16.0
LANE = 128
VMEM_PHYSICAL = 64 * 2**20


def _lambda_init(layer):
    return 0.8 - 0.6 * math.exp(-0.3 * layer)


def _vmem_limit(nbytes):
    return int(min(nbytes * 5 // 4 + (4 << 20), VMEM_PHYSICAL - (8 << 20)))


def _rms(x, gain):
    return x * lax.rsqrt(jnp.mean(x * x, axis=-1, keepdims=True) + NORM_EPS) * gain


def _dot(a, b):
    return jnp.dot(a, b, preferred_element_type=F32)


def _dot_nt(a, b):
    return lax.dot_general(a, b, (((1,), (1,)), ((), ())), preferred_element_type=F32)


def _to_bf16_kernel(w_ref, o_ref, *, rows, cols, tr):
    r_out, c_out = o_ref.shape
    c_copy = min(cols, c_out)
    w = w_ref[0][:, :c_copy]
    if rows % tr:
        r_idx = pl.program_id(0) * tr + lax.broadcasted_iota(jnp.int32, (r_out, c_copy), 0)
        w = jnp.where(r_idx < rows, w, 0.0)
    o_ref[:, :c_copy] = w.astype(BF16)
    if c_copy < c_out:
        o_ref[:, c_copy:] = jnp.zeros((r_out, c_out - c_copy), BF16)


def _to_bf16(w, layer, rows_out, cols_out, *, tr=256):
    _, rows, cols = w.shape
    assert rows_out % tr == 0 and rows_out >= rows and min(cols, cols_out) % LANE == 0
    est = 2 * tr * cols * 4 + 2 * tr * cols_out * 2 + tr * cols * 4
    return pl.pallas_call(
        functools.partial(_to_bf16_kernel, rows=rows, cols=cols, tr=tr),
        name="to_bf16",
        grid=(rows_out // tr,),
        in_specs=[pl.BlockSpec((1, tr, cols), lambda i: (layer, i, 0))],
        out_specs=pl.BlockSpec((tr, cols_out), lambda i: (i, 0)),
        out_shape=jax.ShapeDtypeStruct((rows_out, cols_out), BF16),
        compiler_params=pltpu.CompilerParams(
            dimension_semantics=("parallel",),
            vmem_limit_bytes=_vmem_limit(est)),
    )(w)


def _w_in_prep_kernel(wt_ref, wm_ref, wl_ref, *, n_main_blocks, rank, tr):
    j = pl.program_id(0)

    @pl.when(j < n_main_blocks)
    def _():
        wm_ref[...] = wt_ref[0].T.astype(BF16)

    @pl.when(j == n_main_blocks)
    def _():
        tail = wt_ref[0][:LANE]
        r_idx = lax.broadcasted_iota(jnp.int32, tail.shape, 0)
        wl_ref[...] = jnp.where(r_idx < rank, tail, 0.0).T.astype(BF16)


def _w_in_prep(w_in, layer, n_main, *, tr=512):
    wt = jnp.swapaxes(w_in, 1, 2)
    _, n, d = wt.shape
    rank = n - n_main
    nb = n_main // tr
    assert n_main % tr == 0 and 0 < rank <= LANE <= tr
    est = 2 * tr * d * 4 + 2 * d * (tr + LANE) * 2 + 2 * tr * d * 4
    return pl.pallas_call(
        functools.partial(_w_in_prep_kernel, n_main_blocks=nb, rank=rank, tr=tr),
        name="w_in_prep",
        grid=(nb + 1,),
        in_specs=[pl.BlockSpec((1, tr, d), lambda j: (layer, j, 0))],
        out_specs=[pl.BlockSpec((d, tr), lambda j: (0, jnp.minimum(j, nb - 1))),
                   pl.BlockSpec((d, LANE), lambda j: (0, 0))],
        out_shape=[jax.ShapeDtypeStruct((d, n_main), BF16), jax.ShapeDtypeStruct((d, LANE), BF16)],
        compiler_params=pltpu.CompilerParams(
            dimension_semantics=("arbitrary",),
            vmem_limit_bytes=_vmem_limit(est)),
    )(wt)


def _ffn_kernel(x_ref, g_ref, wg_ref, wu_ref, wd_ref, gt_ref, *rest, tail):
    o_ref = rest[0]
    h_ref, acc_ref = rest[-2:]
    j = pl.program_id(1)

    @pl.when(j == 0)
    def _():
        h_ref[...] = _rms(x_ref[...], g_ref[...]).astype(BF16)
        acc_ref[...] = jnp.zeros_like(acc_ref)

    h = h_ref[...]
    gate = _dot(h, wg_ref[...])
    up = _dot(h, wu_ref[...])
    act = (gate * jax.nn.sigmoid(gate)) * up
    acc_ref[...] += _dot(act.astype(BF16), wd_ref[...])

    @pl.when(j == pl.num_programs(1) - 1)
    def _():
        y = x_ref[...] + 0.5 * acc_ref[...]
        if tail == "final":
            y = _rms(y, gt_ref[...])
        o_ref[...] = y
        if tail == "next":
            rest[1][...] = _rms(y, gt_ref[...]).astype(BF16)


def _ffn(x, gain, wg, wu, wd, gain_tail, *, tail, tm=512, tf=512):
    m, d = x.shape
    fpad = wg.shape[1]
    assert m % tm == 0 and fpad % tf == 0 and tail in ("final", "next", "none")
    est = (2 * 2 * tm * d * 4
           + 2 * tm * d * 2
           + tm * d * (2 + 4)
           + 2 * 3 * d * tf * 2
           + 3 * tm * tf * 4)
    row_spec = pl.BlockSpec((tm, d), lambda i, j: (i, 0))
    out_specs, out_shape = row_spec, jax.ShapeDtypeStruct((m, d), F32)
    if tail == "next":
        out_specs, out_shape = [row_spec, row_spec], [out_shape, jax.ShapeDtypeStruct((m, d), BF16)]
    return pl.pallas_call(
        functools.partial(_ffn_kernel, tail=tail),
        name="ffn_" + tail,
        grid=(m // tm, fpad // tf),
        in_specs=[
            row_spec,
            pl.BlockSpec((1, d), lambda i, j: (0, 0)),
            pl.BlockSpec((d, tf), lambda i, j: (0, j)),
            pl.BlockSpec((d, tf), lambda i, j: (0, j)),
            pl.BlockSpec((tf, d), lambda i, j: (j, 0)),
            pl.BlockSpec((1, d), lambda i, j: (0, 0)),
        ],
        out_specs=out_specs,
        out_shape=out_shape,
        scratch_shapes=[pltpu.VMEM((tm, d), BF16), pltpu.VMEM((tm, d), F32)],
        compiler_params=pltpu.CompilerParams(
            dimension_semantics=("parallel", "arbitrary"),
            vmem_limit_bytes=_vmem_limit(est)),
    )(x, gain, wg, wu, wd, gain_tail)


def _inproj_kernel(h_ref, w_ref, wl_ref, z_ref, glr_ref, kr_ref, vr_ref,
                   *, tm, n_heads, hd, k_tile, v_tile):
    j = pl.program_id(1)
    pitch = 2 * n_heads
    z_ref[...] = _dot(h_ref[...], w_ref[...])

    @pl.when(j == 0)
    def _():
        glr_ref[...] = _dot(h_ref[...], wl_ref[...])

    @pl.when(j == k_tile)
    def _():
        for c in range(pitch):
            kr_ref[pl.ds(c, tm, stride=pitch), :] = z_ref[:, c * hd:(c + 1) * hd]

    @pl.when(j == v_tile)
    def _():
        for h in range(n_heads):
            for half in range(2):
                c = 2 * h + half
                vr_ref[pl.ds(n_heads * half + h, tm, stride=pitch), :] = z_ref[:, c * hd:(c + 1) * hd]


def _inproj(h, w, wl, *, n_heads, hd, col_k, col_v, tm=1024):
    m, d = h.shape
    n = w.shape[1]
    tn = n_heads * 2 * hd
    tm = min(tm, m)
    pitch = 2 * n_heads
    assert m % tm == 0 and n % tn == 0 and col_k % tn == 0 and col_v % tn == 0 and hd == LANE
    est = (2 * tm * d * 2 + 2 * d * (tn + LANE) * 2 + 2 * tm * (tn + LANE) * 4
           + 2 * 2 * tm * pitch * hd * 4 + tm * tn * 4)
    return pl.pallas_call(
        functools.partial(_inproj_kernel, tm=tm, n_heads=n_heads, hd=hd, k_tile=col_k // tn,
                          v_tile=col_v // tn),
        name="in_proj",
        grid=(m // tm, n // tn),
        in_specs=[
            pl.BlockSpec((tm, d), lambda i, j: (i, 0)),
            pl.BlockSpec((d, tn), lambda i, j: (0, j)),
            pl.BlockSpec((d, LANE), lambda i, j: (0, 0)),
        ],
        out_specs=[
            pl.BlockSpec((tm, tn), lambda i, j: (i, j)),
            pl.BlockSpec((tm, LANE), lambda i, j: (i, 0)),
            pl.BlockSpec((tm * pitch, hd), lambda i, j: (i, 0)),
            pl.BlockSpec((tm * pitch, hd), lambda i, j: (i, 0)),
        ],
        out_shape=[jax.ShapeDtypeStruct((m, n), F32),
                   jax.ShapeDtypeStruct((m, LANE), F32),
                   jax.ShapeDtypeStruct((m * pitch, hd), F32),
                   jax.ShapeDtypeStruct((m * pitch, hd), F32)],
        compiler_params=pltpu.CompilerParams(
            dimension_semantics=("parallel", "arbitrary"),
            vmem_limit_bytes=_vmem_limit(est)),
    )(h, w, wl)


def _outproj_kernel(x_ref, a_ref, b_ref, wa_ref, wb_ref, o_ref):
    o_ref[...] = x_ref[...] + _dot(a_ref[...], wa_ref[...]) + _dot(b_ref[...], wb_ref[...])


def _outproj(x, mix_a, mix_b, w, *, tm=512):
    m, d = x.shape
    ka, kb = mix_a.shape[1], mix_b.shape[1]
    assert m % tm == 0 and ka == kb and w.shape == (ka + kb, d)
    est = 2 * 2 * tm * d * 4 + 2 * tm * (ka + kb) * 2 + 2 * (ka + kb) * d * 2 + tm * d * 4
    return pl.pallas_call(
        _outproj_kernel,
        name="out_proj",
        grid=(m // tm,),
        in_specs=[
            pl.BlockSpec((tm, d), lambda i: (i, 0)),
            pl.BlockSpec((tm, ka), lambda i: (i, 0)),
            pl.BlockSpec((tm, kb), lambda i: (i, 0)),
            pl.BlockSpec((ka, d), lambda i: (0, 0)),
            pl.BlockSpec((kb, d), lambda i: (1, 0)),
        ],
        out_specs=pl.BlockSpec((tm, d), lambda i: (i, 0)),
        out_shape=jax.ShapeDtypeStruct((m, d), F32),
        compiler_params=pltpu.CompilerParams(
            dimension_semantics=("parallel",),
            vmem_limit_bytes=_vmem_limit(est)),
    )(x, mix_a, mix_b, w, w)


def _diff_lambda(lam_ref, lam_init):
    lp = lam_ref[...]
    e1 = jnp.exp(jnp.sum(lp[0:1] * lp[1:2], axis=-1, keepdims=True))
    e2 = jnp.exp(jnp.sum(lp[2:3] * lp[3:4], axis=-1, keepdims=True))
    return e1 - e2 + lam_init


def _alibi_slope(h, n_heads):
    slope = F32(0.0)
    for i in range(n_heads):
        slope = jnp.where(h == i, F32(2.0 ** (-8.0 * (i + 1) / n_heads)), slope)
    return slope


def _lane_fold(x, op):
    parts = [x[:, i * LANE:(i + 1) * LANE] for i in range(x.shape[1] // LANE)]
    return functools.reduce(op, parts)


def _attn_prompt_kernel(q_ref, k_ref, v_ref, lam_ref, dn_ref, o_ref, s_ref, m_ref, l_ref, acc_ref,
                        *, tq, tk, hd, n_heads, lam_init):
    h = pl.program_id(1)
    qi = pl.program_id(2)
    scale = hd ** -0.5
    slope = _alibi_slope(h, n_heads)
    n_diag = tq // tk
    n_full = qi * n_diag
    q = q_ref[...]
    zero = jnp.zeros((tq, hd), F32)
    qbd = jnp.concatenate([jnp.concatenate([q[:, :hd], zero], axis=1),
                           jnp.concatenate([zero, q[:, hd:]], axis=1)], axis=0).astype(BF16)
    row = lax.broadcasted_iota(jnp.int32, (tq, tk), 0)
    col = lax.broadcasted_iota(jnp.int32, (tq, tk), 1)
    d0 = (row - col).astype(F32)

    m_ref[...] = jnp.full_like(m_ref, -jnp.inf)

    def scores(c, masked):
        ks = pl.multiple_of(c * tk, tk)
        kc = k_ref[pl.ds(ks, tk), :].astype(BF16)
        off = jnp.full((1, 1), qi * tq - c * tk, jnp.int32).astype(F32)
        dist = d0 + off
        bias = slope * dist
        s = _dot_nt(qbd, kc) * scale - jnp.concatenate([bias, bias], axis=0)
        if masked:
            s = jnp.where(jnp.concatenate([dist, dist], axis=0) >= 0, s, -jnp.inf)
        s_ref[c] = s
        m_ref[...] = jnp.maximum(m_ref[...], _lane_fold(s, jnp.maximum))

    def scores_body(c, carry):
        scores(c, False)
        return carry

    lax.fori_loop(0, n_full, scores_body, 0)
    for j in range(n_diag):
        scores(n_full + j, True)

    m_row = jnp.max(m_ref[...], axis=-1, keepdims=True)
    m_ref[...] = jnp.broadcast_to(m_row, m_ref.shape)
    l_ref[...] = jnp.zeros_like(l_ref)
    acc_ref[...] = jnp.zeros_like(acc_ref)

    def pv_body(c, carry):
        ks = pl.multiple_of(c * tk, tk)
        mb = m_ref[...]
        p = jnp.exp(s_ref[c] - jnp.concatenate([mb] * (tk // LANE), axis=1))
        l_ref[...] += _lane_fold(p, jnp.add)
        acc_ref[...] += _dot(p.astype(BF16), v_ref[pl.ds(ks, tk), :].astype(BF16))
        return carry

    lax.fori_loop(0, n_full + n_diag, pv_body, 0)

    lam = _diff_lambda(lam_ref, lam_init)
    on = acc_ref[...] / jnp.sum(l_ref[...], axis=-1, keepdims=True)
    o = on[:tq] - lam * on[tq:]
    o_ref[...] = (_rms(o, dn_ref[...]) * (1.0 - lam_init)).astype(o_ref.dtype)


def _attn_prompt(z, lam_p, diff_norm, *, batch, seq, n_heads, hd, vd, col_q, col_k, col_v,
                 lam_init, tq=512, tk=512):
    assert vd == 2 * hd and seq % tq == 0 and tq % tk == 0 and tk % LANE == 0
    nq = seq // tq
    cw = 2 * hd
    bq, bk, bv = col_q // cw, col_k // cw, col_v // cw
    est = (2 * tq * cw * 4 + 2 * 2 * seq * cw * 4 + 2 * tq * vd * 2
           + (seq // tk) * 2 * tq * tk * 4 + 2 * 2 * tq * LANE * 4 + 2 * tq * vd * 4
           + 6 * 2 * tq * tk * 4)
    return pl.pallas_call(
        functools.partial(_attn_prompt_kernel, tq=tq, tk=tk, hd=hd, n_heads=n_heads,
                          lam_init=lam_init),
        name="attn_prompt",
        grid=(batch, n_heads, nq),
        in_specs=[
            pl.BlockSpec((tq, cw), lambda b, h, i: (b * nq + i, bq + h)),
            pl.BlockSpec((seq, cw), lambda b, h, i: (b, bk + h)),
            pl.BlockSpec((seq, cw), lambda b, h, i: (b, bv + h)),
            pl.BlockSpec((4, hd), lambda b, h, i: (0, 0)),
            pl.BlockSpec((1, vd), lambda b, h, i: (0, 0)),
        ],
        out_specs=pl.BlockSpec((tq, vd), lambda b, h, i: (b * nq + i, h)),
        out_shape=jax.ShapeDtypeStruct((batch * seq, n_heads * vd), BF16),
        scratch_shapes=[pltpu.VMEM((seq // tk, 2 * tq, tk), F32),
                        pltpu.VMEM((2 * tq, LANE), F32), pltpu.VMEM((2 * tq, LANE), F32),
                        pltpu.VMEM((2 * tq, vd), F32)],
        compiler_params=pltpu.CompilerParams(
            dimension_semantics=("parallel", "parallel", "arbitrary"),
            vmem_limit_bytes=_vmem_limit(est)),
    )(z, z, z, lam_p, diff_norm)


def _attn_sample_kernel(*refs, n_pages, page, n_heads, hd, dec, col_q, col_k, col_v, lam_init):
    z_ref = refs[1]
    kp_refs = refs[2:2 + n_pages]
    vp_refs = refs[2 + n_pages:2 + 2 * n_pages]
    pitch = 2 * n_heads
    lam_ref, dn_ref, o_ref, q2_ref, kn_ref, vn_ref = refs[2 + 2 * n_pages:]
    cw = 2 * hd
    rows = 2 * dec
    scale = hd ** -0.5
    past = n_pages * page
    n_keys = past + page
    r_iota = lax.broadcasted_iota(jnp.int32, (rows, n_keys), 0)
    j_iota = lax.broadcasted_iota(jnp.int32, (rows, n_keys), 1)
    q_idx = jnp.where(r_iota >= dec, r_iota - dec, r_iota)
    dist = (past + q_idx - j_iota).astype(F32)
    valid = j_iota <= past + q_idx
    rr = lax.broadcasted_iota(jnp.int32, (rows, cw), 0)
    cc = lax.broadcasted_iota(jnp.int32, (rows, cw), 1)
    keep = (rr < dec) == (cc < hd)
    lam = _diff_lambda(lam_ref, lam_init)

    width = n_heads * cw
    kn_ref[...] = jnp.zeros_like(kn_ref)
    vn_ref[...] = jnp.zeros_like(vn_ref)
    kn_ref[0:dec, :] = z_ref[0, :, col_k:col_k + width]
    vn_ref[0:dec, :] = z_ref[0, :, col_v:col_v + width]

    for h in range(n_heads):
        slope = 2.0 ** (-8.0 * (h + 1) / n_heads)
        qh = z_ref[0, :, col_q + h * cw: col_q + (h + 1) * cw]
        q2_ref[0:dec, :] = qh
        q2_ref[dec:rows, :] = qh
        qbd = jnp.where(keep, q2_ref[...], 0.0).astype(BF16)
        parts = []
        for kp in kp_refs:
            kcat = jnp.concatenate([kp[0, pl.ds(2 * h + m, page, stride=pitch), :] for m in range(2)],
                                   axis=-1)
            parts.append(_dot_nt(qbd, kcat.astype(BF16)))
        parts.append(_dot_nt(qbd, kn_ref[:, h * cw:(h + 1) * cw].astype(BF16)))
        s = jnp.concatenate(parts, axis=-1) * scale - slope * dist
        s = jnp.where(valid, s, -jnp.inf)
        p = jnp.exp(s - jnp.max(s, axis=-1, keepdims=True))
        l = jnp.sum(p, axis=-1, keepdims=True)
        p16 = p.astype(BF16)
        acc = _dot(p16[:, past:], vn_ref[:, h * cw:(h + 1) * cw].astype(BF16))
        for i, vp in enumerate(vp_refs):
            vcat = jnp.concatenate(
                [vp[0, pl.ds(n_heads * half + h, page, stride=pitch), :] for half in range(2)], axis=-1)
            acc = acc + _dot(p16[:, i * page:(i + 1) * page], vcat.astype(BF16))
        on = acc / l
        o = on[:dec] - lam * on[dec:]
        o_ref[0, :, h * cw:(h + 1) * cw] = (_rms(o, dn_ref[...]) * (1.0 - lam_init)).astype(o_ref.dtype)


def _attn_sample(z3, cache_k, cache_v, page_table, layer, lam_p, diff_norm, *, n_heads, hd, vd,
                 col_q, col_k, col_v, lam_init):
    dec_b, dec, zw = z3.shape
    n_pages = page_table.shape[1]
    depth, n_pool, page = cache_k.shape[:3]
    assert vd == 2 * hd
    width = n_heads * 2 * hd
    prow = page * n_heads * 2
    kf = cache_k.reshape(depth * n_pool, prow, hd)
    vf = cache_v.reshape(depth * n_pool, page, n_heads, 2, hd).transpose(0, 1, 3, 2, 4).reshape(
        depth * n_pool, prow, hd)
    pt = (page_table + layer * n_pool).reshape(-1).astype(jnp.int32)
    rows = 2 * dec
    n_keys = (n_pages + 1) * page

    def page_spec(i):
        return pl.BlockSpec((1, prow, hd), lambda b, pt_ref: (pt_ref[b * n_pages + i], 0, 0))

    est = (2 * dec * zw * 4 + 2 * 2 * n_pages * page * width * 4 + 2 * dec * width * 2
           + rows * 2 * hd * 4 + 2 * page * width * 4 + 8 * rows * n_keys * 4)
    grid_spec = pltpu.PrefetchScalarGridSpec(
        num_scalar_prefetch=1,
        grid=(dec_b,),
        in_specs=([pl.BlockSpec((1, dec, zw), lambda b, pt_ref: (b, 0, 0))]
                  + [page_spec(i) for i in range(n_pages)]
                  + [page_spec(i) for i in range(n_pages)]
                  + [pl.BlockSpec((4, hd), lambda b, pt_ref: (0, 0)),
                     pl.BlockSpec((1, vd), lambda b, pt_ref: (0, 0))]),
        out_specs=pl.BlockSpec((1, dec, n_heads * vd), lambda b, pt_ref: (b, 0, 0)),
        scratch_shapes=[pltpu.VMEM((rows, 2 * hd), F32),
                        pltpu.VMEM((page, width), F32),
                        pltpu.VMEM((page, width), F32)],
    )
    return pl.pallas_call(
        functools.partial(_attn_sample_kernel, n_pages=n_pages, page=page, n_heads=n_heads, hd=hd,
                          dec=dec, col_q=col_q, col_k=col_k, col_v=col_v, lam_init=lam_init),
        name="attn_sample",
        grid_spec=grid_spec,
        out_shape=jax.ShapeDtypeStruct((dec_b, dec, n_heads * vd), BF16),
        compiler_params=pltpu.CompilerParams(
            dimension_semantics=("parallel",),
            vmem_limit_bytes=_vmem_limit(est)),
    )(pt, z3, *([kf] * n_pages), *([vf] * n_pages), lam_p, diff_norm)


def _log_decay(glr, w2_ref, b_ref):
    x = _dot(glr.astype(BF16), w2_ref[...]) + b_ref[...]
    return (jnp.minimum(x, 0.0) - jnp.log(1.0 + jnp.exp(-jnp.abs(x)))) / GLA_TAU


def _split_bf16(x):
    hi = x.astype(BF16)
    r1 = x - hi.astype(F32)
    mid = r1.astype(BF16)
    lo = (r1 - mid.astype(F32)).astype(BF16)
    return hi, mid, lo


def _gla_out(o, gn_ref, gr):
    return _rms(o, gn_ref[...]) * (gr * jax.nn.sigmoid(gr))


GLA_MAX_CHUNK_DECAY = 60.0


def _gla_prompt_kernel(q_ref, k_ref, v_ref, gr_ref, glr_ref, w2_ref, b_ref, gn_ref,
                       o_ref, s_ref, st_ref, la_ref, of_ref, *, chunk, n_heads, dk, dv):
    c = pl.program_id(1)
    scale = dk ** -0.5

    @pl.when(c == 0)
    def _():
        st_ref[...] = jnp.zeros_like(st_ref)

    log_a = _log_decay(glr_ref[...], w2_ref, b_ref)
    row = lax.broadcasted_iota(jnp.int32, (chunk, chunk), 0)
    col = lax.broadcasted_iota(jnp.int32, (chunk, chunk), 1)
    causal = row >= col
    tri = jnp.where(causal, 1.0, 0.0).astype(BF16)
    hi, mid, lo = _split_bf16(log_a)
    cum_all = _dot(tri, hi) + _dot(tri, mid) + _dot(tri, lo)
    steep = jnp.max(-cum_all[chunk - 1:chunk, :]) > GLA_MAX_CHUNK_DECAY

    @pl.when(jnp.logical_not(steep))
    def _():
        for h in range(n_heads):
            cum = cum_all[:, h * dk:(h + 1) * dk]
            last = cum[chunk - 1:chunk, :]
            q = q_ref[:, h * dk:(h + 1) * dk] * scale
            k = k_ref[:, h * dk:(h + 1) * dk]
            v = v_ref[:, h * dv:(h + 1) * dv]
            st = st_ref[h]
            qt = (q * jnp.exp(cum)).astype(BF16)
            kt = (k * jnp.exp(-cum)).astype(BF16)
            att = jnp.where(causal, _dot_nt(qt, kt), 0.0)
            of_ref[:, h * dv:(h + 1) * dv] = (_dot(att.astype(BF16), v.astype(BF16))
                                              + _dot_nt(qt, st.astype(BF16)))
            kd = (k * jnp.exp(last - cum)).astype(BF16)
            st_ref[h] = st * jnp.exp(last) + _dot(v.T.astype(BF16), kd)

    @pl.when(steep)
    def _():
        la_ref[...] = log_a
        eye = (lax.broadcasted_iota(jnp.int32, (dv, dv), 0)
               == lax.broadcasted_iota(jnp.int32, (dv, dv), 1))

        sub = lax.broadcasted_iota(jnp.int32, (8, dv), 0)

        def tokens(g, carry):
            r0 = pl.multiple_of(g * 8, 8)
            a8 = jnp.exp(la_ref[pl.ds(r0, 8), :])
            q8 = q_ref[pl.ds(r0, 8), :] * scale
            k8 = k_ref[pl.ds(r0, 8), :]
            v8 = v_ref[pl.ds(r0, 8), :]
            for h in range(n_heads):
                st = st_ref[h]
                o8 = jnp.zeros((8, dv), F32)
                for r in range(8):
                    v_t = v8[r:r + 1, h * dv:(h + 1) * dv]
                    v_col = jnp.sum(jnp.where(eye, jnp.broadcast_to(v_t, (dv, dv)), 0.0),
                                    axis=-1, keepdims=True)
                    st = st * a8[r:r + 1, h * dk:(h + 1) * dk] + v_col * k8[r:r + 1, h * dk:(h + 1) * dk]
                    o_col = jnp.sum(st * q8[r:r + 1, h * dk:(h + 1) * dk], axis=-1, keepdims=True)
                    o_row = jnp.sum(jnp.where(eye, jnp.broadcast_to(o_col, (dv, dv)), 0.0),
                                    axis=0, keepdims=True)
                    o8 = jnp.where(sub == r, o_row, o8)
                st_ref[h] = st
                of_ref[pl.ds(r0, 8), h * dv:(h + 1) * dv] = o8
            return carry

        lax.fori_loop(0, chunk // 8, tokens, 0)

    for h in range(n_heads):
        o_ref[:, h * dv:(h + 1) * dv] = _gla_out(
            of_ref[:, h * dv:(h + 1) * dv], gn_ref, gr_ref[:, h * dv:(h + 1) * dv]).astype(o_ref.dtype)

    @pl.when(c == pl.num_programs(1) - 1)
    def _():
        for h in range(n_heads):
            s_ref[0, h] = st_ref[h].T


def _gla_prompt(z, glr, w2p, bias, gla_norm, *, batch, seq, n_heads, dk, dv, col_q, col_k, col_v,
                col_gr, chunk=256):
    assert seq % chunk == 0
    nc = seq // chunk
    wk, wv = n_heads * dk, n_heads * dv
    assert col_q % wk == 0 and col_k % wk == 0 and col_v % wv == 0 and col_gr % wv == 0
    bq, bk, bv, bg = col_q // wk, col_k // wk, col_v // wv, col_gr // wv
    est = (2 * chunk * (2 * wk + 2 * wv + LANE) * 4 + 2 * chunk * wv * 2 + 3 * n_heads * dk * dv * 4
           + chunk * (wk + wv) * 4 + 4 * chunk * wk * 4 + 8 * chunk * chunk * 4 + 12 * chunk * dv * 4)
    return pl.pallas_call(
        functools.partial(_gla_prompt_kernel, chunk=chunk, n_heads=n_heads, dk=dk, dv=dv),
        name="gla_prompt",
        grid=(batch, nc),
        in_specs=[
            pl.BlockSpec((chunk, wk), lambda b, c: (b * nc + c, bq)),
            pl.BlockSpec((chunk, wk), lambda b, c: (b * nc + c, bk)),
            pl.BlockSpec((chunk, wv), lambda b, c: (b * nc + c, bv)),
            pl.BlockSpec((chunk, wv), lambda b, c: (b * nc + c, bg)),
            pl.BlockSpec((chunk, LANE), lambda b, c: (b * nc + c, 0)),
            pl.BlockSpec((LANE, wk), lambda b, c: (0, 0)),
            pl.BlockSpec((1, wk), lambda b, c: (0, 0)),
            pl.BlockSpec((1, dv), lambda b, c: (0, 0)),
        ],
        out_specs=[
            pl.BlockSpec((chunk, wv), lambda b, c: (b * nc + c, 0)),
            pl.BlockSpec((1, n_heads, dk, dv), lambda b, c: (b, 0, 0, 0)),
        ],
        out_shape=[jax.ShapeDtypeStruct((batch * seq, wv), BF16),
                   jax.ShapeDtypeStruct((batch, n_heads, dk, dv), F32)],
        scratch_shapes=[pltpu.VMEM((n_heads, dv, dk), F32),
                        pltpu.VMEM((chunk, wk), F32),
                        pltpu.VMEM((chunk, wv), F32)],
        compiler_params=pltpu.CompilerParams(
            dimension_semantics=("parallel", "arbitrary"),
            vmem_limit_bytes=_vmem_limit(est)),
    )(z, z, z, z, glr, w2p, bias, gla_norm)


def _gla_sample_kernel(z_ref, glr_ref, s0_ref, w2_ref, b_ref, gn_ref, o_ref, s_ref,
                       *, n_seq, n_heads, dk, dv, dec, col_q, col_k, col_v, col_gr):
    for s in range(n_seq):
        _gla_sample_one(z_ref.at[s], glr_ref.at[s], s0_ref.at[s], w2_ref, b_ref, gn_ref,
                        o_ref.at[s], s_ref.at[s], n_heads=n_heads, dk=dk, dv=dv, dec=dec,
                        col_q=col_q, col_k=col_k, col_v=col_v, col_gr=col_gr)


def _gla_sample_one(z_ref, glr_ref, s0_ref, w2_ref, b_ref, gn_ref, o_ref, s_ref,
                    *, n_heads, dk, dv, dec, col_q, col_k, col_v, col_gr):
    log_a_all = _log_decay(glr_ref[...], w2_ref, b_ref)
    t_k = lax.broadcasted_iota(jnp.int32, (dec, dk), 0)
    t_v = lax.broadcasted_iota(jnp.int32, (dec, dv), 0)
    eye = (lax.broadcasted_iota(jnp.int32, (dk, dk), 0)
           == lax.broadcasted_iota(jnp.int32, (dk, dk), 1))

    def to_col(r):
        return jnp.sum(jnp.where(eye, jnp.broadcast_to(r, (dk, dk)), 0.0), axis=-1, keepdims=True)

    for h in range(n_heads):
        log_a = log_a_all[:, h * dk:(h + 1) * dk]
        cum = jnp.zeros((dec, dk), F32)
        for t in range(dec):
            cum = cum + jnp.where(t_k >= t, log_a[t:t + 1], 0.0)
        last = cum[dec - 1:dec]
        q = z_ref[:, col_q + h * dk: col_q + (h + 1) * dk] * (dk ** -0.5)
        k = z_ref[:, col_k + h * dk: col_k + (h + 1) * dk]
        v = z_ref[:, col_v + h * dv: col_v + (h + 1) * dv]
        gr = z_ref[:, col_gr + h * dv: col_gr + (h + 1) * dv]
        s0 = s0_ref[h]

        o = _dot((q * jnp.exp(cum)).astype(BF16), s0.astype(BF16))
        for t in range(dec):
            o_t = jnp.zeros((1, dv), F32)
            for j in range(t + 1):
                w = jnp.exp(cum[t:t + 1] - cum[j:j + 1])
                a = jnp.sum(q[t:t + 1] * k[j:j + 1] * w, axis=-1, keepdims=True)
                o_t = o_t + a * v[j:j + 1]
            o = o + jnp.where(t_v == t, o_t, 0.0)

        kd = (k * jnp.exp(last - cum)).astype(BF16)
        upd = lax.dot_general(kd, v.astype(BF16), (((0,), (0,)), ((), ())),
                              preferred_element_type=F32)
        s_ref[h] = s0 * to_col(jnp.exp(last)) + upd
        o_ref[:, h * dv:(h + 1) * dv] = _gla_out(o, gn_ref, gr).astype(o_ref.dtype)


def _gla_sample(z3, glr3, state, w2p, bias, gla_norm, *, n_heads, dk, dv, col_q, col_k, col_v,
                col_gr, n_seq=4):
    dec_b, dec, zw = z3.shape
    assert dec_b % n_seq == 0
    est = n_seq * (2 * dec * zw * 4 + 2 * 2 * n_heads * dk * dv * 4) + LANE * n_heads * dk * 2 \
        + 8 * dk * dv * 4
    return pl.pallas_call(
        functools.partial(_gla_sample_kernel, n_seq=n_seq, n_heads=n_heads, dk=dk, dv=dv, dec=dec,
                          col_q=col_q, col_k=col_k, col_v=col_v, col_gr=col_gr),
        name="gla_sample",
        grid=(dec_b // n_seq,),
        in_specs=[
            pl.BlockSpec((n_seq, dec, zw), lambda b: (b, 0, 0)),
            pl.BlockSpec((n_seq, dec, LANE), lambda b: (b, 0, 0)),
            pl.BlockSpec((n_seq, n_heads, dk, dv), lambda b: (b, 0, 0, 0)),
            pl.BlockSpec((LANE, n_heads * dk), lambda b: (0, 0)),
            pl.BlockSpec((1, n_heads * dk), lambda b: (0, 0)),
            pl.BlockSpec((1, dv), lambda b: (0, 0)),
        ],
        out_specs=[
            pl.BlockSpec((n_seq, dec, n_heads * dv), lambda b: (b, 0, 0)),
            pl.BlockSpec((n_seq, n_heads, dk, dv), lambda b: (b, 0, 0, 0)),
        ],
        out_shape=[jax.ShapeDtypeStruct((dec_b, dec, n_heads * dv), BF16),
                   jax.ShapeDtypeStruct((dec_b, n_heads, dk, dv), F32)],
        compiler_params=pltpu.CompilerParams(
            dimension_semantics=("parallel",),
            vmem_limit_bytes=_vmem_limit(est)),
    )(z3, glr3, state, w2p, bias, gla_norm)


def _round_up(x, m):
    return (x + m - 1) // m * m


def kernel(x_prompt, x_sample, cache_k, cache_v, state_gla, page_table, norm_ffn1, ffn1_w_gate, ffn1_w_up, ffn1_w_down, norm_mix, w_in, gla_gate_w2, gla_gate_b, lambda_q1, lambda_k1, lambda_q2, lambda_k2, diff_norm, gla_norm, w_out, norm_ffn2, ffn2_w_gate, ffn2_w_up, ffn2_w_down, norm_final):
    batch, seq, d = x_prompt.shape
    dec_b, dec, _ = x_sample.shape
    depth = norm_ffn1.shape[0]
    n_heads, hd = cache_k.shape[3], cache_k.shape[5]
    vd = cache_v.shape[4]
    g_heads, dk, dv = state_gla.shape[2:]
    rank = gla_gate_w2.shape[1]
    d_ff = ffn1_w_gate.shape[2]

    w_qk = n_heads * 2 * hd
    sizes = (w_qk, w_qk, n_heads * vd, g_heads * dk, g_heads * dk, g_heads * dv, g_heads * dv, rank)
    cols = [0]
    for s in sizes:
        cols.append(cols[-1] + s)
    col_dq, col_dk, col_dv, col_gq, col_gk, col_gv, col_gr, col_glr = cols[:8]
    assert w_in.shape[2] == cols[8] and rank <= LANE
    ff_pad = _round_up(d_ff, 512)
    diff_w = n_heads * vd

    yp = x_prompt.reshape(batch * seq, d)
    ys = x_sample.reshape(dec_b * dec, d)
    nf = norm_final.reshape(1, d)
    outs = [[] for _ in range(6)]

    for layer in range(depth):
        lam_init = _lambda_init(layer)
        ffn_w = []
        for wg, wu, wd in ((ffn1_w_gate, ffn1_w_up, ffn1_w_down), (ffn2_w_gate, ffn2_w_up, ffn2_w_down)):
            ffn_w.append((_to_bf16(wg, layer, d, ff_pad), _to_bf16(wu, layer, d, ff_pad),
                          _to_bf16(wd, layer, ff_pad, d)))
        w_main, w_glr = _w_in_prep(w_in, layer, col_glr)
        w2p = jnp.pad(gla_gate_w2[layer], ((0, LANE - rank), (0, 0))).astype(BF16)
        gate_b = gla_gate_b[layer].reshape(1, -1)
        lam_p = jnp.stack([lambda_q1[layer], lambda_k1[layer], lambda_q2[layer], lambda_k2[layer]])
        dn = diff_norm[layer].reshape(1, vd)
        gn = gla_norm[layer].reshape(1, dv)
        wo = _to_bf16(w_out, layer, w_out.shape[1], d)
        g1 = norm_ffn1[layer].reshape(1, d)
        gm = norm_mix[layer].reshape(1, d)
        g2 = norm_ffn2[layer].reshape(1, d)
        last = layer == depth - 1

        def pre(x):
            x, hmix = _ffn(x, g1, *ffn_w[0], gm, tail="next")
            return (x,) + tuple(_inproj(hmix, w_main, w_glr, n_heads=n_heads, hd=hd, col_k=col_dk,
                                        col_v=col_dv))

        def post(x, mix_a, mix_b):
            x = _outproj(x, mix_a, mix_b, wo)
            return _ffn(x, g2, *ffn_w[1], nf, tail="final" if last else "none")

        def kv_out(k_rows, v_rows, b, l):
            k = k_rows.reshape(b, l, n_heads, 2, hd)
            v = v_rows.reshape(b, l, 2, n_heads, hd).transpose(0, 1, 3, 2, 4).reshape(b, l, n_heads, vd)
            return k, v

        yp, zp, glr_p, k_rows, v_rows = pre(yp)
        mix_a = _attn_prompt(zp, lam_p, dn, batch=batch, seq=seq, n_heads=n_heads, hd=hd, vd=vd,
                             col_q=col_dq, col_k=col_dk, col_v=col_dv, lam_init=lam_init)
        mix_b, s_p = _gla_prompt(zp, glr_p, w2p, gate_b, gn, batch=batch, seq=seq, n_heads=g_heads,
                                 dk=dk, dv=dv, col_q=col_gq, col_k=col_gk, col_v=col_gv, col_gr=col_gr)
        yp = post(yp, mix_a, mix_b)
        k_new, v_new = kv_out(k_rows, v_rows, batch, seq)
        outs[0].append(k_new)
        outs[1].append(v_new)
        outs[2].append(s_p.astype(state_gla.dtype))

        ys, zs, glr_s, k_rows, v_rows = pre(ys)
        zs3 = zs.reshape(dec_b, dec, col_glr)
        mix_a = _attn_sample(zs3, cache_k, cache_v, page_table, layer, lam_p, dn, n_heads=n_heads,
                             hd=hd, vd=vd, col_q=col_dq, col_k=col_dk, col_v=col_dv,
                             lam_init=lam_init)
        mix_b, s_s = _gla_sample(zs3, glr_s.reshape(dec_b, dec, LANE), state_gla[layer], w2p, gate_b, gn,
                                 n_heads=g_heads, dk=dk, dv=dv, col_q=col_gq, col_k=col_gk,
                                 col_v=col_gv, col_gr=col_gr)
        ys = post(ys, mix_a.reshape(dec_b * dec, -1), mix_b.reshape(dec_b * dec, -1))
        k_new, v_new = kv_out(k_rows, v_rows, dec_b, dec)
        outs[3].append(k_new)
        outs[4].append(v_new)
        outs[5].append(s_s.astype(state_gla.dtype))

    if depth == 0:
        raise ValueError("depth must be positive")
    y_prompt = yp.reshape(batch, seq, d)
    y_sample = ys.reshape(dec_b, dec, d)
    return (y_prompt, y_sample) + tuple(jnp.stack(o) for o in outs)
```

```python
import functools
import math

import jax
import jax.numpy as jnp
from jax import lax
from jax.experimental import pallas as pl
from jax.experimental.pallas import tpu as pltpu

F32 = jnp.float32
BF16 = jnp.bfloat16

NORM_EPS = 1e-6
GLA_TAU = 16.0
LOG2_E = math.log2(math.e)
LANE = 128
VMEM_PHYSICAL = 64 * 2**20


def _lambda_init(layer):
    return 0.8 - 0.6 * math.exp(-0.3 * layer)


def _vmem_limit(nbytes):
    return int(min(nbytes * 5 // 4 + (4 << 20), VMEM_PHYSICAL - (8 << 20)))


def _rms(x, gain):
    return x * lax.rsqrt(jnp.mean(x * x, axis=-1, keepdims=True) + NORM_EPS) * gain


def _dot(a, b):
    return jnp.dot(a, b, preferred_element_type=F32)


def _dot_nt(a, b):
    return lax.dot_general(a, b, (((1,), (1,)), ((), ())), preferred_element_type=F32)


def _to_bf16_kernel(w_ref, o_ref, *, rows, cols, tr):
    r_out, c_out = o_ref.shape
    c_copy = min(cols, c_out)
    w = w_ref[0][:, :c_copy]
    if rows % tr:
        r_idx = pl.program_id(0) * tr + lax.broadcasted_iota(jnp.int32, (r_out, c_copy), 0)
        w = jnp.where(r_idx < rows, w, 0.0)
    o_ref[:, :c_copy] = w.astype(BF16)
    if c_copy < c_out:
        o_ref[:, c_copy:] = jnp.zeros((r_out, c_out - c_copy), BF16)


def _to_bf16(w, layer, rows_out, cols_out, *, tr=256):
    _, rows, cols = w.shape
    assert rows_out % tr == 0 and rows_out >= rows and min(cols, cols_out) % LANE == 0
    est = 2 * tr * cols * 4 + 2 * tr * cols_out * 2 + tr * cols * 4
    return pl.pallas_call(
        functools.partial(_to_bf16_kernel, rows=rows, cols=cols, tr=tr),
        name="to_bf16",
        grid=(rows_out // tr,),
        in_specs=[pl.BlockSpec((1, tr, cols), lambda i: (layer, i, 0))],
        out_specs=pl.BlockSpec((tr, cols_out), lambda i: (i, 0)),
        out_shape=jax.ShapeDtypeStruct((rows_out, cols_out), BF16),
        compiler_params=pltpu.CompilerParams(
            dimension_semantics=("parallel",),
            vmem_limit_bytes=_vmem_limit(est)),
    )(w)


def _w_in_prep_kernel(wt_ref, wm_ref, wl_ref, *, n_main_blocks, rank, tr):
    j = pl.program_id(0)

    @pl.when(j < n_main_blocks)
    def _():
        wm_ref[...] = wt_ref[0].T.astype(BF16)

    @pl.when(j == n_main_blocks)
    def _():
        tail = wt_ref[0][:LANE]
        r_idx = lax.broadcasted_iota(jnp.int32, tail.shape, 0)
        wl_ref[...] = jnp.where(r_idx < rank, tail, 0.0).T.astype(BF16)


def _w_in_prep(w_in, layer, n_main, *, tr=512):
    wt = jnp.swapaxes(w_in, 1, 2)
    _, n, d = wt.shape
    rank = n - n_main
    nb = n_main // tr
    assert n_main % tr == 0 and 0 < rank <= LANE <= tr
    est = 2 * tr * d * 4 + 2 * d * (tr + LANE) * 2 + 2 * tr * d * 4
    return pl.pallas_call(
        functools.partial(_w_in_prep_kernel, n_main_blocks=nb, rank=rank, tr=tr),
        name="w_in_prep",
        grid=(nb + 1,),
        in_specs=[pl.BlockSpec((1, tr, d), lambda j: (layer, j, 0))],
        out_specs=[pl.BlockSpec((d, tr), lambda j: (0, jnp.minimum(j, nb - 1))),
                   pl.BlockSpec((d, LANE), lambda j: (0, 0))],
        out_shape=[jax.ShapeDtypeStruct((d, n_main), BF16), jax.ShapeDtypeStruct((d, LANE), BF16)],
        compiler_params=pltpu.CompilerParams(
            dimension_semantics=("arbitrary",),
            vmem_limit_bytes=_vmem_limit(est)),
    )(wt)


def _ffn_kernel(x_ref, g_ref, wg_ref, wu_ref, wd_ref, gt_ref, *rest, tail):
    o_ref = rest[0]
    h_ref, acc_ref = rest[-2:]
    j = pl.program_id(1)

    @pl.when(j == 0)
    def _():
        h_ref[...] = _rms(x_ref[...], g_ref[...]).astype(BF16)
        acc_ref[...] = jnp.zeros_like(acc_ref)

    h = h_ref[...]
    gate = _dot(h, wg_ref[...])
    up = _dot(h, wu_ref[...])
    act = (gate * jax.nn.sigmoid(gate)) * up
    acc_ref[...] += _dot(act.astype(BF16), wd_ref[...])

    @pl.when(j == pl.num_programs(1) - 1)
    def _():
        y = x_ref[...] + 0.5 * acc_ref[...]
        if tail == "final":
            y = _rms(y, gt_ref[...])
        o_ref[...] = y
        if tail == "next":
            rest[1][...] = _rms(y, gt_ref[...]).astype(BF16)


def _ffn(x, gain, wg, wu, wd, gain_tail, *, tail, tm=512, tf=512):
    m, d = x.shape
    fpad = wg.shape[1]
    assert m % tm == 0 and fpad % tf == 0 and tail in ("final", "next", "none")
    est = (2 * 2 * tm * d * 4
           + 2 * tm * d * 2
           + tm * d * (2 + 4)
           + 2 * 3 * d * tf * 2
           + 3 * tm * tf * 4)
    row_spec = pl.BlockSpec((tm, d), lambda i, j: (i, 0))
    out_specs, out_shape = row_spec, jax.ShapeDtypeStruct((m, d), F32)
    if tail == "next":
        out_specs, out_shape = [row_spec, row_spec], [out_shape, jax.ShapeDtypeStruct((m, d), BF16)]
    return pl.pallas_call(
        functools.partial(_ffn_kernel, tail=tail),
        name="ffn_" + tail,
        grid=(m // tm, fpad // tf),
        in_specs=[
            row_spec,
            pl.BlockSpec((1, d), lambda i, j: (0, 0)),
            pl.BlockSpec((d, tf), lambda i, j: (0, j)),
            pl.BlockSpec((d, tf), lambda i, j: (0, j)),
            pl.BlockSpec((tf, d), lambda i, j: (j, 0)),
            pl.BlockSpec((1, d), lambda i, j: (0, 0)),
        ],
        out_specs=out_specs,
        out_shape=out_shape,
        scratch_shapes=[pltpu.VMEM((tm, d), BF16), pltpu.VMEM((tm, d), F32)],
        compiler_params=pltpu.CompilerParams(
            dimension_semantics=("parallel", "arbitrary"),
            vmem_limit_bytes=_vmem_limit(est)),
    )(x, gain, wg, wu, wd, gain_tail)


def _inproj_kernel(h_ref, w_ref, wl_ref, z_ref, glr_ref, kr_ref, vr_ref,
                   *, tm, n_heads, hd, k_tile, v_tile):
    j = pl.program_id(1)
    pitch = 2 * n_heads
    z_ref[...] = _dot(h_ref[...], w_ref[...])

    @pl.when(j == 0)
    def _():
        glr_ref[...] = _dot(h_ref[...], wl_ref[...])

    @pl.when(j == k_tile)
    def _():
        for c in range(pitch):
            kr_ref[pl.ds(c, tm, stride=pitch), :] = z_ref[:, c * hd:(c + 1) * hd]

    @pl.when(j == v_tile)
    def _():
        for h in range(n_heads):
            for half in range(2):
                c = 2 * h + half
                vr_ref[pl.ds(n_heads * half + h, tm, stride=pitch), :] = z_ref[:, c * hd:(c + 1) * hd]


def _inproj(h, w, wl, *, n_heads, hd, col_k, col_v, tm=1024):
    m, d = h.shape
    n = w.shape[1]
    tn = n_heads * 2 * hd
    tm = min(tm, m)
    pitch = 2 * n_heads
    assert m % tm == 0 and n % tn == 0 and col_k % tn == 0 and col_v % tn == 0 and hd == LANE
    est = (2 * tm * d * 2 + 2 * d * (tn + LANE) * 2 + 2 * tm * (tn + LANE) * 4
           + 2 * 2 * tm * pitch * hd * 4 + tm * tn * 4)
    return pl.pallas_call(
        functools.partial(_inproj_kernel, tm=tm, n_heads=n_heads, hd=hd, k_tile=col_k // tn,
                          v_tile=col_v // tn),
        name="in_proj",
        grid=(m // tm, n // tn),
        in_specs=[
            pl.BlockSpec((tm, d), lambda i, j: (i, 0)),
            pl.BlockSpec((d, tn), lambda i, j: (0, j)),
            pl.BlockSpec((d, LANE), lambda i, j: (0, 0)),
        ],
        out_specs=[
            pl.BlockSpec((tm, tn), lambda i, j: (i, j)),
            pl.BlockSpec((tm, LANE), lambda i, j: (i, 0)),
            pl.BlockSpec((tm * pitch, hd), lambda i, j: (i, 0)),
            pl.BlockSpec((tm * pitch, hd), lambda i, j: (i, 0)),
        ],
        out_shape=[jax.ShapeDtypeStruct((m, n), F32),
                   jax.ShapeDtypeStruct((m, LANE), F32),
                   jax.ShapeDtypeStruct((m * pitch, hd), F32),
                   jax.ShapeDtypeStruct((m * pitch, hd), F32)],
        compiler_params=pltpu.CompilerParams(
            dimension_semantics=("parallel", "arbitrary"),
            vmem_limit_bytes=_vmem_limit(est)),
    )(h, w, wl)


def _outproj_kernel(x_ref, a_ref, b_ref, wa_ref, wb_ref, o_ref):
    o_ref[...] = x_ref[...] + _dot(a_ref[...], wa_ref[...]) + _dot(b_ref[...], wb_ref[...])


def _outproj(x, mix_a, mix_b, w, *, tm=512):
    m, d = x.shape
    ka, kb = mix_a.shape[1], mix_b.shape[1]
    assert m % tm == 0 and ka == kb and w.shape == (ka + kb, d)
    est = 2 * 2 * tm * d * 4 + 2 * tm * (ka + kb) * 2 + 2 * (ka + kb) * d * 2 + tm * d * 4
    return pl.pallas_call(
        _outproj_kernel,
        name="out_proj",
        grid=(m // tm,),
        in_specs=[
            pl.BlockSpec((tm, d), lambda i: (i, 0)),
            pl.BlockSpec((tm, ka), lambda i: (i, 0)),
            pl.BlockSpec((tm, kb), lambda i: (i, 0)),
            pl.BlockSpec((ka, d), lambda i: (0, 0)),
            pl.BlockSpec((kb, d), lambda i: (1, 0)),
        ],
        out_specs=pl.BlockSpec((tm, d), lambda i: (i, 0)),
        out_shape=jax.ShapeDtypeStruct((m, d), F32),
        compiler_params=pltpu.CompilerParams(
            dimension_semantics=("parallel",),
            vmem_limit_bytes=_vmem_limit(est)),
    )(x, mix_a, mix_b, w, w)


def _diff_lambda(lam_ref, lam_init):
    lp = lam_ref[...]
    e1 = jnp.exp(jnp.sum(lp[0:1] * lp[1:2], axis=-1, keepdims=True))
    e2 = jnp.exp(jnp.sum(lp[2:3] * lp[3:4], axis=-1, keepdims=True))
    return e1 - e2 + lam_init


def _alibi_slope(h, n_heads):
    slope = F32(0.0)
    for i in range(n_heads):
        slope = jnp.where(h == i, F32(2.0 ** (-8.0 * (i + 1) / n_heads)), slope)
    return slope


def _lane_fold(x, op):
    parts = [x[:, i * LANE:(i + 1) * LANE] for i in range(x.shape[1] // LANE)]
    return functools.reduce(op, parts)


def _attn_prompt_kernel(q_ref, k_ref, v_ref, lam_ref, dn_ref, o_ref, s_ref, m_ref, l_ref, acc_ref,
                        *, tq, tk, hd, n_heads, lam_init):
    h = pl.program_id(1)
    qi = pl.program_id(2)
    scale = hd ** -0.5 * LOG2_E
    slope = _alibi_slope(h, n_heads) * LOG2_E
    n_diag = tq // tk
    n_full = qi * n_diag
    q = q_ref[...]
    zero = jnp.zeros((tq, hd), F32)
    qbd = jnp.concatenate([jnp.concatenate([q[:, :hd], zero], axis=1),
                           jnp.concatenate([zero, q[:, hd:]], axis=1)], axis=0).astype(BF16)
    row = lax.broadcasted_iota(jnp.int32, (tq, tk), 0)
    col = lax.broadcasted_iota(jnp.int32, (tq, tk), 1)
    d0 = (row - col).astype(F32)

    m_ref[...] = jnp.full_like(m_ref, -jnp.inf)

    def scores(c, masked):
        ks = pl.multiple_of(c * tk, tk)
        kc = k_ref[pl.ds(ks, tk), :].astype(BF16)
        off = jnp.full((1, 1), qi * tq - c * tk, jnp.int32).astype(F32)
        dist = d0 + off
        bias = slope * dist
        s = _dot_nt(qbd, kc) * scale - jnp.concatenate([bias, bias], axis=0)
        if masked:
            s = jnp.where(jnp.concatenate([dist, dist], axis=0) >= 0, s, -jnp.inf)
        s_ref[c] = s
        m_ref[...] = jnp.maximum(m_ref[...], _lane_fold(s, jnp.maximum))

    def scores_body(c, carry):
        scores(c, False)
        return carry

    lax.fori_loop(0, n_full, scores_body, 0)
    for j in range(n_diag):
        scores(n_full + j, True)

    m_row = jnp.max(m_ref[...], axis=-1, keepdims=True)
    m_ref[...] = jnp.broadcast_to(m_row, m_ref.shape)
    l_ref[...] = jnp.zeros_like(l_ref)
    acc_ref[...] = jnp.zeros_like(acc_ref)

    def pv_body(c, carry):
        ks = pl.multiple_of(c * tk, tk)
        mb = m_ref[...]
        p = jnp.exp2(s_ref[c] - jnp.concatenate([mb] * (tk // LANE), axis=1))
        l_ref[...] += _lane_fold(p, jnp.add)
        acc_ref[...] += _dot(p.astype(BF16), v_ref[pl.ds(ks, tk), :].astype(BF16))
        return carry

    lax.fori_loop(0, n_full + n_diag, pv_body, 0)

    lam = _diff_lambda(lam_ref, lam_init)
    on = acc_ref[...] / jnp.sum(l_ref[...], axis=-1, keepdims=True)
    o = on[:tq] - lam * on[tq:]
    o_ref[...] = (_rms(o, dn_ref[...]) * (1.0 - lam_init)).astype(o_ref.dtype)


def _attn_prompt(z, lam_p, diff_norm, *, batch, seq, n_heads, hd, vd, col_q, col_k, col_v,
                 lam_init, tq=512, tk=512):
    assert vd == 2 * hd and seq % tq == 0 and tq % tk == 0 and tk % LANE == 0
    nq = seq // tq
    cw = 2 * hd
    bq, bk, bv = col_q // cw, col_k // cw, col_v // cw
    est = (2 * tq * cw * 4 + 2 * 2 * seq * cw * 4 + 2 * tq * vd * 2
           + (seq // tk) * 2 * tq * tk * 4 + 2 * 2 * tq * LANE * 4 + 2 * tq * vd * 4
           + 6 * 2 * tq * tk * 4)
    return pl.pallas_call(
        functools.partial(_attn_prompt_kernel, tq=tq, tk=tk, hd=hd, n_heads=n_heads,
                          lam_init=lam_init),
        name="attn_prompt",
        grid=(batch, n_heads, nq),
        in_specs=[
            pl.BlockSpec((tq, cw), lambda b, h, i: (b * nq + i, bq + h)),
            pl.BlockSpec((seq, cw), lambda b, h, i: (b, bk + h)),
            pl.BlockSpec((seq, cw), lambda b, h, i: (b, bv + h)),
            pl.BlockSpec((4, hd), lambda b, h, i: (0, 0)),
            pl.BlockSpec((1, vd), lambda b, h, i: (0, 0)),
        ],
        out_specs=pl.BlockSpec((tq, vd), lambda b, h, i: (b * nq + i, h)),
        out_shape=jax.ShapeDtypeStruct((batch * seq, n_heads * vd), BF16),
        scratch_shapes=[pltpu.VMEM((seq // tk, 2 * tq, tk), F32),
                        pltpu.VMEM((2 * tq, LANE), F32), pltpu.VMEM((2 * tq, LANE), F32),
                        pltpu.VMEM((2 * tq, vd), F32)],
        compiler_params=pltpu.CompilerParams(
            dimension_semantics=("parallel", "parallel", "arbitrary"),
            vmem_limit_bytes=_vmem_limit(est)),
    )(z, z, z, lam_p, diff_norm)


def _attn_sample_kernel(*refs, n_pages, page, n_heads, hd, dec, col_q, col_k, col_v, lam_init):
    z_ref = refs[1]
    kp_refs = refs[2:2 + n_pages]
    vp_refs = refs[2 + n_pages:2 + 2 * n_pages]
    pitch = 2 * n_heads
    lam_ref, dn_ref, o_ref, q2_ref, kn_ref, vn_ref = refs[2 + 2 * n_pages:]
    cw = 2 * hd
    rows = 2 * dec
    scale = hd ** -0.5 * LOG2_E
    past = n_pages * page
    n_keys = past + page
    r_iota = lax.broadcasted_iota(jnp.int32, (rows, n_keys), 0)
    j_iota = lax.broadcasted_iota(jnp.int32, (rows, n_keys), 1)
    q_idx = jnp.where(r_iota >= dec, r_iota - dec, r_iota)
    dist = (past + q_idx - j_iota).astype(F32)
    valid = j_iota <= past + q_idx
    rr = lax.broadcasted_iota(jnp.int32, (rows, cw), 0)
    cc = lax.broadcasted_iota(jnp.int32, (rows, cw), 1)
    keep = (rr < dec) == (cc < hd)
    lam = _diff_lambda(lam_ref, lam_init)

    width = n_heads * cw
    kn_ref[...] = jnp.zeros_like(kn_ref)
    vn_ref[...] = jnp.zeros_like(vn_ref)
    kn_ref[0:dec, :] = z_ref[0, :, col_k:col_k + width]
    vn_ref[0:dec, :] = z_ref[0, :, col_v:col_v + width]

    for h in range(n_heads):
        slope = 2.0 ** (-8.0 * (h + 1) / n_heads) * LOG2_E
        qh = z_ref[0, :, col_q + h * cw: col_q + (h + 1) * cw]
        q2_ref[0:dec, :] = qh
        q2_ref[dec:rows, :] = qh
        qbd = jnp.where(keep, q2_ref[...], 0.0).astype(BF16)
        parts = []
        for kp in kp_refs:
            kcat = jnp.concatenate([kp[0, pl.ds(2 * h + m, page, stride=pitch), :] for m in range(2)],
                                   axis=-1)
            parts.append(_dot_nt(qbd, kcat.astype(BF16)))
        parts.append(_dot_nt(qbd, kn_ref[:, h * cw:(h + 1) * cw].astype(BF16)))
        s = jnp.concatenate(parts, axis=-1) * scale - slope * dist
        s = jnp.where(valid, s, -jnp.inf)
        p = jnp.exp2(s - jnp.max(s, axis=-1, keepdims=True))
        l = jnp.sum(p, axis=-1, keepdims=True)
        p16 = p.astype(BF16)
        acc = _dot(p16[:, past:], vn_ref[:, h * cw:(h + 1) * cw].astype(BF16))
        for i, vp in enumerate(vp_refs):
            vcat = jnp.concatenate(
                [vp[0, pl.ds(n_heads * half + h, page, stride=pitch), :] for half in range(2)], axis=-1)
            acc = acc + _dot(p16[:, i * page:(i + 1) * page], vcat.astype(BF16))
        on = acc / l
        o = on[:dec] - lam * on[dec:]
        o_ref[0, :, h * cw:(h + 1) * cw] = (_rms(o, dn_ref[...]) * (1.0 - lam_init)).astype(o_ref.dtype)


def _attn_sample(z3, cache_k, cache_v, page_table, layer, lam_p, diff_norm, *, n_heads, hd, vd,
                 col_q, col_k, col_v, lam_init):
    dec_b, dec, zw = z3.shape
    n_pages = page_table.shape[1]
    depth, n_pool, page = cache_k.shape[:3]
    assert vd == 2 * hd
    width = n_heads * 2 * hd
    prow = page * n_heads * 2
    kf = cache_k.reshape(depth * n_pool, prow, hd)
    vf = cache_v.reshape(depth * n_pool, page, n_heads, 2, hd).transpose(0, 1, 3, 2, 4).reshape(
        depth * n_pool, prow, hd)
    pt = (page_table + layer * n_pool).reshape(-1).astype(jnp.int32)
    rows = 2 * dec
    n_keys = (n_pages + 1) * page

    def page_spec(i):
        return pl.BlockSpec((1, prow, hd), lambda b, pt_ref: (pt_ref[b * n_pages + i], 0, 0))

    est = (2 * dec * zw * 4 + 2 * 2 * n_pages * page * width * 4 + 2 * dec * width * 2
           + rows * 2 * hd * 4 + 2 * page * width * 4 + 8 * rows * n_keys * 4)
    grid_spec = pltpu.PrefetchScalarGridSpec(
        num_scalar_prefetch=1,
        grid=(dec_b,),
        in_specs=([pl.BlockSpec((1, dec, zw), lambda b, pt_ref: (b, 0, 0))]
                  + [page_spec(i) for i in range(n_pages)]
                  + [page_spec(i) for i in range(n_pages)]
                  + [pl.BlockSpec((4, hd), lambda b, pt_ref: (0, 0)),
                     pl.BlockSpec((1, vd), lambda b, pt_ref: (0, 0))]),
        out_specs=pl.BlockSpec((1, dec, n_heads * vd), lambda b, pt_ref: (b, 0, 0)),
        scratch_shapes=[pltpu.VMEM((rows, 2 * hd), F32),
                        pltpu.VMEM((page, width), F32),
                        pltpu.VMEM((page, width), F32)],
    )
    return pl.pallas_call(
        functools.partial(_attn_sample_kernel, n_pages=n_pages, page=page, n_heads=n_heads, hd=hd,
                          dec=dec, col_q=col_q, col_k=col_k, col_v=col_v, lam_init=lam_init),
        name="attn_sample",
        grid_spec=grid_spec,
        out_shape=jax.ShapeDtypeStruct((dec_b, dec, n_heads * vd), BF16),
        compiler_params=pltpu.CompilerParams(
            dimension_semantics=("parallel",),
            vmem_limit_bytes=_vmem_limit(est)),
    )(pt, z3, *([kf] * n_pages), *([vf] * n_pages), lam_p, diff_norm)


def _log_decay(glr, w2_ref, b_ref):
    x = _dot(glr.astype(BF16), w2_ref[...]) + b_ref[...]
    return (jnp.minimum(x, 0.0) - jnp.log(1.0 + jnp.exp(-jnp.abs(x)))) / GLA_TAU


def _split_bf16(x):
    hi = x.astype(BF16)
    r1 = x - hi.astype(F32)
    mid = r1.astype(BF16)
    lo = (r1 - mid.astype(F32)).astype(BF16)
    return hi, mid, lo


def _gla_out(o, gn_ref, gr):
    return _rms(o, gn_ref[...]) * (gr * jax.nn.sigmoid(gr))


GLA_MAX_CHUNK_DECAY = 60.0


def _gla_prompt_kernel(q_ref, k_ref, v_ref, gr_ref, glr_ref, w2_ref, b_ref, gn_ref,
                       o_ref, s_ref, st_ref, la_ref, of_ref, *, chunk, n_heads, dk, dv):
    c = pl.program_id(1)
    scale = dk ** -0.5

    @pl.when(c == 0)
    def _():
        st_ref[...] = jnp.zeros_like(st_ref)

    log_a = _log_decay(glr_ref[...], w2_ref, b_ref)
    row = lax.broadcasted_iota(jnp.int32, (chunk, chunk), 0)
    col = lax.broadcasted_iota(jnp.int32, (chunk, chunk), 1)
    causal = row >= col
    tri = jnp.where(causal, 1.0, 0.0).astype(BF16)
    hi, mid, lo = _split_bf16(log_a)
    cum_all = _dot(tri, hi) + _dot(tri, mid) + _dot(tri, lo)
    steep = jnp.max(-cum_all[chunk - 1:chunk, :]) > GLA_MAX_CHUNK_DECAY

    @pl.when(jnp.logical_not(steep))
    def _():
        for h in range(n_heads):
            cum = cum_all[:, h * dk:(h + 1) * dk]
            last = cum[chunk - 1:chunk, :]
            q = q_ref[:, h * dk:(h + 1) * dk] * scale
            k = k_ref[:, h * dk:(h + 1) * dk]
            v = v_ref[:, h * dv:(h + 1) * dv]
            st = st_ref[h]
            qt = (q * jnp.exp(cum)).astype(BF16)
            kt = (k * jnp.exp(-cum)).astype(BF16)
            att = jnp.where(causal, _dot_nt(qt, kt), 0.0)
            of_ref[:, h * dv:(h + 1) * dv] = (_dot(att.astype(BF16), v.astype(BF16))
                                              + _dot_nt(qt, st.astype(BF16)))
            kd = (k * jnp.exp(last - cum)).astype(BF16)
            st_ref[h] = st * jnp.exp(last) + _dot(v.T.astype(BF16), kd)

    @pl.when(steep)
    def _():
        la_ref[...] = log_a
        eye = (lax.broadcasted_iota(jnp.int32, (dv, dv), 0)
               == lax.broadcasted_iota(jnp.int32, (dv, dv), 1))

        sub = lax.broadcasted_iota(jnp.int32, (8, dv), 0)

        def tokens(g, carry):
            r0 = pl.multiple_of(g * 8, 8)
            a8 = jnp.exp(la_ref[pl.ds(r0, 8), :])
            q8 = q_ref[pl.ds(r0, 8), :] * scale
            k8 = k_ref[pl.ds(r0, 8), :]
            v8 = v_ref[pl.ds(r0, 8), :]
            for h in range(n_heads):
                st = st_ref[h]
                o8 = jnp.zeros((8, dv), F32)
                for r in range(8):
                    v_t = v8[r:r + 1, h * dv:(h + 1) * dv]
                    v_col = jnp.sum(jnp.where(eye, jnp.broadcast_to(v_t, (dv, dv)), 0.0),
                                    axis=-1, keepdims=True)
                    st = st * a8[r:r + 1, h * dk:(h + 1) * dk] + v_col * k8[r:r + 1, h * dk:(h + 1) * dk]
                    o_col = jnp.sum(st * q8[r:r + 1, h * dk:(h + 1) * dk], axis=-1, keepdims=True)
                    o_row = jnp.sum(jnp.where(eye, jnp.broadcast_to(o_col, (dv, dv)), 0.0),
                                    axis=0, keepdims=True)
                    o8 = jnp.where(sub == r, o_row, o8)
                st_ref[h] = st
                of_ref[pl.ds(r0, 8), h * dv:(h + 1) * dv] = o8
            return carry

        lax.fori_loop(0, chunk // 8, tokens, 0)

    for h in range(n_heads):
        o_ref[:, h * dv:(h + 1) * dv] = _gla_out(
            of_ref[:, h * dv:(h + 1) * dv], gn_ref, gr_ref[:, h * dv:(h + 1) * dv]).astype(o_ref.dtype)

    @pl.when(c == pl.num_programs(1) - 1)
    def _():
        for h in range(n_heads):
            s_ref[0, h] = st_ref[h].T


def _gla_prompt(z, glr, w2p, bias, gla_norm, *, batch, seq, n_heads, dk, dv, col_q, col_k, col_v,
                col_gr, chunk=256):
    assert seq % chunk == 0
    nc = seq // chunk
    wk, wv = n_heads * dk, n_heads * dv
    assert col_q % wk == 0 and col_k % wk == 0 and col_v % wv == 0 and col_gr % wv == 0
    bq, bk, bv, bg = col_q // wk, col_k // wk, col_v // wv, col_gr // wv
    est = (2 * chunk * (2 * wk + 2 * wv + LANE) * 4 + 2 * chunk * wv * 2 + 3 * n_heads * dk * dv * 4
           + chunk * (wk + wv) * 4 + 4 * chunk * wk * 4 + 8 * chunk * chunk * 4 + 12 * chunk * dv * 4)
    return pl.pallas_call(
        functools.partial(_gla_prompt_kernel, chunk=chunk, n_heads=n_heads, dk=dk, dv=dv),
        name="gla_prompt",
        grid=(batch, nc),
        in_specs=[
            pl.BlockSpec((chunk, wk), lambda b, c: (b * nc + c, bq)),
            pl.BlockSpec((chunk, wk), lambda b, c: (b * nc + c, bk)),
            pl.BlockSpec((chunk, wv), lambda b, c: (b * nc + c, bv)),
            pl.BlockSpec((chunk, wv), lambda b, c: (b * nc + c, bg)),
            pl.BlockSpec((chunk, LANE), lambda b, c: (b * nc + c, 0)),
            pl.BlockSpec((LANE, wk), lambda b, c: (0, 0)),
            pl.BlockSpec((1, wk), lambda b, c: (0, 0)),
            pl.BlockSpec((1, dv), lambda b, c: (0, 0)),
        ],
        out_specs=[
            pl.BlockSpec((chunk, wv), lambda b, c: (b * nc + c, 0)),
            pl.BlockSpec((1, n_heads, dk, dv), lambda b, c: (b, 0, 0, 0)),
        ],
        out_shape=[jax.ShapeDtypeStruct((batch * seq, wv), BF16),
                   jax.ShapeDtypeStruct((batch, n_heads, dk, dv), F32)],
        scratch_shapes=[pltpu.VMEM((n_heads, dv, dk), F32),
                        pltpu.VMEM((chunk, wk), F32),
                        pltpu.VMEM((chunk, wv), F32)],
        compiler_params=pltpu.CompilerParams(
            dimension_semantics=("parallel", "arbitrary"),
            vmem_limit_bytes=_vmem_limit(est)),
    )(z, z, z, z, glr, w2p, bias, gla_norm)


def _gla_sample_kernel(z_ref, glr_ref, s0_ref, w2_ref, b_ref, gn_ref, o_ref, s_ref,
                       *, n_seq, n_heads, dk, dv, dec, col_q, col_k, col_v, col_gr):
    for s in range(n_seq):
        _gla_sample_one(z_ref.at[s], glr_ref.at[s], s0_ref.at[s], w2_ref, b_ref, gn_ref,
                        o_ref.at[s], s_ref.at[s], n_heads=n_heads, dk=dk, dv=dv, dec=dec,
                        col_q=col_q, col_k=col_k, col_v=col_v, col_gr=col_gr)


def _gla_sample_one(z_ref, glr_ref, s0_ref, w2_ref, b_ref, gn_ref, o_ref, s_ref,
                    *, n_heads, dk, dv, dec, col_q, col_k, col_v, col_gr):
    log_a_all = _log_decay(glr_ref[...], w2_ref, b_ref)
    t_k = lax.broadcasted_iota(jnp.int32, (dec, dk), 0)
    t_v = lax.broadcasted_iota(jnp.int32, (dec, dv), 0)
    eye = (lax.broadcasted_iota(jnp.int32, (dk, dk), 0)
           == lax.broadcasted_iota(jnp.int32, (dk, dk), 1))

    def to_col(r):
        return jnp.sum(jnp.where(eye, jnp.broadcast_to(r, (dk, dk)), 0.0), axis=-1, keepdims=True)

    for h in range(n_heads):
        log_a = log_a_all[:, h * dk:(h + 1) * dk]
        cum = jnp.zeros((dec, dk), F32)
        for t in range(dec):
            cum = cum + jnp.where(t_k >= t, log_a[t:t + 1], 0.0)
        last = cum[dec - 1:dec]
        q = z_ref[:, col_q + h * dk: col_q + (h + 1) * dk] * (dk ** -0.5)
        k = z_ref[:, col_k + h * dk: col_k + (h + 1) * dk]
        v = z_ref[:, col_v + h * dv: col_v + (h + 1) * dv]
        gr = z_ref[:, col_gr + h * dv: col_gr + (h + 1) * dv]
        s0 = s0_ref[h]

        o = _dot((q * jnp.exp(cum)).astype(BF16), s0.astype(BF16))
        for t in range(dec):
            o_t = jnp.zeros((1, dv), F32)
            for j in range(t + 1):
                w = jnp.exp(cum[t:t + 1] - cum[j:j + 1])
                a = jnp.sum(q[t:t + 1] * k[j:j + 1] * w, axis=-1, keepdims=True)
                o_t = o_t + a * v[j:j + 1]
            o = o + jnp.where(t_v == t, o_t, 0.0)

        kd = (k * jnp.exp(last - cum)).astype(BF16)
        upd = lax.dot_general(kd, v.astype(BF16), (((0,), (0,)), ((), ())),
                              preferred_element_type=F32)
        s_ref[h] = s0 * to_col(jnp.exp(last)) + upd
        o_ref[:, h * dv:(h + 1) * dv] = _gla_out(o, gn_ref, gr).astype(o_ref.dtype)


def _gla_sample(z3, glr3, state, w2p, bias, gla_norm, *, n_heads, dk, dv, col_q, col_k, col_v,
                col_gr, n_seq=4):
    dec_b, dec, zw = z3.shape
    assert dec_b % n_seq == 0
    est = n_seq * (2 * dec * zw * 4 + 2 * 2 * n_heads * dk * dv * 4) + LANE * n_heads * dk * 2 \
        + 8 * dk * dv * 4
    return pl.pallas_call(
        functools.partial(_gla_sample_kernel, n_seq=n_seq, n_heads=n_heads, dk=dk, dv=dv, dec=dec,
                          col_q=col_q, col_k=col_k, col_v=col_v, col_gr=col_gr),
        name="gla_sample",
        grid=(dec_b // n_seq,),
        in_specs=[
            pl.BlockSpec((n_seq, dec, zw), lambda b: (b, 0, 0)),
            pl.BlockSpec((n_seq, dec, LANE), lambda b: (b, 0, 0)),
            pl.BlockSpec((n_seq, n_heads, dk, dv), lambda b: (b, 0, 0, 0)),
            pl.BlockSpec((LANE, n_heads * dk), lambda b: (0, 0)),
            pl.BlockSpec((1, n_heads * dk), lambda b: (0, 0)),
            pl.BlockSpec((1, dv), lambda b: (0, 0)),
        ],
        out_specs=[
            pl.BlockSpec((n_seq, dec, n_heads * dv), lambda b: (b, 0, 0)),
            pl.BlockSpec((n_seq, n_heads, dk, dv), lambda b: (b, 0, 0, 0)),
        ],
        out_shape=[jax.ShapeDtypeStruct((dec_b, dec, n_heads * dv), BF16),
                   jax.ShapeDtypeStruct((dec_b, n_heads, dk, dv), F32)],
        compiler_params=pltpu.CompilerParams(
            dimension_semantics=("parallel",),
            vmem_limit_bytes=_vmem_limit(est)),
    )(z3, glr3, state, w2p, bias, gla_norm)


def _round_up(x, m):
    return (x + m - 1) // m * m


def kernel(x_prompt, x_sample, cache_k, cache_v, state_gla, page_table, norm_ffn1, ffn1_w_gate, ffn1_w_up, ffn1_w_down, norm_mix, w_in, gla_gate_w2, gla_gate_b, lambda_q1, lambda_k1, lambda_q2, lambda_k2, diff_norm, gla_norm, w_out, norm_ffn2, ffn2_w_gate, ffn2_w_up, ffn2_w_down, norm_final):
    batch, seq, d = x_prompt.shape
    dec_b, dec, _ = x_sample.shape
    depth = norm_ffn1.shape[0]
    n_heads, hd = cache_k.shape[3], cache_k.shape[5]
    vd = cache_v.shape[4]
    g_heads, dk, dv = state_gla.shape[2:]
    rank = gla_gate_w2.shape[1]
    d_ff = ffn1_w_gate.shape[2]

    w_qk = n_heads * 2 * hd
    sizes = (w_qk, w_qk, n_heads * vd, g_heads * dk, g_heads * dk, g_heads * dv, g_heads * dv, rank)
    cols = [0]
    for s in sizes:
        cols.append(cols[-1] + s)
    col_dq, col_dk, col_dv, col_gq, col_gk, col_gv, col_gr, col_glr = cols[:8]
    assert w_in.shape[2] == cols[8] and rank <= LANE
    ff_pad = _round_up(d_ff, 512)
    diff_w = n_heads * vd

    yp = x_prompt.reshape(batch * seq, d)
    ys = x_sample.reshape(dec_b * dec, d)
    nf = norm_final.reshape(1, d)
    outs = [[] for _ in range(6)]

    for layer in range(depth):
        lam_init = _lambda_init(layer)
        ffn_w = []
        for wg, wu, wd in ((ffn1_w_gate, ffn1_w_up, ffn1_w_down), (ffn2_w_gate, ffn2_w_up, ffn2_w_down)):
            ffn_w.append((_to_bf16(wg, layer, d, ff_pad), _to_bf16(wu, layer, d, ff_pad),
                          _to_bf16(wd, layer, ff_pad, d)))
        w_main, w_glr = _w_in_prep(w_in, layer, col_glr)
        w2p = jnp.pad(gla_gate_w2[layer], ((0, LANE - rank), (0, 0))).astype(BF16)
        gate_b = gla_gate_b[layer].reshape(1, -1)
        lam_p = jnp.stack([lambda_q1[layer], lambda_k1[layer], lambda_q2[layer], lambda_k2[layer]])
        dn = diff_norm[layer].reshape(1, vd)
        gn = gla_norm[layer].reshape(1, dv)
        wo = _to_bf16(w_out, layer, w_out.shape[1], d)
        g1 = norm_ffn1[layer].reshape(1, d)
        gm = norm_mix[layer].reshape(1, d)
        g2 = norm_ffn2[layer].reshape(1, d)
        last = layer == depth - 1

        def pre(x):
            x, hmix = _ffn(x, g1, *ffn_w[0], gm, tail="next")
            return (x,) + tuple(_inproj(hmix, w_main, w_glr, n_heads=n_heads, hd=hd, col_k=col_dk,
                                        col_v=col_dv))

        def post(x, mix_a, mix_b):
            x = _outproj(x, mix_a, mix_b, wo)
            return _ffn(x, g2, *ffn_w[1], nf, tail="final" if last else "none")

        def kv_out(k_rows, v_rows, b, l):
            k = k_rows.reshape(b, l, n_heads, 2, hd)
            v = v_rows.reshape(b, l, 2, n_heads, hd).transpose(0, 1, 3, 2, 4).reshape(b, l, n_heads, vd)
            return k, v

        yp, zp, glr_p, k_rows, v_rows = pre(yp)
        mix_a = _attn_prompt(zp, lam_p, dn, batch=batch, seq=seq, n_heads=n_heads, hd=hd, vd=vd,
                             col_q=col_dq, col_k=col_dk, col_v=col_dv, lam_init=lam_init)
        mix_b, s_p = _gla_prompt(zp, glr_p, w2p, gate_b, gn, batch=batch, seq=seq, n_heads=g_heads,
                                 dk=dk, dv=dv, col_q=col_gq, col_k=col_gk, col_v=col_gv, col_gr=col_gr)
        yp = post(yp, mix_a, mix_b)
        k_new, v_new = kv_out(k_rows, v_rows, batch, seq)
        outs[0].append(k_new)
        outs[1].append(v_new)
        outs[2].append(s_p.astype(state_gla.dtype))

        ys, zs, glr_s, k_rows, v_rows = pre(ys)
        zs3 = zs.reshape(dec_b, dec, col_glr)
        mix_a = _attn_sample(zs3, cache_k, cache_v, page_table, layer, lam_p, dn, n_heads=n_heads,
                             hd=hd, vd=vd, col_q=col_dq, col_k=col_dk, col_v=col_dv,
                             lam_init=lam_init)
        mix_b, s_s = _gla_sample(zs3, glr_s.reshape(dec_b, dec, LANE), state_gla[layer], w2p, gate_b, gn,
                                 n_heads=g_heads, dk=dk, dv=dv, col_q=col_gq, col_k=col_gk,
                                 col_v=col_gv, col_gr=col_gr)
        ys = post(ys, mix_a.reshape(dec_b * dec, -1), mix_b.reshape(dec_b * dec, -1))
        k_new, v_new = kv_out(k_rows, v_rows, dec_b, dec)
        outs[3].append(k_new)
        outs[4].append(v_new)
        outs[5].append(s_s.astype(state_gla.dtype))

    if depth == 0:
        raise ValueError("depth must be positive")
    y_prompt = yp.reshape(batch, seq, d)
    y_sample = ys.reshape(dec_b, dec, d)
    return (y_prompt, y_sample) + tuple(jnp.stack(o) for o in outs)
```

```python
import functools
import math

import jax
import jax.numpy as jnp
from jax import lax
from jax.experimental import pallas as pl
from jax.experimental.pallas import tpu as pltpu

F32 = jnp.float32
BF16 = jnp.bfloat16

NORM_EPS = 1e-6
GLA_TAU = 16.0
LOG2_E = math.log2(math.e)
LANE = 128
VMEM_PHYSICAL = 64 * 2**20


def _lambda_init(layer):
    return 0.8 - 0.6 * math.exp(-0.3 * layer)


def _vmem_limit(nbytes):
    return int(min(nbytes * 5 // 4 + (4 << 20), VMEM_PHYSICAL - (8 << 20)))


def _rms(x, gain):
    return x * lax.rsqrt(jnp.mean(x * x, axis=-1, keepdims=True) + NORM_EPS) * gain


def _dot(a, b):
    return jnp.dot(a, b, preferred_element_type=F32)


def _dot_nt(a, b):
    return lax.dot_general(a, b, (((1,), (1,)), ((), ())), preferred_element_type=F32)


def _to_bf16_kernel(w_ref, o_ref, *, rows, cols, tr):
    r_out, c_out = o_ref.shape
    c_copy = min(cols, c_out)
    w = w_ref[0][:, :c_copy]
    if rows % tr:
        r_idx = pl.program_id(0) * tr + lax.broadcasted_iota(jnp.int32, (r_out, c_copy), 0)
        w = jnp.where(r_idx < rows, w, 0.0)
    o_ref[:, :c_copy] = w.astype(BF16)
    if c_copy < c_out:
        o_ref[:, c_copy:] = jnp.zeros((r_out, c_out - c_copy), BF16)


def _to_bf16(w, layer, rows_out, cols_out, *, tr=256):
    _, rows, cols = w.shape
    assert rows_out % tr == 0 and rows_out >= rows and min(cols, cols_out) % LANE == 0
    est = 2 * tr * cols * 4 + 2 * tr * cols_out * 2 + tr * cols * 4
    return pl.pallas_call(
        functools.partial(_to_bf16_kernel, rows=rows, cols=cols, tr=tr),
        name="to_bf16",
        grid=(rows_out // tr,),
        in_specs=[pl.BlockSpec((1, tr, cols), lambda i: (layer, i, 0))],
        out_specs=pl.BlockSpec((tr, cols_out), lambda i: (i, 0)),
        out_shape=jax.ShapeDtypeStruct((rows_out, cols_out), BF16),
        compiler_params=pltpu.CompilerParams(
            dimension_semantics=("parallel",),
            vmem_limit_bytes=_vmem_limit(est)),
    )(w)


def _w_in_prep_kernel(wt_ref, wm_ref, wl_ref, *, n_main_blocks, rank, tr):
    j = pl.program_id(0)

    @pl.when(j < n_main_blocks)
    def _():
        wm_ref[...] = wt_ref[0].T.astype(BF16)

    @pl.when(j == n_main_blocks)
    def _():
        tail = wt_ref[0][:LANE]
        r_idx = lax.broadcasted_iota(jnp.int32, tail.shape, 0)
        wl_ref[...] = jnp.where(r_idx < rank, tail, 0.0).T.astype(BF16)


def _w_in_prep(w_in, layer, n_main, *, tr=512):
    wt = jnp.swapaxes(w_in, 1, 2)
    _, n, d = wt.shape
    rank = n - n_main
    nb = n_main // tr
    assert n_main % tr == 0 and 0 < rank <= LANE <= tr
    est = 2 * tr * d * 4 + 2 * d * (tr + LANE) * 2 + 2 * tr * d * 4
    return pl.pallas_call(
        functools.partial(_w_in_prep_kernel, n_main_blocks=nb, rank=rank, tr=tr),
        name="w_in_prep",
        grid=(nb + 1,),
        in_specs=[pl.BlockSpec((1, tr, d), lambda j: (layer, j, 0))],
        out_specs=[pl.BlockSpec((d, tr), lambda j: (0, jnp.minimum(j, nb - 1))),
                   pl.BlockSpec((d, LANE), lambda j: (0, 0))],
        out_shape=[jax.ShapeDtypeStruct((d, n_main), BF16), jax.ShapeDtypeStruct((d, LANE), BF16)],
        compiler_params=pltpu.CompilerParams(
            dimension_semantics=("arbitrary",),
            vmem_limit_bytes=_vmem_limit(est)),
    )(wt)


def _ffn_kernel(x_ref, g_ref, wg_ref, wu_ref, wd_ref, gt_ref, *rest, tail):
    o_ref = rest[0]
    h_ref, acc_ref = rest[-2:]
    j = pl.program_id(1)

    @pl.when(j == 0)
    def _():
        h_ref[...] = _rms(x_ref[...], g_ref[...]).astype(BF16)
        acc_ref[...] = jnp.zeros_like(acc_ref)

    h = h_ref[...]
    gate = _dot(h, wg_ref[...])
    up = _dot(h, wu_ref[...])
    act = (gate * jax.nn.sigmoid(gate)) * up
    acc_ref[...] += _dot(act.astype(BF16), wd_ref[...])

    @pl.when(j == pl.num_programs(1) - 1)
    def _():
        y = x_ref[...] + 0.5 * acc_ref[...]
        if tail == "final":
            y = _rms(y, gt_ref[...])
        o_ref[...] = y
        if tail == "next":
            rest[1][...] = _rms(y, gt_ref[...]).astype(BF16)


def _ffn(x, gain, wg, wu, wd, gain_tail, *, tail, tm=512, tf=512):
    m, d = x.shape
    fpad = wg.shape[1]
    assert m % tm == 0 and fpad % tf == 0 and tail in ("final", "next", "none")
    est = (2 * 2 * tm * d * 4
           + 2 * tm * d * 2
           + tm * d * (2 + 4)
           + 2 * 3 * d * tf * 2
           + 3 * tm * tf * 4)
    row_spec = pl.BlockSpec((tm, d), lambda i, j: (i, 0))
    out_specs, out_shape = row_spec, jax.ShapeDtypeStruct((m, d), F32)
    if tail == "next":
        out_specs, out_shape = [row_spec, row_spec], [out_shape, jax.ShapeDtypeStruct((m, d), BF16)]
    return pl.pallas_call(
        functools.partial(_ffn_kernel, tail=tail),
        name="ffn_" + tail,
        grid=(m // tm, fpad // tf),
        in_specs=[
            row_spec,
            pl.BlockSpec((1, d), lambda i, j: (0, 0)),
            pl.BlockSpec((d, tf), lambda i, j: (0, j)),
            pl.BlockSpec((d, tf), lambda i, j: (0, j)),
            pl.BlockSpec((tf, d), lambda i, j: (j, 0)),
            pl.BlockSpec((1, d), lambda i, j: (0, 0)),
        ],
        out_specs=out_specs,
        out_shape=out_shape,
        scratch_shapes=[pltpu.VMEM((tm, d), BF16), pltpu.VMEM((tm, d), F32)],
        compiler_params=pltpu.CompilerParams(
            dimension_semantics=("parallel", "arbitrary"),
            vmem_limit_bytes=_vmem_limit(est)),
    )(x, gain, wg, wu, wd, gain_tail)


def _inproj_kernel(h_ref, w_ref, wl_ref, z_ref, glr_ref, kr_ref, vr_ref, kv16_ref,
                   *, tm, n_heads, hd, k_tile, v_tile):
    j = pl.program_id(1)
    pitch = 2 * n_heads
    z_ref[...] = _dot(h_ref[...], w_ref[...])

    @pl.when(j == 0)
    def _():
        glr_ref[...] = _dot(h_ref[...], wl_ref[...])

    @pl.when(j == k_tile)
    def _():
        kv16_ref[...] = z_ref[...].astype(BF16)
        for c in range(pitch):
            kr_ref[pl.ds(c, tm, stride=pitch), :] = z_ref[:, c * hd:(c + 1) * hd]

    @pl.when(j == v_tile)
    def _():
        kv16_ref[...] = z_ref[...].astype(BF16)
        for h in range(n_heads):
            for half in range(2):
                c = 2 * h + half
                vr_ref[pl.ds(n_heads * half + h, tm, stride=pitch), :] = z_ref[:, c * hd:(c + 1) * hd]


def _inproj(h, w, wl, *, n_heads, hd, col_k, col_v, tm=1024):
    m, d = h.shape
    n = w.shape[1]
    tn = n_heads * 2 * hd
    tm = min(tm, m)
    pitch = 2 * n_heads
    assert m % tm == 0 and n % tn == 0 and col_k % tn == 0 and col_v == col_k + tn and hd == LANE
    k_tile = col_k // tn
    est = (2 * tm * d * 2 + 2 * d * (tn + LANE) * 2 + 2 * tm * (tn + LANE) * 4
           + 2 * 2 * tm * pitch * hd * 4 + tm * tn * 4 + 2 * tm * tn * 2)
    return pl.pallas_call(
        functools.partial(_inproj_kernel, tm=tm, n_heads=n_heads, hd=hd, k_tile=k_tile,
                          v_tile=k_tile + 1),
        name="in_proj",
        grid=(m // tm, n // tn),
        in_specs=[
            pl.BlockSpec((tm, d), lambda i, j: (i, 0)),
            pl.BlockSpec((d, tn), lambda i, j: (0, j)),
            pl.BlockSpec((d, LANE), lambda i, j: (0, 0)),
        ],
        out_specs=[
            pl.BlockSpec((tm, tn), lambda i, j: (i, j)),
            pl.BlockSpec((tm, LANE), lambda i, j: (i, 0)),
            pl.BlockSpec((tm * pitch, hd), lambda i, j: (i, 0)),
            pl.BlockSpec((tm * pitch, hd), lambda i, j: (i, 0)),
            pl.BlockSpec((tm, tn), lambda i, j: (i, jnp.clip(j - k_tile, 0, 1))),
        ],
        out_shape=[jax.ShapeDtypeStruct((m, n), F32),
                   jax.ShapeDtypeStruct((m, LANE), F32),
                   jax.ShapeDtypeStruct((m * pitch, hd), F32),
                   jax.ShapeDtypeStruct((m * pitch, hd), F32),
                   jax.ShapeDtypeStruct((m, 2 * tn), BF16)],
        compiler_params=pltpu.CompilerParams(
            dimension_semantics=("parallel", "arbitrary"),
            vmem_limit_bytes=_vmem_limit(est)),
    )(h, w, wl)


def _outproj_kernel(x_ref, a_ref, b_ref, wa_ref, wb_ref, o_ref):
    o_ref[...] = x_ref[...] + _dot(a_ref[...], wa_ref[...]) + _dot(b_ref[...], wb_ref[...])


def _outproj(x, mix_a, mix_b, w, *, tm=512):
    m, d = x.shape
    ka, kb = mix_a.shape[1], mix_b.shape[1]
    assert m % tm == 0 and ka == kb and w.shape == (ka + kb, d)
    est = 2 * 2 * tm * d * 4 + 2 * tm * (ka + kb) * 2 + 2 * (ka + kb) * d * 2 + tm * d * 4
    return pl.pallas_call(
        _outproj_kernel,
        name="out_proj",
        grid=(m // tm,),
        in_specs=[
            pl.BlockSpec((tm, d), lambda i: (i, 0)),
            pl.BlockSpec((tm, ka), lambda i: (i, 0)),
            pl.BlockSpec((tm, kb), lambda i: (i, 0)),
            pl.BlockSpec((ka, d), lambda i: (0, 0)),
            pl.BlockSpec((kb, d), lambda i: (1, 0)),
        ],
        out_specs=pl.BlockSpec((tm, d), lambda i: (i, 0)),
        out_shape=jax.ShapeDtypeStruct((m, d), F32),
        compiler_params=pltpu.CompilerParams(
            dimension_semantics=("parallel",),
            vmem_limit_bytes=_vmem_limit(est)),
    )(x, mix_a, mix_b, w, w)


def _diff_lambda(lam_ref, lam_init):
    lp = lam_ref[...]
    e1 = jnp.exp(jnp.sum(lp[0:1] * lp[1:2], axis=-1, keepdims=True))
    e2 = jnp.exp(jnp.sum(lp[2:3] * lp[3:4], axis=-1, keepdims=True))
    return e1 - e2 + lam_init


def _alibi_slope(h, n_heads):
    slope = F32(0.0)
    for i in range(n_heads):
        slope = jnp.where(h == i, F32(2.0 ** (-8.0 * (i + 1) / n_heads)), slope)
    return slope


def _lane_fold(x, op):
    parts = [x[:, i * LANE:(i + 1) * LANE] for i in range(x.shape[1] // LANE)]
    return functools.reduce(op, parts)


def _attn_prompt_kernel(q_ref, k_ref, v_ref, lam_ref, dn_ref, o_ref, s_ref, m_ref, l_ref, acc_ref,
                        *, tq, tk, hd, n_heads, lam_init):
    h = pl.program_id(1)
    qi = pl.program_id(2)
    scale = hd ** -0.5 * LOG2_E
    slope = _alibi_slope(h, n_heads) * LOG2_E
    n_diag = tq // tk
    n_full = qi * n_diag
    q = q_ref[...]
    zero = jnp.zeros((tq, hd), F32)
    qbd = jnp.concatenate([jnp.concatenate([q[:, :hd], zero], axis=1),
                           jnp.concatenate([zero, q[:, hd:]], axis=1)], axis=0).astype(BF16)
    row = lax.broadcasted_iota(jnp.int32, (tq, tk), 0)
    col = lax.broadcasted_iota(jnp.int32, (tq, tk), 1)
    d0 = (row - col).astype(F32)

    m_ref[...] = jnp.full_like(m_ref, -jnp.inf)

    def scores(c, masked):
        ks = pl.multiple_of(c * tk, tk)
        kc = k_ref[pl.ds(ks, tk), :]
        off = jnp.full((1, 1), qi * tq - c * tk, jnp.int32).astype(F32)
        dist = d0 + off
        bias = slope * dist
        s = _dot_nt(qbd, kc) * scale - jnp.concatenate([bias, bias], axis=0)
        if masked:
            s = jnp.where(jnp.concatenate([dist, dist], axis=0) >= 0, s, -jnp.inf)
        s_ref[c] = s
        m_ref[...] = jnp.maximum(m_ref[...], _lane_fold(s, jnp.maximum))

    def scores_body(c, carry):
        scores(c, False)
        return carry

    lax.fori_loop(0, n_full, scores_body, 0)
    for j in range(n_diag):
        scores(n_full + j, True)

    m_row = jnp.max(m_ref[...], axis=-1, keepdims=True)
    m_ref[...] = jnp.broadcast_to(m_row, m_ref.shape)
    l_ref[...] = jnp.zeros_like(l_ref)
    acc_ref[...] = jnp.zeros_like(acc_ref)

    def pv_body(c, carry):
        ks = pl.multiple_of(c * tk, tk)
        mb = m_ref[...]
        p = jnp.exp2(s_ref[c] - jnp.concatenate([mb] * (tk // LANE), axis=1))
        l_ref[...] += _lane_fold(p, jnp.add)
        acc_ref[...] += _dot(p.astype(BF16), v_ref[pl.ds(ks, tk), :])
        return carry

    lax.fori_loop(0, n_full + n_diag, pv_body, 0)

    lam = _diff_lambda(lam_ref, lam_init)
    on = acc_ref[...] / jnp.sum(l_ref[...], axis=-1, keepdims=True)
    o = on[:tq] - lam * on[tq:]
    o_ref[...] = (_rms(o, dn_ref[...]) * (1.0 - lam_init)).astype(o_ref.dtype)


def _attn_both_kernel(*refs, n_pages, prompt, sample):
    pt_ref = refs[0]
    q_ref, k_ref, v_ref, lam_ref, dn_ref, z_ref = refs[1:7]
    pages = refs[7:7 + 2 * n_pages]
    op_ref, os_ref = refs[7 + 2 * n_pages:9 + 2 * n_pages]
    s_ref, m_ref, l_ref, acc_ref, q2_ref, kn_ref, vn_ref = refs[9 + 2 * n_pages:]
    _attn_sample_kernel(pt_ref, z_ref, *pages, lam_ref, dn_ref, os_ref, q2_ref, kn_ref, vn_ref, **sample)
    _attn_prompt_kernel(q_ref, k_ref, v_ref, lam_ref, dn_ref, op_ref, s_ref, m_ref, l_ref, acc_ref,
                        **prompt)


def _attention(z, kv16, z3, cache_k, cache_v, page_table, layer, lam_p, diff_norm, *, batch, seq,
               n_heads, hd, vd, col_q, col_k, col_v, lam_init, tq=512, tk=512):
    dec_b, dec, zw = z3.shape
    n_pages = page_table.shape[1]
    depth, n_pool, page = cache_k.shape[:3]
    nq = seq // tq
    cw = 2 * hd
    width = n_heads * cw
    prow = page * n_heads * 2
    rows = 2 * dec
    assert vd == 2 * hd and seq % tq == 0 and tq % tk == 0 and tk % LANE == 0 and col_q % cw == 0
    assert dec_b == batch * n_heads * nq
    kf = cache_k.reshape(depth * n_pool, prow, hd)
    vf = cache_v.reshape(depth * n_pool, page, n_heads, 2, hd).transpose(0, 1, 3, 2, 4).reshape(
        depth * n_pool, prow, hd)
    pt = (page_table + layer * n_pool).reshape(-1).astype(jnp.int32)

    def sample_idx(b, h, i):
        return (b * n_heads + h) * nq + i

    def page_spec(p):
        return pl.BlockSpec((1, prow, hd),
                            lambda b, h, i, pt_ref: (pt_ref[sample_idx(b, h, i) * n_pages + p], 0, 0))

    est = (2 * tq * cw * 4 + 2 * 2 * seq * cw * 2 + 2 * tq * vd * 2
           + (seq // tk) * 2 * tq * tk * 4 + 2 * 2 * tq * LANE * 4 + 2 * tq * vd * 4
           + 3 * 2 * tq * tk * 4
           + 2 * dec * zw * 4 + 2 * 2 * n_pages * page * width * 4 + 2 * dec * width * 2
           + rows * 2 * hd * 4 + 2 * page * width * 4)
    grid_spec = pltpu.PrefetchScalarGridSpec(
        num_scalar_prefetch=1,
        grid=(batch, n_heads, nq),
        in_specs=([pl.BlockSpec((tq, cw), lambda b, h, i, pt_ref: (b * nq + i, col_q // cw + h)),
                   pl.BlockSpec((seq, cw), lambda b, h, i, pt_ref: (b, h)),
                   pl.BlockSpec((seq, cw), lambda b, h, i, pt_ref: (b, n_heads + h)),
                   pl.BlockSpec((4, hd), lambda b, h, i, pt_ref: (0, 0)),
                   pl.BlockSpec((1, vd), lambda b, h, i, pt_ref: (0, 0)),
                   pl.BlockSpec((1, dec, zw), lambda b, h, i, pt_ref: (sample_idx(b, h, i), 0, 0))]
                  + [page_spec(p) for p in range(n_pages)]
                  + [page_spec(p) for p in range(n_pages)]),
        out_specs=[pl.BlockSpec((tq, vd), lambda b, h, i, pt_ref: (b * nq + i, h)),
                   pl.BlockSpec((1, dec, width), lambda b, h, i, pt_ref: (sample_idx(b, h, i), 0, 0))],
        scratch_shapes=[pltpu.VMEM((seq // tk, 2 * tq, tk), F32),
                        pltpu.VMEM((2 * tq, LANE), F32), pltpu.VMEM((2 * tq, LANE), F32),
                        pltpu.VMEM((2 * tq, vd), F32),
                        pltpu.VMEM((rows, 2 * hd), F32),
                        pltpu.VMEM((page, width), F32),
                        pltpu.VMEM((page, width), F32)],
    )
    return pl.pallas_call(
        functools.partial(
            _attn_both_kernel, n_pages=n_pages,
            prompt=dict(tq=tq, tk=tk, hd=hd, n_heads=n_heads, lam_init=lam_init),
            sample=dict(n_pages=n_pages, page=page, n_heads=n_heads, hd=hd, dec=dec, col_q=col_q,
                        col_k=col_k, col_v=col_v, lam_init=lam_init)),
        name="attention",
        grid_spec=grid_spec,
        out_shape=[jax.ShapeDtypeStruct((batch * seq, width), BF16),
                   jax.ShapeDtypeStruct((dec_b, dec, width), BF16)],
        compiler_params=pltpu.CompilerParams(
            dimension_semantics=("parallel", "parallel", "arbitrary"),
            vmem_limit_bytes=_vmem_limit(est)),
    )(pt, z, kv16, kv16, lam_p, diff_norm, z3, *([kf] * n_pages), *([vf] * n_pages))


def _attn_sample_kernel(*refs, n_pages, page, n_heads, hd, dec, col_q, col_k, col_v, lam_init):
    z_ref = refs[1]
    kp_refs = refs[2:2 + n_pages]
    vp_refs = refs[2 + n_pages:2 + 2 * n_pages]
    pitch = 2 * n_heads
    lam_ref, dn_ref, o_ref, q2_ref, kn_ref, vn_ref = refs[2 + 2 * n_pages:]
    cw = 2 * hd
    rows = 2 * dec
    scale = hd ** -0.5 * LOG2_E
    past = n_pages * page
    n_keys = past + page
    r_iota = lax.broadcasted_iota(jnp.int32, (rows, n_keys), 0)
    j_iota = lax.broadcasted_iota(jnp.int32, (rows, n_keys), 1)
    q_idx = jnp.where(r_iota >= dec, r_iota - dec, r_iota)
    dist = (past + q_idx - j_iota).astype(F32)
    valid = j_iota <= past + q_idx
    rr = lax.broadcasted_iota(jnp.int32, (rows, cw), 0)
    cc = lax.broadcasted_iota(jnp.int32, (rows, cw), 1)
    keep = (rr < dec) == (cc < hd)
    lam = _diff_lambda(lam_ref, lam_init)

    width = n_heads * cw
    kn_ref[...] = jnp.zeros_like(kn_ref)
    vn_ref[...] = jnp.zeros_like(vn_ref)
    kn_ref[0:dec, :] = z_ref[0, :, col_k:col_k + width]
    vn_ref[0:dec, :] = z_ref[0, :, col_v:col_v + width]

    for h in range(n_heads):
        slope = 2.0 ** (-8.0 * (h + 1) / n_heads) * LOG2_E
        qh = z_ref[0, :, col_q + h * cw: col_q + (h + 1) * cw]
        q2_ref[0:dec, :] = qh
        q2_ref[dec:rows, :] = qh
        qbd = jnp.where(keep, q2_ref[...], 0.0).astype(BF16)
        parts = []
        for kp in kp_refs:
            kcat = jnp.concatenate([kp[0, pl.ds(2 * h + m, page, stride=pitch), :] for m in range(2)],
                                   axis=-1)
            parts.append(_dot_nt(qbd, kcat.astype(BF16)))
        parts.append(_dot_nt(qbd, kn_ref[:, h * cw:(h + 1) * cw].astype(BF16)))
        s = jnp.concatenate(parts, axis=-1) * scale - slope * dist
        s = jnp.where(valid, s, -jnp.inf)
        p = jnp.exp2(s - jnp.max(s, axis=-1, keepdims=True))
        l = jnp.sum(p, axis=-1, keepdims=True)
        p16 = p.astype(BF16)
        acc = _dot(p16[:, past:], vn_ref[:, h * cw:(h + 1) * cw].astype(BF16))
        for i, vp in enumerate(vp_refs):
            vcat = jnp.concatenate(
                [vp[0, pl.ds(n_heads * half + h, page, stride=pitch), :] for half in range(2)], axis=-1)
            acc = acc + _dot(p16[:, i * page:(i + 1) * page], vcat.astype(BF16))
        on = acc / l
        o = on[:dec] - lam * on[dec:]
        o_ref[0, :, h * cw:(h + 1) * cw] = (_rms(o, dn_ref[...]) * (1.0 - lam_init)).astype(o_ref.dtype)


def _log_decay(glr, w2_ref, b_ref):
    x = _dot(glr.astype(BF16), w2_ref[...]) + b_ref[...]
    return (jnp.minimum(x, 0.0) - jnp.log(1.0 + jnp.exp(-jnp.abs(x)))) / GLA_TAU


def _split_bf16(x):
    hi = x.astype(BF16)
    r1 = x - hi.astype(F32)
    mid = r1.astype(BF16)
    lo = (r1 - mid.astype(F32)).astype(BF16)
    return hi, mid, lo


def _gla_out(o, gn_ref, gr):
    return _rms(o, gn_ref[...]) * (gr * jax.nn.sigmoid(gr))


GLA_MAX_CHUNK_DECAY = 60.0


def _gla_prompt_kernel(q_ref, k_ref, v_ref, gr_ref, glr_ref, w2_ref, b_ref, gn_ref,
                       o_ref, s_ref, st_ref, la_ref, of_ref, *, chunk, n_heads, dk, dv):
    c = pl.program_id(1)
    scale = dk ** -0.5

    @pl.when(c == 0)
    def _():
        st_ref[...] = jnp.zeros_like(st_ref)

    log_a = _log_decay(glr_ref[...], w2_ref, b_ref)
    row = lax.broadcasted_iota(jnp.int32, (chunk, chunk), 0)
    col = lax.broadcasted_iota(jnp.int32, (chunk, chunk), 1)
    causal = row >= col
    tri = jnp.where(causal, 1.0, 0.0).astype(BF16)
    hi, mid, lo = _split_bf16(log_a)
    cum_all = _dot(tri, hi) + _dot(tri, mid) + _dot(tri, lo)
    steep = jnp.max(-cum_all[chunk - 1:chunk, :]) > GLA_MAX_CHUNK_DECAY

    @pl.when(jnp.logical_not(steep))
    def _():
        for h in range(n_heads):
            cum = cum_all[:, h * dk:(h + 1) * dk]
            last = cum[chunk - 1:chunk, :]
            q = q_ref[:, h * dk:(h + 1) * dk] * scale
            k = k_ref[:, h * dk:(h + 1) * dk]
            v = v_ref[:, h * dv:(h + 1) * dv]
            st = st_ref[h]
            qt = (q * jnp.exp(cum)).astype(BF16)
            kt = (k * jnp.exp(-cum)).astype(BF16)
            att = jnp.where(causal, _dot_nt(qt, kt), 0.0)
            of_ref[:, h * dv:(h + 1) * dv] = (_dot(att.astype(BF16), v.astype(BF16))
                                              + _dot_nt(qt, st.astype(BF16)))
            kd = (k * jnp.exp(last - cum)).astype(BF16)
            st_ref[h] = st * jnp.exp(last) + _dot(v.T.astype(BF16), kd)

    @pl.when(steep)
    def _():
        la_ref[...] = log_a
        eye = (lax.broadcasted_iota(jnp.int32, (dv, dv), 0)
               == lax.broadcasted_iota(jnp.int32, (dv, dv), 1))

        sub = lax.broadcasted_iota(jnp.int32, (8, dv), 0)

        def tokens(g, carry):
            r0 = pl.multiple_of(g * 8, 8)
            a8 = jnp.exp(la_ref[pl.ds(r0, 8), :])
            q8 = q_ref[pl.ds(r0, 8), :] * scale
            k8 = k_ref[pl.ds(r0, 8), :]
            v8 = v_ref[pl.ds(r0, 8), :]
            for h in range(n_heads):
                st = st_ref[h]
                o8 = jnp.zeros((8, dv), F32)
                for r in range(8):
                    v_t = v8[r:r + 1, h * dv:(h + 1) * dv]
                    v_col = jnp.sum(jnp.where(eye, jnp.broadcast_to(v_t, (dv, dv)), 0.0),
                                    axis=-1, keepdims=True)
                    st = st * a8[r:r + 1, h * dk:(h + 1) * dk] + v_col * k8[r:r + 1, h * dk:(h + 1) * dk]
                    o_col = jnp.sum(st * q8[r:r + 1, h * dk:(h + 1) * dk], axis=-1, keepdims=True)
                    o_row = jnp.sum(jnp.where(eye, jnp.broadcast_to(o_col, (dv, dv)), 0.0),
                                    axis=0, keepdims=True)
                    o8 = jnp.where(sub == r, o_row, o8)
                st_ref[h] = st
                of_ref[pl.ds(r0, 8), h * dv:(h + 1) * dv] = o8
            return carry

        lax.fori_loop(0, chunk // 8, tokens, 0)

    for h in range(n_heads):
        o_ref[:, h * dv:(h + 1) * dv] = _gla_out(
            of_ref[:, h * dv:(h + 1) * dv], gn_ref, gr_ref[:, h * dv:(h + 1) * dv]).astype(o_ref.dtype)

    @pl.when(c == pl.num_programs(1) - 1)
    def _():
        for h in range(n_heads):
            s_ref[0, h] = st_ref[h].T


def _gla_prompt(z, glr, w2p, bias, gla_norm, *, batch, seq, n_heads, dk, dv, col_q, col_k, col_v,
                col_gr, chunk=256):
    assert seq % chunk == 0
    nc = seq // chunk
    wk, wv = n_heads * dk, n_heads * dv
    assert col_q % wk == 0 and col_k % wk == 0 and col_v % wv == 0 and col_gr % wv == 0
    bq, bk, bv, bg = col_q // wk, col_k // wk, col_v // wv, col_gr // wv
    est = (2 * chunk * (2 * wk + 2 * wv + LANE) * 4 + 2 * chunk * wv * 2 + 3 * n_heads * dk * dv * 4
           + chunk * (wk + wv) * 4 + 4 * chunk * wk * 4 + 8 * chunk * chunk * 4 + 12 * chunk * dv * 4)
    return pl.pallas_call(
        functools.partial(_gla_prompt_kernel, chunk=chunk, n_heads=n_heads, dk=dk, dv=dv),
        name="gla_prompt",
        grid=(batch, nc),
        in_specs=[
            pl.BlockSpec((chunk, wk), lambda b, c: (b * nc + c, bq)),
            pl.BlockSpec((chunk, wk), lambda b, c: (b * nc + c, bk)),
            pl.BlockSpec((chunk, wv), lambda b, c: (b * nc + c, bv)),
            pl.BlockSpec((chunk, wv), lambda b, c: (b * nc + c, bg)),
            pl.BlockSpec((chunk, LANE), lambda b, c: (b * nc + c, 0)),
            pl.BlockSpec((LANE, wk), lambda b, c: (0, 0)),
            pl.BlockSpec((1, wk), lambda b, c: (0, 0)),
            pl.BlockSpec((1, dv), lambda b, c: (0, 0)),
        ],
        out_specs=[
            pl.BlockSpec((chunk, wv), lambda b, c: (b * nc + c, 0)),
            pl.BlockSpec((1, n_heads, dk, dv), lambda b, c: (b, 0, 0, 0)),
        ],
        out_shape=[jax.ShapeDtypeStruct((batch * seq, wv), BF16),
                   jax.ShapeDtypeStruct((batch, n_heads, dk, dv), F32)],
        scratch_shapes=[pltpu.VMEM((n_heads, dv, dk), F32),
                        pltpu.VMEM((chunk, wk), F32),
                        pltpu.VMEM((chunk, wv), F32)],
        compiler_params=pltpu.CompilerParams(
            dimension_semantics=("parallel", "arbitrary"),
            vmem_limit_bytes=_vmem_limit(est)),
    )(z, z, z, z, glr, w2p, bias, gla_norm)


def _gla_sample_kernel(z_ref, glr_ref, s0_ref, w2_ref, b_ref, gn_ref, o_ref, s_ref,
                       *, n_seq, n_heads, dk, dv, dec, col_q, col_k, col_v, col_gr):
    for s in range(n_seq):
        _gla_sample_one(z_ref.at[s], glr_ref.at[s], s0_ref.at[s], w2_ref, b_ref, gn_ref,
                        o_ref.at[s], s_ref.at[s], n_heads=n_heads, dk=dk, dv=dv, dec=dec,
                        col_q=col_q, col_k=col_k, col_v=col_v, col_gr=col_gr)


def _gla_sample_one(z_ref, glr_ref, s0_ref, w2_ref, b_ref, gn_ref, o_ref, s_ref,
                    *, n_heads, dk, dv, dec, col_q, col_k, col_v, col_gr):
    log_a_all = _log_decay(glr_ref[...], w2_ref, b_ref)
    t_k = lax.broadcasted_iota(jnp.int32, (dec, dk), 0)
    t_v = lax.broadcasted_iota(jnp.int32, (dec, dv), 0)
    eye = (lax.broadcasted_iota(jnp.int32, (dk, dk), 0)
           == lax.broadcasted_iota(jnp.int32, (dk, dk), 1))

    def to_col(r):
        return jnp.sum(jnp.where(eye, jnp.broadcast_to(r, (dk, dk)), 0.0), axis=-1, keepdims=True)

    for h in range(n_heads):
        log_a = log_a_all[:, h * dk:(h + 1) * dk]
        cum = jnp.zeros((dec, dk), F32)
        for t in range(dec):
            cum = cum + jnp.where(t_k >= t, log_a[t:t + 1], 0.0)
        last = cum[dec - 1:dec]
        q = z_ref[:, col_q + h * dk: col_q + (h + 1) * dk] * (dk ** -0.5)
        k = z_ref[:, col_k + h * dk: col_k + (h + 1) * dk]
        v = z_ref[:, col_v + h * dv: col_v + (h + 1) * dv]
        gr = z_ref[:, col_gr + h * dv: col_gr + (h + 1) * dv]
        s0 = s0_ref[h]

        o = _dot((q * jnp.exp(cum)).astype(BF16), s0.astype(BF16))
        for t in range(dec):
            o_t = jnp.zeros((1, dv), F32)
            for j in range(t + 1):
                w = jnp.exp(cum[t:t + 1] - cum[j:j + 1])
                a = jnp.sum(q[t:t + 1] * k[j:j + 1] * w, axis=-1, keepdims=True)
                o_t = o_t + a * v[j:j + 1]
            o = o + jnp.where(t_v == t, o_t, 0.0)

        kd = (k * jnp.exp(last - cum)).astype(BF16)
        upd = lax.dot_general(kd, v.astype(BF16), (((0,), (0,)), ((), ())),
                              preferred_element_type=F32)
        s_ref[h] = s0 * to_col(jnp.exp(last)) + upd
        o_ref[:, h * dv:(h + 1) * dv] = _gla_out(o, gn_ref, gr).astype(o_ref.dtype)


def _gla_sample(z3, glr3, state, w2p, bias, gla_norm, *, n_heads, dk, dv, col_q, col_k, col_v,
                col_gr, n_seq=4):
    dec_b, dec, zw = z3.shape
    assert dec_b % n_seq == 0
    est = n_seq * (2 * dec * zw * 4 + 2 * 2 * n_heads * dk * dv * 4) + LANE * n_heads * dk * 2 \
        + 8 * dk * dv * 4
    return pl.pallas_call(
        functools.partial(_gla_sample_kernel, n_seq=n_seq, n_heads=n_heads, dk=dk, dv=dv, dec=dec,
                          col_q=col_q, col_k=col_k, col_v=col_v, col_gr=col_gr),
        name="gla_sample",
        grid=(dec_b // n_seq,),
        in_specs=[
            pl.BlockSpec((n_seq, dec, zw), lambda b: (b, 0, 0)),
            pl.BlockSpec((n_seq, dec, LANE), lambda b: (b, 0, 0)),
            pl.BlockSpec((n_seq, n_heads, dk, dv), lambda b: (b, 0, 0, 0)),
            pl.BlockSpec((LANE, n_heads * dk), lambda b: (0, 0)),
            pl.BlockSpec((1, n_heads * dk), lambda b: (0, 0)),
            pl.BlockSpec((1, dv), lambda b: (0, 0)),
        ],
        out_specs=[
            pl.BlockSpec((n_seq, dec, n_heads * dv), lambda b: (b, 0, 0)),
            pl.BlockSpec((n_seq, n_heads, dk, dv), lambda b: (b, 0, 0, 0)),
        ],
        out_shape=[jax.ShapeDtypeStruct((dec_b, dec, n_heads * dv), BF16),
                   jax.ShapeDtypeStruct((dec_b, n_heads, dk, dv), F32)],
        compiler_params=pltpu.CompilerParams(
            dimension_semantics=("parallel",),
            vmem_limit_bytes=_vmem_limit(est)),
    )(z3, glr3, state, w2p, bias, gla_norm)


def _round_up(x, m):
    return (x + m - 1) // m * m


def kernel(x_prompt, x_sample, cache_k, cache_v, state_gla, page_table, norm_ffn1, ffn1_w_gate, ffn1_w_up, ffn1_w_down, norm_mix, w_in, gla_gate_w2, gla_gate_b, lambda_q1, lambda_k1, lambda_q2, lambda_k2, diff_norm, gla_norm, w_out, norm_ffn2, ffn2_w_gate, ffn2_w_up, ffn2_w_down, norm_final):
    batch, seq, d = x_prompt.shape
    dec_b, dec, _ = x_sample.shape
    depth = norm_ffn1.shape[0]
    n_heads, hd = cache_k.shape[3], cache_k.shape[5]
    vd = cache_v.shape[4]
    g_heads, dk, dv = state_gla.shape[2:]
    rank = gla_gate_w2.shape[1]
    d_ff = ffn1_w_gate.shape[2]

    w_qk = n_heads * 2 * hd
    sizes = (w_qk, w_qk, n_heads * vd, g_heads * dk, g_heads * dk, g_heads * dv, g_heads * dv, rank)
    cols = [0]
    for s in sizes:
        cols.append(cols[-1] + s)
    col_dq, col_dk, col_dv, col_gq, col_gk, col_gv, col_gr, col_glr = cols[:8]
    assert w_in.shape[2] == cols[8] and rank <= LANE
    ff_pad = _round_up(d_ff, 512)
    diff_w = n_heads * vd

    yp = x_prompt.reshape(batch * seq, d)
    ys = x_sample.reshape(dec_b * dec, d)
    nf = norm_final.reshape(1, d)
    outs = [[] for _ in range(6)]

    for layer in range(depth):
        lam_init = _lambda_init(layer)
        ffn_w = []
        for wg, wu, wd in ((ffn1_w_gate, ffn1_w_up, ffn1_w_down), (ffn2_w_gate, ffn2_w_up, ffn2_w_down)):
            ffn_w.append((_to_bf16(wg, layer, d, ff_pad), _to_bf16(wu, layer, d, ff_pad),
                          _to_bf16(wd, layer, ff_pad, d)))
        w_main, w_glr = _w_in_prep(w_in, layer, col_glr)
        w2p = jnp.pad(gla_gate_w2[layer], ((0, LANE - rank), (0, 0))).astype(BF16)
        gate_b = gla_gate_b[layer].reshape(1, -1)
        lam_p = jnp.stack([lambda_q1[layer], lambda_k1[layer], lambda_q2[layer], lambda_k2[layer]])
        dn = diff_norm[layer].reshape(1, vd)
        gn = gla_norm[layer].reshape(1, dv)
        wo = _to_bf16(w_out, layer, w_out.shape[1], d)
        g1 = norm_ffn1[layer].reshape(1, d)
        gm = norm_mix[layer].reshape(1, d)
        g2 = norm_ffn2[layer].reshape(1, d)
        last = layer == depth - 1

        def pre(x):
            x, hmix = _ffn(x, g1, *ffn_w[0], gm, tail="next")
            return (x,) + tuple(_inproj(hmix, w_main, w_glr, n_heads=n_heads, hd=hd, col_k=col_dk,
                                        col_v=col_dv))

        def post(x, mix_a, mix_b):
            x = _outproj(x, mix_a, mix_b, wo)
            return _ffn(x, g2, *ffn_w[1], nf, tail="final" if last else "none")

        def kv_out(k_rows, v_rows, b, l):
            k = k_rows.reshape(b, l, n_heads, 2, hd)
            v = v_rows.reshape(b, l, 2, n_heads, hd).transpose(0, 1, 3, 2, 4).reshape(b, l, n_heads, vd)
            return k, v

        yp, zp, glr_p, k_rows, v_rows, kv16_p = pre(yp)
        ys, zs, glr_s, k_rows_s, v_rows_s, _ = pre(ys)
        zs3 = zs.reshape(dec_b, dec, col_glr)

        mix_a, mix_a_s = _attention(zp, kv16_p, zs3, cache_k, cache_v, page_table, layer, lam_p, dn,
                                    batch=batch, seq=seq, n_heads=n_heads, hd=hd, vd=vd, col_q=col_dq,
                                    col_k=col_dk, col_v=col_dv, lam_init=lam_init)

        mix_b, s_p = _gla_prompt(zp, glr_p, w2p, gate_b, gn, batch=batch, seq=seq, n_heads=g_heads,
                                 dk=dk, dv=dv, col_q=col_gq, col_k=col_gk, col_v=col_gv, col_gr=col_gr)
        yp = post(yp, mix_a, mix_b)
        k_new, v_new = kv_out(k_rows, v_rows, batch, seq)
        outs[0].append(k_new)
        outs[1].append(v_new)
        outs[2].append(s_p.astype(state_gla.dtype))

        mix_b, s_s = _gla_sample(zs3, glr_s.reshape(dec_b, dec, LANE), state_gla[layer], w2p, gate_b, gn,
                                 n_heads=g_heads, dk=dk, dv=dv, col_q=col_gq, col_k=col_gk,
                                 col_v=col_gv, col_gr=col_gr)
        ys = post(ys, mix_a_s.reshape(dec_b * dec, -1), mix_b.reshape(dec_b * dec, -1))
        k_new, v_new = kv_out(k_rows_s, v_rows_s, dec_b, dec)
        outs[3].append(k_new)
        outs[4].append(v_new)
        outs[5].append(s_s.astype(state_gla.dtype))

    if depth == 0:
        raise ValueError("depth must be positive")
    y_prompt = yp.reshape(batch, seq, d)
    y_sample = ys.reshape(dec_b, dec, d)
    return (y_prompt, y_sample) + tuple(jnp.stack(o) for o in outs)
```

```python
import functools
import math

import jax
import jax.numpy as jnp
from jax import lax
from jax.experimental import pallas as pl
from jax.experimental.pallas import tpu as pltpu

F32 = jnp.float32
BF16 = jnp.bfloat16

NORM_EPS = 1e-6
GLA_TAU = 16.0
LOG2_E = math.log2(math.e)
LANE = 128
VMEM_PHYSICAL = 64 * 2**20
VMEM_RESERVED = 8 * 2**20
VMEM_TEMPORARIES = 4 * 2**20
FFN_TF = 512


def _lambda_init(layer):
    return 0.8 - 0.6 * math.exp(-0.3 * layer)


def _vmem_limit(nbytes):
    return int(min(nbytes * 5 // 4 + VMEM_TEMPORARIES, VMEM_PHYSICAL - VMEM_RESERVED))


def _rms(x, gain):
    return x * lax.rsqrt(jnp.mean(x * x, axis=-1, keepdims=True) + NORM_EPS) * gain


def _dot(a, b):
    return jnp.dot(a, b, preferred_element_type=F32)


def _dot_nt(a, b):
    return lax.dot_general(a, b, (((1,), (1,)), ((), ())), preferred_element_type=F32)


def _to_bf16_kernel(w_ref, o_ref, *, rows, cols, tr):
    r_out, c_out = o_ref.shape
    c_copy = min(cols, c_out)
    w = w_ref[0][:, :c_copy]
    if rows % tr:
        r_idx = pl.program_id(0) * tr + lax.broadcasted_iota(jnp.int32, (r_out, c_copy), 0)
        w = jnp.where(r_idx < rows, w, 0.0)
    o_ref[:, :c_copy] = w.astype(BF16)
    if c_copy < c_out:
        o_ref[:, c_copy:] = jnp.zeros((r_out, c_out - c_copy), BF16)


def _to_bf16(w, layer, rows_out, cols_out, *, tr=512):
    _, rows, cols = w.shape
    assert rows_out % tr == 0 and rows_out >= rows and min(cols, cols_out) % LANE == 0
    est = 2 * tr * cols * 4 + 2 * tr * cols_out * 2 + tr * cols * 4
    return pl.pallas_call(
        functools.partial(_to_bf16_kernel, rows=rows, cols=cols, tr=tr),
        name="to_bf16",
        grid=(rows_out // tr,),
        in_specs=[pl.BlockSpec((1, tr, cols), lambda i: (layer, i, 0))],
        out_specs=pl.BlockSpec((tr, cols_out), lambda i: (i, 0)),
        out_shape=jax.ShapeDtypeStruct((rows_out, cols_out), BF16),
        compiler_params=pltpu.CompilerParams(
            dimension_semantics=("parallel",),
            vmem_limit_bytes=_vmem_limit(est)),
    )(w)


def _w_in_prep_kernel(wt_ref, wm_ref, wl_ref, *, n_main_blocks, rank, tr):
    j = pl.program_id(0)

    @pl.when(j < n_main_blocks)
    def _():
        wm_ref[...] = wt_ref[0].T.astype(BF16)

    @pl.when(j == n_main_blocks)
    def _():
        tail = wt_ref[0][:LANE]
        r_idx = lax.broadcasted_iota(jnp.int32, tail.shape, 0)
        wl_ref[...] = jnp.where(r_idx < rank, tail, 0.0).T.astype(BF16)


def _w_in_prep(w_in, layer, n_main, *, tr=512):
    wt = jnp.swapaxes(w_in, 1, 2)
    _, n, d = wt.shape
    rank = n - n_main
    nb = n_main // tr
    assert n_main % tr == 0 and 0 < rank <= LANE <= tr
    est = 2 * tr * d * 4 + 2 * d * (tr + LANE) * 2 + 2 * tr * d * 4
    return pl.pallas_call(
        functools.partial(_w_in_prep_kernel, n_main_blocks=nb, rank=rank, tr=tr),
        name="w_in_prep",
        grid=(nb + 1,),
        in_specs=[pl.BlockSpec((1, tr, d), lambda j: (layer, j, 0))],
        out_specs=[pl.BlockSpec((d, tr), lambda j: (0, jnp.minimum(j, nb - 1))),
                   pl.BlockSpec((d, LANE), lambda j: (0, 0))],
        out_shape=[jax.ShapeDtypeStruct((d, n_main), BF16), jax.ShapeDtypeStruct((d, LANE), BF16)],
        compiler_params=pltpu.CompilerParams(
            dimension_semantics=("arbitrary",),
            vmem_limit_bytes=_vmem_limit(est)),
    )(wt)


def _ffn_kernel(x_ref, g_ref, wg_ref, wu_ref, wd_ref, gt_ref, *rest, tail):
    o_ref = rest[0]
    h_ref, acc_ref = rest[-2:]
    j = pl.program_id(1)

    @pl.when(j == 0)
    def _():
        h_ref[...] = _rms(x_ref[...], g_ref[...]).astype(BF16)
        acc_ref[...] = jnp.zeros_like(acc_ref)

    h = h_ref[...]
    gate = _dot(h, wg_ref[...])
    up = _dot(h, wu_ref[...])
    act = (gate * jax.nn.sigmoid(gate)) * up
    acc_ref[...] += _dot(act.astype(BF16), wd_ref[...])

    @pl.when(j == pl.num_programs(1) - 1)
    def _():
        y = x_ref[...] + 0.5 * acc_ref[...]
        if tail == "final":
            y = _rms(y, gt_ref[...])
        o_ref[...] = y
        if tail == "next":
            rest[1][...] = _rms(y, gt_ref[...]).astype(BF16)


def _ffn(x, gain, wg, wu, wd, gain_tail, *, tail, tm=512, tf=FFN_TF):
    m, d = x.shape
    fpad = wg.shape[1]
    assert m % tm == 0 and fpad % tf == 0 and tail in ("final", "next", "none")
    est = (2 * 2 * tm * d * 4
           + 2 * tm * d * 2
           + tm * d * (2 + 4)
           + 2 * 3 * d * tf * 2
           + 3 * tm * tf * 4)
    row_spec = pl.BlockSpec((tm, d), lambda i, j: (i, 0))
    out_specs, out_shape = row_spec, jax.ShapeDtypeStruct((m, d), F32)
    if tail == "next":
        out_specs, out_shape = [row_spec, row_spec], [out_shape, jax.ShapeDtypeStruct((m, d), BF16)]
    return pl.pallas_call(
        functools.partial(_ffn_kernel, tail=tail),
        name="ffn_" + tail,
        grid=(m // tm, fpad // tf),
        in_specs=[
            row_spec,
            pl.BlockSpec((1, d), lambda i, j: (0, 0)),
            pl.BlockSpec((d, tf), lambda i, j: (0, j)),
            pl.BlockSpec((d, tf), lambda i, j: (0, j)),
            pl.BlockSpec((tf, d), lambda i, j: (j, 0)),
            pl.BlockSpec((1, d), lambda i, j: (0, 0)),
        ],
        out_specs=out_specs,
        out_shape=out_shape,
        scratch_shapes=[pltpu.VMEM((tm, d), BF16), pltpu.VMEM((tm, d), F32)],
        compiler_params=pltpu.CompilerParams(
            dimension_semantics=("parallel", "arbitrary"),
            vmem_limit_bytes=_vmem_limit(est)),
    )(x, gain, wg, wu, wd, gain_tail)


def _inproj_kernel(h_ref, w_ref, wl_ref, z_ref, glr_ref, kr_ref, vr_ref, kv16_ref,
                   *, tm, n_heads, hd, k_tile, v_tile):
    j = pl.program_id(1)
    pitch = 2 * n_heads
    z_ref[...] = _dot(h_ref[...], w_ref[...])

    @pl.when(j == 0)
    def _():
        glr_ref[...] = _dot(h_ref[...], wl_ref[...])

    @pl.when(j == k_tile)
    def _():
        kv16_ref[...] = z_ref[...].astype(BF16)
        for c in range(pitch):
            kr_ref[pl.ds(c, tm, stride=pitch), :] = z_ref[:, c * hd:(c + 1) * hd]

    @pl.when(j == v_tile)
    def _():
        kv16_ref[...] = z_ref[...].astype(BF16)
        for h in range(n_heads):
            for half in range(2):
                c = 2 * h + half
                vr_ref[pl.ds(n_heads * half + h, tm, stride=pitch), :] = z_ref[:, c * hd:(c + 1) * hd]


def _inproj(h, w, wl, *, n_heads, hd, col_k, col_v, tm=1024):
    m, d = h.shape
    n = w.shape[1]
    tn = n_heads * 2 * hd
    tm = min(tm, m)
    pitch = 2 * n_heads
    assert m % tm == 0 and n % tn == 0 and col_k % tn == 0 and col_v == col_k + tn and hd == LANE
    k_tile = col_k // tn
    est = (2 * tm * d * 2 + 2 * d * (tn + LANE) * 2 + 2 * tm * (tn + LANE) * 4
           + 2 * 2 * tm * pitch * hd * 4 + tm * tn * 4 + 2 * tm * tn * 2)
    return pl.pallas_call(
        functools.partial(_inproj_kernel, tm=tm, n_heads=n_heads, hd=hd, k_tile=k_tile,
                          v_tile=k_tile + 1),
        name="in_proj",
        grid=(m // tm, n // tn),
        in_specs=[
            pl.BlockSpec((tm, d), lambda i, j: (i, 0)),
            pl.BlockSpec((d, tn), lambda i, j: (0, j)),
            pl.BlockSpec((d, LANE), lambda i, j: (0, 0)),
        ],
        out_specs=[
            pl.BlockSpec((tm, tn), lambda i, j: (i, j)),
            pl.BlockSpec((tm, LANE), lambda i, j: (i, 0)),
            pl.BlockSpec((tm * pitch, hd), lambda i, j: (i, 0)),
            pl.BlockSpec((tm * pitch, hd), lambda i, j: (i, 0)),
            pl.BlockSpec((tm, tn), lambda i, j: (i, jnp.clip(j - k_tile, 0, 1))),
        ],
        out_shape=[jax.ShapeDtypeStruct((m, n), F32),
                   jax.ShapeDtypeStruct((m, LANE), F32),
                   jax.ShapeDtypeStruct((m * pitch, hd), F32),
                   jax.ShapeDtypeStruct((m * pitch, hd), F32),
                   jax.ShapeDtypeStruct((m, 2 * tn), BF16)],
        compiler_params=pltpu.CompilerParams(
            dimension_semantics=("parallel", "arbitrary"),
            vmem_limit_bytes=_vmem_limit(est)),
    )(h, w, wl)


def _outproj_kernel(x_ref, a_ref, b_ref, wa_ref, wb_ref, o_ref):
    o_ref[...] = x_ref[...] + _dot(a_ref[...], wa_ref[...]) + _dot(b_ref[...], wb_ref[...])


def _outproj(x, mix_a, mix_b, w, *, tm=512):
    m, d = x.shape
    ka, kb = mix_a.shape[1], mix_b.shape[1]
    assert m % tm == 0 and ka == kb and w.shape == (ka + kb, d)
    est = 2 * 2 * tm * d * 4 + 2 * tm * (ka + kb) * 2 + 2 * (ka + kb) * d * 2 + tm * d * 4
    return pl.pallas_call(
        _outproj_kernel,
        name="out_proj",
        grid=(m // tm,),
        in_specs=[
            pl.BlockSpec((tm, d), lambda i: (i, 0)),
            pl.BlockSpec((tm, ka), lambda i: (i, 0)),
            pl.BlockSpec((tm, kb), lambda i: (i, 0)),
            pl.BlockSpec((ka, d), lambda i: (0, 0)),
            pl.BlockSpec((kb, d), lambda i: (1, 0)),
        ],
        out_specs=pl.BlockSpec((tm, d), lambda i: (i, 0)),
        out_shape=jax.ShapeDtypeStruct((m, d), F32),
        compiler_params=pltpu.CompilerParams(
            dimension_semantics=("parallel",),
            vmem_limit_bytes=_vmem_limit(est)),
    )(x, mix_a, mix_b, w, w)


def _diff_lambda(lam_ref, lam_init):
    lp = lam_ref[...]
    e1 = jnp.exp(jnp.sum(lp[0:1] * lp[1:2], axis=-1, keepdims=True))
    e2 = jnp.exp(jnp.sum(lp[2:3] * lp[3:4], axis=-1, keepdims=True))
    return e1 - e2 + lam_init


def _alibi_slope(h, n_heads):
    slope = F32(0.0)
    for i in range(n_heads):
        slope = jnp.where(h == i, F32(2.0 ** (-8.0 * (i + 1) / n_heads)), slope)
    return slope


def _lane_fold(x, op):
    parts = [x[:, i * LANE:(i + 1) * LANE] for i in range(x.shape[1] // LANE)]
    return functools.reduce(op, parts)


def _attn_prompt_kernel(q_ref, k_ref, v_ref, lam_ref, dn_ref, o_ref, s_ref, m_ref, l_ref, acc_ref,
                        *, tq, tk, hd, n_heads, lam_init):
    h = pl.program_id(1)
    qi = pl.program_id(2)
    scale = hd ** -0.5 * LOG2_E
    slope = _alibi_slope(h, n_heads) * LOG2_E
    n_diag = tq // tk
    n_full = qi * n_diag
    q = q_ref[...]
    zero = jnp.zeros((tq, hd), F32)
    qbd = jnp.concatenate([jnp.concatenate([q[:, :hd], zero], axis=1),
                           jnp.concatenate([zero, q[:, hd:]], axis=1)], axis=0).astype(BF16)
    row = lax.broadcasted_iota(jnp.int32, (tq, tk), 0)
    col = lax.broadcasted_iota(jnp.int32, (tq, tk), 1)
    d0 = (row - col).astype(F32)

    m_ref[...] = jnp.full_like(m_ref, -jnp.inf)

    def scores(c, masked):
        ks = pl.multiple_of(c * tk, tk)
        kc = k_ref[pl.ds(ks, tk), :]
        off = jnp.full((1, 1), qi * tq - c * tk, jnp.int32).astype(F32)
        dist = d0 + off
        bias = slope * dist
        s = _dot_nt(qbd, kc) * scale - jnp.concatenate([bias, bias], axis=0)
        if masked:
            s = jnp.where(jnp.concatenate([dist, dist], axis=0) >= 0, s, -jnp.inf)
        s_ref[c] = s
        m_ref[...] = jnp.maximum(m_ref[...], _lane_fold(s, jnp.maximum))

    def scores_body(c, carry):
        scores(c, False)
        return carry

    lax.fori_loop(0, n_full, scores_body, 0)
    for j in range(n_diag):
        scores(n_full + j, True)

    m_row = jnp.max(m_ref[...], axis=-1, keepdims=True)
    m_ref[...] = jnp.broadcast_to(m_row, m_ref.shape)
    l_ref[...] = jnp.zeros_like(l_ref)
    acc_ref[...] = jnp.zeros_like(acc_ref)

    def pv_body(c, carry):
        ks = pl.multiple_of(c * tk, tk)
        mb = m_ref[...]
        p = jnp.exp2(s_ref[c] - jnp.concatenate([mb] * (tk // LANE), axis=1))
        l_ref[...] += _lane_fold(p, jnp.add)
        acc_ref[...] += _dot(p.astype(BF16), v_ref[pl.ds(ks, tk), :])
        return carry

    lax.fori_loop(0, n_full + n_diag, pv_body, 0)

    lam = _diff_lambda(lam_ref, lam_init)
    on = acc_ref[...] / jnp.sum(l_ref[...], axis=-1, keepdims=True)
    o = on[:tq] - lam * on[tq:]
    o_ref[...] = (_rms(o, dn_ref[...]) * (1.0 - lam_init)).astype(o_ref.dtype)


def _attn_both_kernel(*refs, n_pages, prompt, sample):
    pt_ref = refs[0]
    q_ref, k_ref, v_ref, lam_ref, dn_ref, z_ref = refs[1:7]
    pages = refs[7:7 + 2 * n_pages]
    op_ref, os_ref = refs[7 + 2 * n_pages:9 + 2 * n_pages]
    s_ref, m_ref, l_ref, acc_ref, q2_ref, kn_ref, vn_ref = refs[9 + 2 * n_pages:]
    _attn_sample_kernel(pt_ref, z_ref, *pages, lam_ref, dn_ref, os_ref, q2_ref, kn_ref, vn_ref, **sample)
    _attn_prompt_kernel(q_ref, k_ref, v_ref, lam_ref, dn_ref, op_ref, s_ref, m_ref, l_ref, acc_ref,
                        **prompt)


def _attention(z, kv16, z3, cache_k, cache_v, page_table, layer, lam_p, diff_norm, *, batch, seq,
               n_heads, hd, vd, col_q, col_k, col_v, lam_init, tq=512, tk=512):
    dec_b, dec, zw = z3.shape
    n_pages = page_table.shape[1]
    depth, n_pool, page = cache_k.shape[:3]
    nq = seq // tq
    cw = 2 * hd
    width = n_heads * cw
    prow = page * n_heads * 2
    rows = 2 * dec
    assert vd == 2 * hd and seq % tq == 0 and tq % tk == 0 and tk % LANE == 0 and col_q % cw == 0
    assert dec_b == batch * n_heads * nq
    kf = cache_k.reshape(depth * n_pool, prow, hd)
    vf = cache_v.reshape(depth * n_pool, page, n_heads, 2, hd).transpose(0, 1, 3, 2, 4).reshape(
        depth * n_pool, prow, hd)
    pt = (page_table + layer * n_pool).reshape(-1).astype(jnp.int32)

    def sample_idx(b, h, i):
        return (b * n_heads + h) * nq + i

    def page_spec(p):
        return pl.BlockSpec((1, prow, hd),
                            lambda b, h, i, pt_ref: (pt_ref[sample_idx(b, h, i) * n_pages + p], 0, 0))

    est = (2 * tq * cw * 4 + 2 * 2 * seq * cw * 2 + 2 * tq * vd * 2
           + (seq // tk) * 2 * tq * tk * 4 + 2 * 2 * tq * LANE * 4 + 2 * tq * vd * 4
           + 3 * 2 * tq * tk * 4
           + 2 * dec * zw * 4 + 2 * 2 * n_pages * page * width * 4 + 2 * dec * width * 2
           + rows * 2 * hd * 4 + 2 * page * width * 4)
    grid_spec = pltpu.PrefetchScalarGridSpec(
        num_scalar_prefetch=1,
        grid=(batch, n_heads, nq),
        in_specs=([pl.BlockSpec((tq, cw), lambda b, h, i, pt_ref: (b * nq + i, col_q // cw + h)),
                   pl.BlockSpec((seq, cw), lambda b, h, i, pt_ref: (b, h)),
                   pl.BlockSpec((seq, cw), lambda b, h, i, pt_ref: (b, n_heads + h)),
                   pl.BlockSpec((4, hd), lambda b, h, i, pt_ref: (0, 0)),
                   pl.BlockSpec((1, vd), lambda b, h, i, pt_ref: (0, 0)),
                   pl.BlockSpec((1, dec, zw), lambda b, h, i, pt_ref: (sample_idx(b, h, i), 0, 0))]
                  + [page_spec(p) for p in range(n_pages)]
                  + [page_spec(p) for p in range(n_pages)]),
        out_specs=[pl.BlockSpec((tq, vd), lambda b, h, i, pt_ref: (b * nq + i, h)),
                   pl.BlockSpec((1, dec, width), lambda b, h, i, pt_ref: (sample_idx(b, h, i), 0, 0))],
        scratch_shapes=[pltpu.VMEM((seq // tk, 2 * tq, tk), F32),
                        pltpu.VMEM((2 * tq, LANE), F32), pltpu.VMEM((2 * tq, LANE), F32),
                        pltpu.VMEM((2 * tq, vd), F32),
                        pltpu.VMEM((rows, 2 * hd), F32),
                        pltpu.VMEM((page, width), F32),
                        pltpu.VMEM((page, width), F32)],
    )
    return pl.pallas_call(
        functools.partial(
            _attn_both_kernel, n_pages=n_pages,
            prompt=dict(tq=tq, tk=tk, hd=hd, n_heads=n_heads, lam_init=lam_init),
            sample=dict(n_pages=n_pages, page=page, n_heads=n_heads, hd=hd, dec=dec, col_q=col_q,
                        col_k=col_k, col_v=col_v, lam_init=lam_init)),
        name="attention",
        grid_spec=grid_spec,
        out_shape=[jax.ShapeDtypeStruct((batch * seq, width), BF16),
                   jax.ShapeDtypeStruct((dec_b, dec, width), BF16)],
        compiler_params=pltpu.CompilerParams(
            dimension_semantics=("parallel", "parallel", "arbitrary"),
            vmem_limit_bytes=_vmem_limit(est)),
    )(pt, z, kv16, kv16, lam_p, diff_norm, z3, *([kf] * n_pages), *([vf] * n_pages))


def _attn_sample_kernel(*refs, n_pages, page, n_heads, hd, dec, col_q, col_k, col_v, lam_init):
    z_ref = refs[1]
    kp_refs = refs[2:2 + n_pages]
    vp_refs = refs[2 + n_pages:2 + 2 * n_pages]
    pitch = 2 * n_heads
    lam_ref, dn_ref, o_ref, q2_ref, kn_ref, vn_ref = refs[2 + 2 * n_pages:]
    cw = 2 * hd
    rows = 2 * dec
    scale = hd ** -0.5 * LOG2_E
    past = n_pages * page
    n_keys = past + page
    r_iota = lax.broadcasted_iota(jnp.int32, (rows, n_keys), 0)
    j_iota = lax.broadcasted_iota(jnp.int32, (rows, n_keys), 1)
    q_idx = jnp.where(r_iota >= dec, r_iota - dec, r_iota)
    dist = (past + q_idx - j_iota).astype(F32)
    valid = j_iota <= past + q_idx
    rr = lax.broadcasted_iota(jnp.int32, (rows, cw), 0)
    cc = lax.broadcasted_iota(jnp.int32, (rows, cw), 1)
    keep = (rr < dec) == (cc < hd)
    lam = _diff_lambda(lam_ref, lam_init)

    width = n_heads * cw
    kn_ref[...] = jnp.zeros_like(kn_ref)
    vn_ref[...] = jnp.zeros_like(vn_ref)
    kn_ref[0:dec, :] = z_ref[0, :, col_k:col_k + width]
    vn_ref[0:dec, :] = z_ref[0, :, col_v:col_v + width]

    for h in range(n_heads):
        slope = 2.0 ** (-8.0 * (h + 1) / n_heads) * LOG2_E
        qh = z_ref[0, :, col_q + h * cw: col_q + (h + 1) * cw]
        q2_ref[0:dec, :] = qh
        q2_ref[dec:rows, :] = qh
        qbd = jnp.where(keep, q2_ref[...], 0.0).astype(BF16)
        parts = []
        for kp in kp_refs:
            kcat = jnp.concatenate([kp[0, pl.ds(2 * h + m, page, stride=pitch), :] for m in range(2)],
                                   axis=-1)
            parts.append(_dot_nt(qbd, kcat.astype(BF16)))
        parts.append(_dot_nt(qbd, kn_ref[:, h * cw:(h + 1) * cw].astype(BF16)))
        s = jnp.concatenate(parts, axis=-1) * scale - slope * dist
        s = jnp.where(valid, s, -jnp.inf)
        p = jnp.exp2(s - jnp.max(s, axis=-1, keepdims=True))
        l = jnp.sum(p, axis=-1, keepdims=True)
        p16 = p.astype(BF16)
        acc = _dot(p16[:, past:], vn_ref[:, h * cw:(h + 1) * cw].astype(BF16))
        for i, vp in enumerate(vp_refs):
            vcat = jnp.concatenate(
                [vp[0, pl.ds(n_heads * half + h, page, stride=pitch), :] for half in range(2)], axis=-1)
            acc = acc + _dot(p16[:, i * page:(i + 1) * page], vcat.astype(BF16))
        on = acc / l
        o = on[:dec] - lam * on[dec:]
        o_ref[0, :, h * cw:(h + 1) * cw] = (_rms(o, dn_ref[...]) * (1.0 - lam_init)).astype(o_ref.dtype)


def _log_decay(glr, w2_ref, b_ref):
    x = _dot(glr.astype(BF16), w2_ref[...]) + b_ref[...]
    return (jnp.minimum(x, 0.0) - jnp.log(1.0 + jnp.exp(-jnp.abs(x)))) / GLA_TAU


def _split_bf16(x):
    hi = x.astype(BF16)
    r1 = x - hi.astype(F32)
    mid = r1.astype(BF16)
    lo = (r1 - mid.astype(F32)).astype(BF16)
    return hi, mid, lo


def _gla_out(o, gn_ref, gr):
    return _rms(o, gn_ref[...]) * (gr * jax.nn.sigmoid(gr))


GLA_MAX_CHUNK_DECAY = 60.0


def _gla_prompt_kernel(q_ref, k_ref, v_ref, gr_ref, glr_ref, w2_ref, b_ref, gn_ref,
                       o_ref, s_ref, st_ref, la_ref, of_ref, *, chunk, n_heads, dk, dv):
    c = pl.program_id(1)
    scale = dk ** -0.5

    @pl.when(c == 0)
    def _():
        st_ref[...] = jnp.zeros_like(st_ref)

    log_a = _log_decay(glr_ref[...], w2_ref, b_ref)
    row = lax.broadcasted_iota(jnp.int32, (chunk, chunk), 0)
    col = lax.broadcasted_iota(jnp.int32, (chunk, chunk), 1)
    causal = row >= col
    tri = jnp.where(causal, 1.0, 0.0).astype(BF16)
    hi, mid, lo = _split_bf16(log_a)
    cum_all = _dot(tri, hi) + _dot(tri, mid) + _dot(tri, lo)
    steep = jnp.max(-cum_all[chunk - 1:chunk, :]) > GLA_MAX_CHUNK_DECAY

    @pl.when(jnp.logical_not(steep))
    def _():
        for h in range(n_heads):
            cum = cum_all[:, h * dk:(h + 1) * dk]
            last = cum[chunk - 1:chunk, :]
            q = q_ref[:, h * dk:(h + 1) * dk] * scale
            k = k_ref[:, h * dk:(h + 1) * dk]
            v = v_ref[:, h * dv:(h + 1) * dv]
            st = st_ref[h]
            qt = (q * jnp.exp(cum)).astype(BF16)
            kt = (k * jnp.exp(-cum)).astype(BF16)
            att = jnp.where(causal, _dot_nt(qt, kt), 0.0)
            of_ref[:, h * dv:(h + 1) * dv] = (_dot(att.astype(BF16), v.astype(BF16))
                                              + _dot_nt(qt, st.astype(BF16)))
            kd = (k * jnp.exp(last - cum)).astype(BF16)
            st_ref[h] = st * jnp.exp(last) + _dot(v.T.astype(BF16), kd)

    @pl.when(steep)
    def _():
        la_ref[...] = log_a
        eye = (lax.broadcasted_iota(jnp.int32, (dv, dv), 0)
               == lax.broadcasted_iota(jnp.int32, (dv, dv), 1))

        sub = lax.broadcasted_iota(jnp.int32, (8, dv), 0)

        def tokens(g, carry):
            r0 = pl.multiple_of(g * 8, 8)
            a8 = jnp.exp(la_ref[pl.ds(r0, 8), :])
            q8 = q_ref[pl.ds(r0, 8), :] * scale
            k8 = k_ref[pl.ds(r0, 8), :]
            v8 = v_ref[pl.ds(r0, 8), :]
            for h in range(n_heads):
                st = st_ref[h]
                o8 = jnp.zeros((8, dv), F32)
                for r in range(8):
                    v_t = v8[r:r + 1, h * dv:(h + 1) * dv]
                    v_col = jnp.sum(jnp.where(eye, jnp.broadcast_to(v_t, (dv, dv)), 0.0),
                                    axis=-1, keepdims=True)
                    st = st * a8[r:r + 1, h * dk:(h + 1) * dk] + v_col * k8[r:r + 1, h * dk:(h + 1) * dk]
                    o_col = jnp.sum(st * q8[r:r + 1, h * dk:(h + 1) * dk], axis=-1, keepdims=True)
                    o_row = jnp.sum(jnp.where(eye, jnp.broadcast_to(o_col, (dv, dv)), 0.0),
                                    axis=0, keepdims=True)
                    o8 = jnp.where(sub == r, o_row, o8)
                st_ref[h] = st
                of_ref[pl.ds(r0, 8), h * dv:(h + 1) * dv] = o8
            return carry

        lax.fori_loop(0, chunk // 8, tokens, 0)

    for h in range(n_heads):
        o_ref[:, h * dv:(h + 1) * dv] = _gla_out(
            of_ref[:, h * dv:(h + 1) * dv], gn_ref, gr_ref[:, h * dv:(h + 1) * dv]).astype(o_ref.dtype)

    @pl.when(c == pl.num_programs(1) - 1)
    def _():
        for h in range(n_heads):
            s_ref[0, h] = st_ref[h].T


def _gla_prompt(z, glr, w2p, bias, gla_norm, *, batch, seq, n_heads, dk, dv, col_q, col_k, col_v,
                col_gr, chunk=256):
    assert seq % chunk == 0
    nc = seq // chunk
    wk, wv = n_heads * dk, n_heads * dv
    assert col_q % wk == 0 and col_k % wk == 0 and col_v % wv == 0 and col_gr % wv == 0
    bq, bk, bv, bg = col_q // wk, col_k // wk, col_v // wv, col_gr // wv
    est = (2 * chunk * (2 * wk + 2 * wv + LANE) * 4 + 2 * chunk * wv * 2 + 3 * n_heads * dk * dv * 4
           + chunk * (wk + wv) * 4 + 4 * chunk * wk * 4 + 8 * chunk * chunk * 4 + 12 * chunk * dv * 4)
    return pl.pallas_call(
        functools.partial(_gla_prompt_kernel, chunk=chunk, n_heads=n_heads, dk=dk, dv=dv),
        name="gla_prompt",
        grid=(batch, nc),
        in_specs=[
            pl.BlockSpec((chunk, wk), lambda b, c: (b * nc + c, bq)),
            pl.BlockSpec((chunk, wk), lambda b, c: (b * nc + c, bk)),
            pl.BlockSpec((chunk, wv), lambda b, c: (b * nc + c, bv)),
            pl.BlockSpec((chunk, wv), lambda b, c: (b * nc + c, bg)),
            pl.BlockSpec((chunk, LANE), lambda b, c: (b * nc + c, 0)),
            pl.BlockSpec((LANE, wk), lambda b, c: (0, 0)),
            pl.BlockSpec((1, wk), lambda b, c: (0, 0)),
            pl.BlockSpec((1, dv), lambda b, c: (0, 0)),
        ],
        out_specs=[
            pl.BlockSpec((chunk, wv), lambda b, c: (b * nc + c, 0)),
            pl.BlockSpec((1, n_heads, dk, dv), lambda b, c: (b, 0, 0, 0)),
        ],
        out_shape=[jax.ShapeDtypeStruct((batch * seq, wv), BF16),
                   jax.ShapeDtypeStruct((batch, n_heads, dk, dv), F32)],
        scratch_shapes=[pltpu.VMEM((n_heads, dv, dk), F32),
                        pltpu.VMEM((chunk, wk), F32),
                        pltpu.VMEM((chunk, wv), F32)],
        compiler_params=pltpu.CompilerParams(
            dimension_semantics=("parallel", "arbitrary"),
            vmem_limit_bytes=_vmem_limit(est)),
    )(z, z, z, z, glr, w2p, bias, gla_norm)


def _gla_sample_kernel(z_ref, glr_ref, s0_ref, w2_ref, b_ref, gn_ref, o_ref, s_ref,
                       *, n_seq, n_heads, dk, dv, dec, col_q, col_k, col_v, col_gr):
    for s in range(n_seq):
        _gla_sample_one(z_ref.at[s], glr_ref.at[s], s0_ref.at[s], w2_ref, b_ref, gn_ref,
                        o_ref.at[s], s_ref.at[s], n_heads=n_heads, dk=dk, dv=dv, dec=dec,
                        col_q=col_q, col_k=col_k, col_v=col_v, col_gr=col_gr)


def _gla_sample_one(z_ref, glr_ref, s0_ref, w2_ref, b_ref, gn_ref, o_ref, s_ref,
                    *, n_heads, dk, dv, dec, col_q, col_k, col_v, col_gr):
    log_a_all = _log_decay(glr_ref[...], w2_ref, b_ref)
    t_k = lax.broadcasted_iota(jnp.int32, (dec, dk), 0)
    t_v = lax.broadcasted_iota(jnp.int32, (dec, dv), 0)
    eye = (lax.broadcasted_iota(jnp.int32, (dk, dk), 0)
           == lax.broadcasted_iota(jnp.int32, (dk, dk), 1))

    def to_col(r):
        return jnp.sum(jnp.where(eye, jnp.broadcast_to(r, (dk, dk)), 0.0), axis=-1, keepdims=True)

    for h in range(n_heads):
        log_a = log_a_all[:, h * dk:(h + 1) * dk]
        cum = jnp.zeros((dec, dk), F32)
        for t in range(dec):
            cum = cum + jnp.where(t_k >= t, log_a[t:t + 1], 0.0)
        last = cum[dec - 1:dec]
        q = z_ref[:, col_q + h * dk: col_q + (h + 1) * dk] * (dk ** -0.5)
        k = z_ref[:, col_k + h * dk: col_k + (h + 1) * dk]
        v = z_ref[:, col_v + h * dv: col_v + (h + 1) * dv]
        gr = z_ref[:, col_gr + h * dv: col_gr + (h + 1) * dv]
        s0 = s0_ref[h]

        o = _dot((q * jnp.exp(cum)).astype(BF16), s0.astype(BF16))
        for t in range(dec):
            o_t = jnp.zeros((1, dv), F32)
            for j in range(t + 1):
                w = jnp.exp(cum[t:t + 1] - cum[j:j + 1])
                a = jnp.sum(q[t:t + 1] * k[j:j + 1] * w, axis=-1, keepdims=True)
                o_t = o_t + a * v[j:j + 1]
            o = o + jnp.where(t_v == t, o_t, 0.0)

        kd = (k * jnp.exp(last - cum)).astype(BF16)
        upd = lax.dot_general(kd, v.astype(BF16), (((0,), (0,)), ((), ())),
                              preferred_element_type=F32)
        s_ref[h] = s0 * to_col(jnp.exp(last)) + upd
        o_ref[:, h * dv:(h + 1) * dv] = _gla_out(o, gn_ref, gr).astype(o_ref.dtype)


def _gla_sample(z3, glr3, state, w2p, bias, gla_norm, *, n_heads, dk, dv, col_q, col_k, col_v,
                col_gr, n_seq=8):
    dec_b, dec, zw = z3.shape
    assert dec_b % n_seq == 0
    est = n_seq * (2 * dec * zw * 4 + 2 * 2 * n_heads * dk * dv * 4) + LANE * n_heads * dk * 2 \
        + 8 * dk * dv * 4
    return pl.pallas_call(
        functools.partial(_gla_sample_kernel, n_seq=n_seq, n_heads=n_heads, dk=dk, dv=dv, dec=dec,
                          col_q=col_q, col_k=col_k, col_v=col_v, col_gr=col_gr),
        name="gla_sample",
        grid=(dec_b // n_seq,),
        in_specs=[
            pl.BlockSpec((n_seq, dec, zw), lambda b: (b, 0, 0)),
            pl.BlockSpec((n_seq, dec, LANE), lambda b: (b, 0, 0)),
            pl.BlockSpec((n_seq, n_heads, dk, dv), lambda b: (b, 0, 0, 0)),
            pl.BlockSpec((LANE, n_heads * dk), lambda b: (0, 0)),
            pl.BlockSpec((1, n_heads * dk), lambda b: (0, 0)),
            pl.BlockSpec((1, dv), lambda b: (0, 0)),
        ],
        out_specs=[
            pl.BlockSpec((n_seq, dec, n_heads * dv), lambda b: (b, 0, 0)),
            pl.BlockSpec((n_seq, n_heads, dk, dv), lambda b: (b, 0, 0, 0)),
        ],
        out_shape=[jax.ShapeDtypeStruct((dec_b, dec, n_heads * dv), BF16),
                   jax.ShapeDtypeStruct((dec_b, n_heads, dk, dv), F32)],
        compiler_params=pltpu.CompilerParams(
            dimension_semantics=("parallel",),
            vmem_limit_bytes=_vmem_limit(est)),
    )(z3, glr3, state, w2p, bias, gla_norm)


def _round_up(x, m):
    return (x + m - 1) // m * m


def kernel(x_prompt, x_sample, cache_k, cache_v, state_gla, page_table, norm_ffn1, ffn1_w_gate, ffn1_w_up, ffn1_w_down, norm_mix, w_in, gla_gate_w2, gla_gate_b, lambda_q1, lambda_k1, lambda_q2, lambda_k2, diff_norm, gla_norm, w_out, norm_ffn2, ffn2_w_gate, ffn2_w_up, ffn2_w_down, norm_final):
    batch, seq, d = x_prompt.shape
    dec_b, dec, _ = x_sample.shape
    depth = norm_ffn1.shape[0]
    n_heads, hd = cache_k.shape[3], cache_k.shape[5]
    vd = cache_v.shape[4]
    g_heads, dk, dv = state_gla.shape[2:]
    rank = gla_gate_w2.shape[1]
    d_ff = ffn1_w_gate.shape[2]

    w_qk = n_heads * 2 * hd
    sizes = (w_qk, w_qk, n_heads * vd, g_heads * dk, g_heads * dk, g_heads * dv, g_heads * dv, rank)
    cols = [0]
    for s in sizes:
        cols.append(cols[-1] + s)
    col_dq, col_dk, col_dv, col_gq, col_gk, col_gv, col_gr, col_glr = cols[:8]
    assert w_in.shape[2] == cols[8] and rank <= LANE
    ff_pad = _round_up(d_ff, FFN_TF)

    yp = x_prompt.reshape(batch * seq, d)
    ys = x_sample.reshape(dec_b * dec, d)
    nf = norm_final.reshape(1, d)
    outs = [[] for _ in range(6)]

    for layer in range(depth):
        lam_init = _lambda_init(layer)
        ffn_w = []
        for wg, wu, wd in ((ffn1_w_gate, ffn1_w_up, ffn1_w_down), (ffn2_w_gate, ffn2_w_up, ffn2_w_down)):
            ffn_w.append((_to_bf16(wg, layer, d, ff_pad), _to_bf16(wu, layer, d, ff_pad),
                          _to_bf16(wd, layer, ff_pad, d)))
        w_main, w_glr = _w_in_prep(w_in, layer, col_glr)
        w2p = jnp.pad(gla_gate_w2[layer], ((0, LANE - rank), (0, 0))).astype(BF16)
        gate_b = gla_gate_b[layer].reshape(1, -1)
        lam_p = jnp.stack([lambda_q1[layer], lambda_k1[layer], lambda_q2[layer], lambda_k2[layer]])
        dn = diff_norm[layer].reshape(1, vd)
        gn = gla_norm[layer].reshape(1, dv)
        wo = _to_bf16(w_out, layer, w_out.shape[1], d)
        g1 = norm_ffn1[layer].reshape(1, d)
        gm = norm_mix[layer].reshape(1, d)
        g2 = norm_ffn2[layer].reshape(1, d)
        last = layer == depth - 1

        def pre(x):
            x, hmix = _ffn(x, g1, *ffn_w[0], gm, tail="next")
            return (x,) + tuple(_inproj(hmix, w_main, w_glr, n_heads=n_heads, hd=hd, col_k=col_dk,
                                        col_v=col_dv))

        def post(x, mix_a, mix_b):
            x = _outproj(x, mix_a, mix_b, wo)
            return _ffn(x, g2, *ffn_w[1], nf, tail="final" if last else "none")

        def kv_out(k_rows, v_rows, b, l):
            k = k_rows.reshape(b, l, n_heads, 2, hd)
            v = v_rows.reshape(b, l, 2, n_heads, hd).transpose(0, 1, 3, 2, 4).reshape(b, l, n_heads, vd)
            return k, v

        yp, zp, glr_p, k_rows, v_rows, kv16_p = pre(yp)
        ys, zs, glr_s, k_rows_s, v_rows_s, _ = pre(ys)
        zs3 = zs.reshape(dec_b, dec, col_glr)

        mix_a, mix_a_s = _attention(zp, kv16_p, zs3, cache_k, cache_v, page_table, layer, lam_p, dn,
                                    batch=batch, seq=seq, n_heads=n_heads, hd=hd, vd=vd, col_q=col_dq,
                                    col_k=col_dk, col_v=col_dv, lam_init=lam_init)

        mix_b, s_p = _gla_prompt(zp, glr_p, w2p, gate_b, gn, batch=batch, seq=seq, n_heads=g_heads,
                                 dk=dk, dv=dv, col_q=col_gq, col_k=col_gk, col_v=col_gv, col_gr=col_gr)
        yp = post(yp, mix_a, mix_b)
        k_new, v_new = kv_out(k_rows, v_rows, batch, seq)
        outs[0].append(k_new)
        outs[1].append(v_new)
        outs[2].append(s_p.astype(state_gla.dtype))

        mix_b, s_s = _gla_sample(zs3, glr_s.reshape(dec_b, dec, LANE), state_gla[layer], w2p, gate_b, gn,
                                 n_heads=g_heads, dk=dk, dv=dv, col_q=col_gq, col_k=col_gk,
                                 col_v=col_gv, col_gr=col_gr)
        ys = post(ys, mix_a_s.reshape(dec_b * dec, -1), mix_b.reshape(dec_b * dec, -1))
        k_new, v_new = kv_out(k_rows_s, v_rows_s, dec_b, dec)
        outs[3].append(k_new)
        outs[4].append(v_new)
        outs[5].append(s_s.astype(state_gla.dtype))

    if depth == 0:
        raise ValueError("depth must be positive")
    y_prompt = yp.reshape(batch, seq, d)
    y_sample = ys.reshape(dec_b, dec, d)
    return (y_prompt, y_sample) + tuple(jnp.stack(o) for o in outs)
```

```python
import functools
import math

import jax
import jax.numpy as jnp
from jax import lax
from jax.experimental import pallas as pl
from jax.experimental.pallas import tpu as pltpu

F32 = jnp.float32
BF16 = jnp.bfloat16

NORM_EPS = 1e-6
GLA_TAU = 16.0
LOG2_E = math.log2(math.e)
LANE = 128
VMEM_PHYSICAL = 64 * 2**20
VMEM_RESERVED = 8 * 2**20
VMEM_TEMPORARIES = 4 * 2**20
FFN_TF = 512


def _lambda_init(layer):
    return 0.8 - 0.6 * math.exp(-0.3 * layer)


def _vmem_limit(nbytes):
    return int(min(nbytes * 5 // 4 + VMEM_TEMPORARIES, VMEM_PHYSICAL - VMEM_RESERVED))


def _rms(x, gain):
    return x * lax.rsqrt(jnp.mean(x * x, axis=-1, keepdims=True) + NORM_EPS) * gain


def _dot(a, b):
    return jnp.dot(a, b, preferred_element_type=F32)


def _dot_nt(a, b):
    return lax.dot_general(a, b, (((1,), (1,)), ((), ())), preferred_element_type=F32)


def _to_bf16_kernel(w_ref, o_ref, *, rows, cols, tr):
    r_out, c_out = o_ref.shape
    c_copy = min(cols, c_out)
    w = w_ref[0][:, :c_copy]
    if rows % tr:
        r_idx = pl.program_id(0) * tr + lax.broadcasted_iota(jnp.int32, (r_out, c_copy), 0)
        w = jnp.where(r_idx < rows, w, 0.0)
    o_ref[:, :c_copy] = w.astype(BF16)
    if c_copy < c_out:
        o_ref[:, c_copy:] = jnp.zeros((r_out, c_out - c_copy), BF16)


def _to_bf16(w, layer, rows_out, cols_out, *, tr=512):
    _, rows, cols = w.shape
    assert rows_out % tr == 0 and rows_out >= rows and min(cols, cols_out) % LANE == 0
    est = 2 * tr * cols * 4 + 2 * tr * cols_out * 2 + tr * cols * 4
    return pl.pallas_call(
        functools.partial(_to_bf16_kernel, rows=rows, cols=cols, tr=tr),
        name="to_bf16",
        grid=(rows_out // tr,),
        in_specs=[pl.BlockSpec((1, tr, cols), lambda i: (layer, i, 0))],
        out_specs=pl.BlockSpec((tr, cols_out), lambda i: (i, 0)),
        out_shape=jax.ShapeDtypeStruct((rows_out, cols_out), BF16),
        compiler_params=pltpu.CompilerParams(
            dimension_semantics=("parallel",),
            vmem_limit_bytes=_vmem_limit(est)),
    )(w)


def _w_in_prep_kernel(wt_ref, wm_ref, wl_ref, *, n_main_blocks, rank, tr):
    j = pl.program_id(0)

    @pl.when(j < n_main_blocks)
    def _():
        wm_ref[...] = wt_ref[0].T.astype(BF16)

    @pl.when(j == n_main_blocks)
    def _():
        tail = wt_ref[0][:LANE]
        r_idx = lax.broadcasted_iota(jnp.int32, tail.shape, 0)
        wl_ref[...] = jnp.where(r_idx < rank, tail, 0.0).T.astype(BF16)


def _w_in_prep(w_in, layer, n_main, *, tr=512):
    wt = jnp.swapaxes(w_in, 1, 2)
    _, n, d = wt.shape
    rank = n - n_main
    nb = n_main // tr
    assert n_main % tr == 0 and 0 < rank <= LANE <= tr
    est = 2 * tr * d * 4 + 2 * d * (tr + LANE) * 2 + 2 * tr * d * 4
    return pl.pallas_call(
        functools.partial(_w_in_prep_kernel, n_main_blocks=nb, rank=rank, tr=tr),
        name="w_in_prep",
        grid=(nb + 1,),
        in_specs=[pl.BlockSpec((1, tr, d), lambda j: (layer, j, 0))],
        out_specs=[pl.BlockSpec((d, tr), lambda j: (0, jnp.minimum(j, nb - 1))),
                   pl.BlockSpec((d, LANE), lambda j: (0, 0))],
        out_shape=[jax.ShapeDtypeStruct((d, n_main), BF16), jax.ShapeDtypeStruct((d, LANE), BF16)],
        compiler_params=pltpu.CompilerParams(
            dimension_semantics=("arbitrary",),
            vmem_limit_bytes=_vmem_limit(est)),
    )(wt)


def _ffn_kernel(x_ref, g_ref, wg_ref, wu_ref, wd_ref, gt_ref, *rest, tail):
    o_ref = rest[0]
    h_ref, acc_ref = rest[-2:]
    j = pl.program_id(1)

    @pl.when(j == 0)
    def _():
        h_ref[...] = _rms(x_ref[...], g_ref[...]).astype(BF16)
        acc_ref[...] = jnp.zeros_like(acc_ref)

    h = h_ref[...]
    gate = _dot(h, wg_ref[...])
    up = _dot(h, wu_ref[...])
    act = (gate * jax.nn.sigmoid(gate)) * up
    acc_ref[...] += _dot(act.astype(BF16), wd_ref[...])

    @pl.when(j == pl.num_programs(1) - 1)
    def _():
        y = x_ref[...] + 0.5 * acc_ref[...]
        if tail == "final":
            y = _rms(y, gt_ref[...])
        o_ref[...] = y
        if tail == "next":
            rest[1][...] = _rms(y, gt_ref[...]).astype(BF16)


def _ffn(x, gain, wg, wu, wd, gain_tail, *, tail, tm=512, tf=FFN_TF):
    m, d = x.shape
    fpad = wg.shape[1]
    assert m % tm == 0 and fpad % tf == 0 and tail in ("final", "next", "none")
    est = (2 * 2 * tm * d * 4
           + 2 * tm * d * 2
           + tm * d * (2 + 4)
           + 2 * 3 * d * tf * 2
           + 3 * tm * tf * 4)
    row_spec = pl.BlockSpec((tm, d), lambda i, j: (i, 0))
    out_specs, out_shape = row_spec, jax.ShapeDtypeStruct((m, d), F32)
    if tail == "next":
        out_specs, out_shape = [row_spec, row_spec], [out_shape, jax.ShapeDtypeStruct((m, d), BF16)]
    return pl.pallas_call(
        functools.partial(_ffn_kernel, tail=tail),
        name="ffn_" + tail,
        grid=(m // tm, fpad // tf),
        in_specs=[
            row_spec,
            pl.BlockSpec((1, d), lambda i, j: (0, 0)),
            pl.BlockSpec((d, tf), lambda i, j: (0, j)),
            pl.BlockSpec((d, tf), lambda i, j: (0, j)),
            pl.BlockSpec((tf, d), lambda i, j: (j, 0)),
            pl.BlockSpec((1, d), lambda i, j: (0, 0)),
        ],
        out_specs=out_specs,
        out_shape=out_shape,
        scratch_shapes=[pltpu.VMEM((tm, d), BF16), pltpu.VMEM((tm, d), F32)],
        compiler_params=pltpu.CompilerParams(
            dimension_semantics=("parallel", "arbitrary"),
            vmem_limit_bytes=_vmem_limit(est)),
    )(x, gain, wg, wu, wd, gain_tail)


def _inproj_kernel(h_ref, w_ref, wl_ref, z_ref, glr_ref, kr_ref, vr_ref, kv16_ref,
                   *, tm, n_heads, hd, k_tile, v_tile):
    j = pl.program_id(1)
    pitch = 2 * n_heads
    z_ref[...] = _dot(h_ref[...], w_ref[...])

    @pl.when(j == 0)
    def _():
        glr_ref[...] = _dot(h_ref[...], wl_ref[...])

    @pl.when(j == k_tile)
    def _():
        kv16_ref[...] = z_ref[...].astype(BF16)
        for c in range(pitch):
            kr_ref[pl.ds(c, tm, stride=pitch), :] = z_ref[:, c * hd:(c + 1) * hd]

    @pl.when(j == v_tile)
    def _():
        kv16_ref[...] = z_ref[...].astype(BF16)
        for h in range(n_heads):
            for half in range(2):
                c = 2 * h + half
                vr_ref[pl.ds(n_heads * half + h, tm, stride=pitch), :] = z_ref[:, c * hd:(c + 1) * hd]


def _inproj(h, w, wl, *, n_heads, hd, col_k, col_v, tm=1024):
    m, d = h.shape
    n = w.shape[1]
    tn = n_heads * 2 * hd
    tm = min(tm, m)
    pitch = 2 * n_heads
    assert m % tm == 0 and n % tn == 0 and col_k % tn == 0 and col_v == col_k + tn and hd == LANE
    k_tile = col_k // tn
    last_i = m // tm - 1

    def after(i, j, written_at):
        return jnp.where(jnp.logical_and(j > written_at, i < last_i), i + 1, i)

    def kv16_index(i, j):
        moved = jnp.logical_and(j > k_tile + 1, i < last_i)
        return (jnp.where(moved, i + 1, i), jnp.where(moved, 0, jnp.clip(j - k_tile, 0, 1)))

    est = (2 * tm * d * 2 + 2 * d * (tn + LANE) * 2 + 2 * tm * (tn + LANE) * 4
           + 2 * 2 * tm * pitch * hd * 4 + tm * tn * 4 + 2 * tm * tn * 2)
    return pl.pallas_call(
        functools.partial(_inproj_kernel, tm=tm, n_heads=n_heads, hd=hd, k_tile=k_tile,
                          v_tile=k_tile + 1),
        name="in_proj",
        grid=(m // tm, n // tn),
        in_specs=[
            pl.BlockSpec((tm, d), lambda i, j: (i, 0)),
            pl.BlockSpec((d, tn), lambda i, j: (0, j)),
            pl.BlockSpec((d, LANE), lambda i, j: (0, 0)),
        ],
        out_specs=[
            pl.BlockSpec((tm, tn), lambda i, j: (i, j)),
            pl.BlockSpec((tm, LANE), lambda i, j: (after(i, j, 0), 0)),
            pl.BlockSpec((tm * pitch, hd), lambda i, j: (after(i, j, k_tile), 0)),
            pl.BlockSpec((tm * pitch, hd), lambda i, j: (after(i, j, k_tile + 1), 0)),
            pl.BlockSpec((tm, tn), kv16_index),
        ],
        out_shape=[jax.ShapeDtypeStruct((m, n), F32),
                   jax.ShapeDtypeStruct((m, LANE), F32),
                   jax.ShapeDtypeStruct((m * pitch, hd), F32),
                   jax.ShapeDtypeStruct((m * pitch, hd), F32),
                   jax.ShapeDtypeStruct((m, 2 * tn), BF16)],
        compiler_params=pltpu.CompilerParams(
            dimension_semantics=("arbitrary", "arbitrary"),
            vmem_limit_bytes=_vmem_limit(est)),
    )(h, w, wl)


def _outproj_kernel(x_ref, a_ref, b_ref, wa_ref, wb_ref, o_ref):
    o_ref[...] = x_ref[...] + _dot(a_ref[...], wa_ref[...]) + _dot(b_ref[...], wb_ref[...])


def _outproj(x, mix_a, mix_b, w, *, tm=512):
    m, d = x.shape
    ka, kb = mix_a.shape[1], mix_b.shape[1]
    assert m % tm == 0 and ka == kb and w.shape == (ka + kb, d)
    est = 2 * 2 * tm * d * 4 + 2 * tm * (ka + kb) * 2 + 2 * (ka + kb) * d * 2 + tm * d * 4
    return pl.pallas_call(
        _outproj_kernel,
        name="out_proj",
        grid=(m // tm,),
        in_specs=[
            pl.BlockSpec((tm, d), lambda i: (i, 0)),
            pl.BlockSpec((tm, ka), lambda i: (i, 0)),
            pl.BlockSpec((tm, kb), lambda i: (i, 0)),
            pl.BlockSpec((ka, d), lambda i: (0, 0)),
            pl.BlockSpec((kb, d), lambda i: (1, 0)),
        ],
        out_specs=pl.BlockSpec((tm, d), lambda i: (i, 0)),
        out_shape=jax.ShapeDtypeStruct((m, d), F32),
        compiler_params=pltpu.CompilerParams(
            dimension_semantics=("parallel",),
            vmem_limit_bytes=_vmem_limit(est)),
    )(x, mix_a, mix_b, w, w)


def _diff_lambda(lam_ref, lam_init):
    lp = lam_ref[...]
    e1 = jnp.exp(jnp.sum(lp[0:1] * lp[1:2], axis=-1, keepdims=True))
    e2 = jnp.exp(jnp.sum(lp[2:3] * lp[3:4], axis=-1, keepdims=True))
    return e1 - e2 + lam_init


def _alibi_slope(h, n_heads):
    slope = F32(0.0)
    for i in range(n_heads):
        slope = jnp.where(h == i, F32(2.0 ** (-8.0 * (i + 1) / n_heads)), slope)
    return slope


def _lane_fold(x, op):
    parts = [x[:, i * LANE:(i + 1) * LANE] for i in range(x.shape[1] // LANE)]
    return functools.reduce(op, parts)


def _attn_prompt_kernel(q_ref, k_ref, v_ref, lam_ref, dn_ref, o_ref, s_ref, m_ref, l_ref, acc_ref,
                        *, tq, tk, hd, n_heads, lam_init):
    h = pl.program_id(1)
    qi = pl.program_id(2)
    scale = hd ** -0.5 * LOG2_E
    slope = _alibi_slope(h, n_heads) * LOG2_E
    n_diag = tq // tk
    n_full = qi * n_diag
    q = q_ref[...]
    zero = jnp.zeros((tq, hd), F32)
    qbd = jnp.concatenate([jnp.concatenate([q[:, :hd], zero], axis=1),
                           jnp.concatenate([zero, q[:, hd:]], axis=1)], axis=0).astype(BF16)
    row = lax.broadcasted_iota(jnp.int32, (tq, tk), 0)
    col = lax.broadcasted_iota(jnp.int32, (tq, tk), 1)
    d0 = (row - col).astype(F32)

    m_ref[...] = jnp.full_like(m_ref, -jnp.inf)

    def scores(c, masked):
        ks = pl.multiple_of(c * tk, tk)
        kc = k_ref[pl.ds(ks, tk), :]
        off = jnp.full((1, 1), qi * tq - c * tk, jnp.int32).astype(F32)
        dist = d0 + off
        bias = slope * dist
        s = _dot_nt(qbd, kc) * scale - jnp.concatenate([bias, bias], axis=0)
        if masked:
            s = jnp.where(jnp.concatenate([dist, dist], axis=0) >= 0, s, -jnp.inf)
        s_ref[c] = s
        m_ref[...] = jnp.maximum(m_ref[...], _lane_fold(s, jnp.maximum))

    def scores_body(c, carry):
        scores(c, False)
        return carry

    lax.fori_loop(0, n_full, scores_body, 0)
    for j in range(n_diag):
        scores(n_full + j, True)

    m_row = jnp.max(m_ref[...], axis=-1, keepdims=True)
    m_ref[...] = jnp.broadcast_to(m_row, m_ref.shape)
    l_ref[...] = jnp.zeros_like(l_ref)
    acc_ref[...] = jnp.zeros_like(acc_ref)

    def pv_body(c, carry):
        ks = pl.multiple_of(c * tk, tk)
        mb = m_ref[...]
        p = jnp.exp2(s_ref[c] - jnp.concatenate([mb] * (tk // LANE), axis=1))
        l_ref[...] += _lane_fold(p, jnp.add)
        acc_ref[...] += _dot(p.astype(BF16), v_ref[pl.ds(ks, tk), :])
        return carry

    lax.fori_loop(0, n_full + n_diag, pv_body, 0)

    lam = _diff_lambda(lam_ref, lam_init)
    on = acc_ref[...] / jnp.sum(l_ref[...], axis=-1, keepdims=True)
    o = on[:tq] - lam * on[tq:]
    o_ref[...] = (_rms(o, dn_ref[...]) * (1.0 - lam_init)).astype(o_ref.dtype)


def _attn_both_kernel(*refs, n_pages, prompt, sample):
    pt_ref = refs[0]
    q_ref, k_ref, v_ref, lam_ref, dn_ref, z_ref = refs[1:7]
    pages = refs[7:7 + 2 * n_pages]
    op_ref, os_ref = refs[7 + 2 * n_pages:9 + 2 * n_pages]
    s_ref, m_ref, l_ref, acc_ref, q2_ref, kn_ref, vn_ref = refs[9 + 2 * n_pages:]
    _attn_sample_kernel(pt_ref, z_ref, *pages, lam_ref, dn_ref, os_ref, q2_ref, kn_ref, vn_ref, **sample)
    _attn_prompt_kernel(q_ref, k_ref, v_ref, lam_ref, dn_ref, op_ref, s_ref, m_ref, l_ref, acc_ref,
                        **prompt)


def _attention(z, kv16, z3, cache_k, cache_v, page_table, layer, lam_p, diff_norm, *, batch, seq,
               n_heads, hd, vd, col_q, col_k, col_v, lam_init, tq=512, tk=512):
    dec_b, dec, zw = z3.shape
    n_pages = page_table.shape[1]
    depth, n_pool, page = cache_k.shape[:3]
    nq = seq // tq
    cw = 2 * hd
    width = n_heads * cw
    prow = page * n_heads * 2
    rows = 2 * dec
    assert vd == 2 * hd and seq % tq == 0 and tq % tk == 0 and tk % LANE == 0 and col_q % cw == 0
    assert dec_b == batch * n_heads * nq
    kf = cache_k.reshape(depth * n_pool, prow, hd)
    vf = cache_v.reshape(depth * n_pool, page, n_heads, 2, hd).transpose(0, 1, 3, 2, 4).reshape(
        depth * n_pool, prow, hd)
    pt = (page_table + layer * n_pool).reshape(-1).astype(jnp.int32)

    def sample_idx(b, h, i):
        return (b * n_heads + h) * nq + i

    def page_spec(p):
        return pl.BlockSpec((1, prow, hd),
                            lambda b, h, i, pt_ref: (pt_ref[sample_idx(b, h, i) * n_pages + p], 0, 0))

    est = (2 * tq * cw * 4 + 2 * 2 * seq * cw * 2 + 2 * tq * vd * 2
           + (seq // tk) * 2 * tq * tk * 4 + 2 * 2 * tq * LANE * 4 + 2 * tq * vd * 4
           + 3 * 2 * tq * tk * 4
           + 2 * dec * zw * 4 + 2 * 2 * n_pages * page * width * 4 + 2 * dec * width * 2
           + rows * 2 * hd * 4 + 2 * page * width * 4)
    grid_spec = pltpu.PrefetchScalarGridSpec(
        num_scalar_prefetch=1,
        grid=(batch, n_heads, nq),
        in_specs=([pl.BlockSpec((tq, cw), lambda b, h, i, pt_ref: (b * nq + i, col_q // cw + h)),
                   pl.BlockSpec((seq, cw), lambda b, h, i, pt_ref: (b, h)),
                   pl.BlockSpec((seq, cw), lambda b, h, i, pt_ref: (b, n_heads + h)),
                   pl.BlockSpec((4, hd), lambda b, h, i, pt_ref: (0, 0)),
                   pl.BlockSpec((1, vd), lambda b, h, i, pt_ref: (0, 0)),
                   pl.BlockSpec((1, dec, zw), lambda b, h, i, pt_ref: (sample_idx(b, h, i), 0, 0))]
                  + [page_spec(p) for p in range(n_pages)]
                  + [page_spec(p) for p in range(n_pages)]),
        out_specs=[pl.BlockSpec((tq, vd), lambda b, h, i, pt_ref: (b * nq + i, h)),
                   pl.BlockSpec((1, dec, width), lambda b, h, i, pt_ref: (sample_idx(b, h, i), 0, 0))],
        scratch_shapes=[pltpu.VMEM((seq // tk, 2 * tq, tk), F32),
                        pltpu.VMEM((2 * tq, LANE), F32), pltpu.VMEM((2 * tq, LANE), F32),
                        pltpu.VMEM((2 * tq, vd), F32),
                        pltpu.VMEM((rows, 2 * hd), F32),
                        pltpu.VMEM((page, width), F32),
                        pltpu.VMEM((page, width), F32)],
    )
    return pl.pallas_call(
        functools.partial(
            _attn_both_kernel, n_pages=n_pages,
            prompt=dict(tq=tq, tk=tk, hd=hd, n_heads=n_heads, lam_init=lam_init),
            sample=dict(n_pages=n_pages, page=page, n_heads=n_heads, hd=hd, dec=dec, col_q=col_q,
                        col_k=col_k, col_v=col_v, lam_init=lam_init)),
        name="attention",
        grid_spec=grid_spec,
        out_shape=[jax.ShapeDtypeStruct((batch * seq, width), BF16),
                   jax.ShapeDtypeStruct((dec_b, dec, width), BF16)],
        compiler_params=pltpu.CompilerParams(
            dimension_semantics=("parallel", "parallel", "arbitrary"),
            vmem_limit_bytes=_vmem_limit(est)),
    )(pt, z, kv16, kv16, lam_p, diff_norm, z3, *([kf] * n_pages), *([vf] * n_pages))


def _attn_sample_kernel(*refs, n_pages, page, n_heads, hd, dec, col_q, col_k, col_v, lam_init):
    z_ref = refs[1]
    kp_refs = refs[2:2 + n_pages]
    vp_refs = refs[2 + n_pages:2 + 2 * n_pages]
    pitch = 2 * n_heads
    lam_ref, dn_ref, o_ref, q2_ref, kn_ref, vn_ref = refs[2 + 2 * n_pages:]
    cw = 2 * hd
    rows = 2 * dec
    scale = hd ** -0.5 * LOG2_E
    past = n_pages * page
    n_keys = past + page
    r_iota = lax.broadcasted_iota(jnp.int32, (rows, n_keys), 0)
    j_iota = lax.broadcasted_iota(jnp.int32, (rows, n_keys), 1)
    q_idx = jnp.where(r_iota >= dec, r_iota - dec, r_iota)
    dist = (past + q_idx - j_iota).astype(F32)
    valid = j_iota <= past + q_idx
    rr = lax.broadcasted_iota(jnp.int32, (rows, cw), 0)
    cc = lax.broadcasted_iota(jnp.int32, (rows, cw), 1)
    keep = (rr < dec) == (cc < hd)
    lam = _diff_lambda(lam_ref, lam_init)

    width = n_heads * cw
    kn_ref[...] = jnp.zeros_like(kn_ref)
    vn_ref[...] = jnp.zeros_like(vn_ref)
    kn_ref[0:dec, :] = z_ref[0, :, col_k:col_k + width]
    vn_ref[0:dec, :] = z_ref[0, :, col_v:col_v + width]

    for h in range(n_heads):
        slope = 2.0 ** (-8.0 * (h + 1) / n_heads) * LOG2_E
        qh = z_ref[0, :, col_q + h * cw: col_q + (h + 1) * cw]
        q2_ref[0:dec, :] = qh
        q2_ref[dec:rows, :] = qh
        qbd = jnp.where(keep, q2_ref[...], 0.0).astype(BF16)
        parts = []
        for kp in kp_refs:
            kcat = jnp.concatenate([kp[0, pl.ds(2 * h + m, page, stride=pitch), :] for m in range(2)],
                                   axis=-1)
            parts.append(_dot_nt(qbd, kcat.astype(BF16)))
        parts.append(_dot_nt(qbd, kn_ref[:, h * cw:(h + 1) * cw].astype(BF16)))
        s = jnp.concatenate(parts, axis=-1) * scale - slope * dist
        s = jnp.where(valid, s, -jnp.inf)
        p = jnp.exp2(s - jnp.max(s, axis=-1, keepdims=True))
        l = jnp.sum(p, axis=-1, keepdims=True)
        p16 = p.astype(BF16)
        acc = _dot(p16[:, past:], vn_ref[:, h * cw:(h + 1) * cw].astype(BF16))
        for i, vp in enumerate(vp_refs):
            vcat = jnp.concatenate(
                [vp[0, pl.ds(n_heads * half + h, page, stride=pitch), :] for half in range(2)], axis=-1)
            acc = acc + _dot(p16[:, i * page:(i + 1) * page], vcat.astype(BF16))
        on = acc / l
        o = on[:dec] - lam * on[dec:]
        o_ref[0, :, h * cw:(h + 1) * cw] = (_rms(o, dn_ref[...]) * (1.0 - lam_init)).astype(o_ref.dtype)


def _log_decay(glr, w2_ref, b_ref):
    x = _dot(glr.astype(BF16), w2_ref[...]) + b_ref[...]
    return (jnp.minimum(x, 0.0) - jnp.log(1.0 + jnp.exp(-jnp.abs(x)))) / GLA_TAU


def _split_bf16(x):
    hi = x.astype(BF16)
    r1 = x - hi.astype(F32)
    mid = r1.astype(BF16)
    lo = (r1 - mid.astype(F32)).astype(BF16)
    return hi, mid, lo


def _gla_out(o, gn_ref, gr):
    return _rms(o, gn_ref[...]) * (gr * jax.nn.sigmoid(gr))


GLA_MAX_CHUNK_DECAY = 60.0


def _gla_prompt_kernel(q_ref, k_ref, v_ref, gr_ref, glr_ref, w2_ref, b_ref, gn_ref,
                       o_ref, s_ref, st_ref, la_ref, of_ref, *, chunk, n_heads, dk, dv):
    c = pl.program_id(1)
    scale = dk ** -0.5

    @pl.when(c == 0)
    def _():
        st_ref[...] = jnp.zeros_like(st_ref)

    log_a = _log_decay(glr_ref[...], w2_ref, b_ref)
    row = lax.broadcasted_iota(jnp.int32, (chunk, chunk), 0)
    col = lax.broadcasted_iota(jnp.int32, (chunk, chunk), 1)
    causal = row >= col
    tri = jnp.where(causal, 1.0, 0.0).astype(BF16)
    hi, mid, lo = _split_bf16(log_a)
    cum_all = _dot(tri, hi) + _dot(tri, mid) + _dot(tri, lo)
    steep = jnp.max(-cum_all[chunk - 1:chunk, :]) > GLA_MAX_CHUNK_DECAY

    @pl.when(jnp.logical_not(steep))
    def _():
        for h in range(n_heads):
            cum = cum_all[:, h * dk:(h + 1) * dk]
            last = cum[chunk - 1:chunk, :]
            q = q_ref[:, h * dk:(h + 1) * dk] * scale
            k = k_ref[:, h * dk:(h + 1) * dk]
            v = v_ref[:, h * dv:(h + 1) * dv]
            st = st_ref[h]
            qt = (q * jnp.exp(cum)).astype(BF16)
            kt = (k * jnp.exp(-cum)).astype(BF16)
            att = jnp.where(causal, _dot_nt(qt, kt), 0.0)
            of_ref[:, h * dv:(h + 1) * dv] = (_dot(att.astype(BF16), v.astype(BF16))
                                              + _dot_nt(qt, st.astype(BF16)))
            kd = (k * jnp.exp(last - cum)).astype(BF16)
            st_ref[h] = st * jnp.exp(last) + _dot(v.T.astype(BF16), kd)

    @pl.when(steep)
    def _():
        la_ref[...] = log_a
        eye = (lax.broadcasted_iota(jnp.int32, (dv, dv), 0)
               == lax.broadcasted_iota(jnp.int32, (dv, dv), 1))

        sub = lax.broadcasted_iota(jnp.int32, (8, dv), 0)

        def tokens(g, carry):
            r0 = pl.multiple_of(g * 8, 8)
            a8 = jnp.exp(la_ref[pl.ds(r0, 8), :])
            q8 = q_ref[pl.ds(r0, 8), :] * scale
            k8 = k_ref[pl.ds(r0, 8), :]
            v8 = v_ref[pl.ds(r0, 8), :]
            for h in range(n_heads):
                st = st_ref[h]
                o8 = jnp.zeros((8, dv), F32)
                for r in range(8):
                    v_t = v8[r:r + 1, h * dv:(h + 1) * dv]
                    v_col = jnp.sum(jnp.where(eye, jnp.broadcast_to(v_t, (dv, dv)), 0.0),
                                    axis=-1, keepdims=True)
                    st = st * a8[r:r + 1, h * dk:(h + 1) * dk] + v_col * k8[r:r + 1, h * dk:(h + 1) * dk]
                    o_col = jnp.sum(st * q8[r:r + 1, h * dk:(h + 1) * dk], axis=-1, keepdims=True)
                    o_row = jnp.sum(jnp.where(eye, jnp.broadcast_to(o_col, (dv, dv)), 0.0),
                                    axis=0, keepdims=True)
                    o8 = jnp.where(sub == r, o_row, o8)
                st_ref[h] = st
                of_ref[pl.ds(r0, 8), h * dv:(h + 1) * dv] = o8
            return carry

        lax.fori_loop(0, chunk // 8, tokens, 0)

    for h in range(n_heads):
        o_ref[:, h * dv:(h + 1) * dv] = _gla_out(
            of_ref[:, h * dv:(h + 1) * dv], gn_ref, gr_ref[:, h * dv:(h + 1) * dv]).astype(o_ref.dtype)

    @pl.when(c == pl.num_programs(1) - 1)
    def _():
        for h in range(n_heads):
            s_ref[0, h] = st_ref[h].T


def _gla_prompt(z, glr, w2p, bias, gla_norm, *, batch, seq, n_heads, dk, dv, col_q, col_k, col_v,
                col_gr, chunk=256):
    assert seq % chunk == 0
    nc = seq // chunk
    wk, wv = n_heads * dk, n_heads * dv
    assert col_q % wk == 0 and col_k % wk == 0 and col_v % wv == 0 and col_gr % wv == 0
    bq, bk, bv, bg = col_q // wk, col_k // wk, col_v // wv, col_gr // wv
    est = (2 * chunk * (2 * wk + 2 * wv + LANE) * 4 + 2 * chunk * wv * 2 + 3 * n_heads * dk * dv * 4
           + chunk * (wk + wv) * 4 + 4 * chunk * wk * 4 + 8 * chunk * chunk * 4 + 12 * chunk * dv * 4)
    return pl.pallas_call(
        functools.partial(_gla_prompt_kernel, chunk=chunk, n_heads=n_heads, dk=dk, dv=dv),
        name="gla_prompt",
        grid=(batch, nc),
        in_specs=[
            pl.BlockSpec((chunk, wk), lambda b, c: (b * nc + c, bq)),
            pl.BlockSpec((chunk, wk), lambda b, c: (b * nc + c, bk)),
            pl.BlockSpec((chunk, wv), lambda b, c: (b * nc + c, bv)),
            pl.BlockSpec((chunk, wv), lambda b, c: (b * nc + c, bg)),
            pl.BlockSpec((chunk, LANE), lambda b, c: (b * nc + c, 0)),
            pl.BlockSpec((LANE, wk), lambda b, c: (0, 0)),
            pl.BlockSpec((1, wk), lambda b, c: (0, 0)),
            pl.BlockSpec((1, dv), lambda b, c: (0, 0)),
        ],
        out_specs=[
            pl.BlockSpec((chunk, wv), lambda b, c: (b * nc + c, 0)),
            pl.BlockSpec((1, n_heads, dk, dv), lambda b, c: (b, 0, 0, 0)),
        ],
        out_shape=[jax.ShapeDtypeStruct((batch * seq, wv), BF16),
                   jax.ShapeDtypeStruct((batch, n_heads, dk, dv), F32)],
        scratch_shapes=[pltpu.VMEM((n_heads, dv, dk), F32),
                        pltpu.VMEM((chunk, wk), F32),
                        pltpu.VMEM((chunk, wv), F32)],
        compiler_params=pltpu.CompilerParams(
            dimension_semantics=("parallel", "arbitrary"),
            vmem_limit_bytes=_vmem_limit(est)),
    )(z, z, z, z, glr, w2p, bias, gla_norm)


def _gla_sample_kernel(z_ref, glr_ref, s0_ref, w2_ref, b_ref, gn_ref, o_ref, s_ref,
                       *, n_seq, n_heads, dk, dv, dec, col_q, col_k, col_v, col_gr):
    for s in range(n_seq):
        _gla_sample_one(z_ref.at[s], glr_ref.at[s], s0_ref.at[s], w2_ref, b_ref, gn_ref,
                        o_ref.at[s], s_ref.at[s], n_heads=n_heads, dk=dk, dv=dv, dec=dec,
                        col_q=col_q, col_k=col_k, col_v=col_v, col_gr=col_gr)


def _gla_sample_one(z_ref, glr_ref, s0_ref, w2_ref, b_ref, gn_ref, o_ref, s_ref,
                    *, n_heads, dk, dv, dec, col_q, col_k, col_v, col_gr):
    log_a_all = _log_decay(glr_ref[...], w2_ref, b_ref)
    t_k = lax.broadcasted_iota(jnp.int32, (dec, dk), 0)
    t_v = lax.broadcasted_iota(jnp.int32, (dec, dv), 0)
    eye = (lax.broadcasted_iota(jnp.int32, (dk, dk), 0)
           == lax.broadcasted_iota(jnp.int32, (dk, dk), 1))

    def to_col(r):
        return jnp.sum(jnp.where(eye, jnp.broadcast_to(r, (dk, dk)), 0.0), axis=-1, keepdims=True)

    for h in range(n_heads):
        log_a = log_a_all[:, h * dk:(h + 1) * dk]
        cum = jnp.zeros((dec, dk), F32)
        for t in range(dec):
            cum = cum + jnp.where(t_k >= t, log_a[t:t + 1], 0.0)
        last = cum[dec - 1:dec]
        q = z_ref[:, col_q + h * dk: col_q + (h + 1) * dk] * (dk ** -0.5)
        k = z_ref[:, col_k + h * dk: col_k + (h + 1) * dk]
        v = z_ref[:, col_v + h * dv: col_v + (h + 1) * dv]
        gr = z_ref[:, col_gr + h * dv: col_gr + (h + 1) * dv]
        s0 = s0_ref[h]

        o = _dot((q * jnp.exp(cum)).astype(BF16), s0.astype(BF16))
        for t in range(dec):
            o_t = jnp.zeros((1, dv), F32)
            for j in range(t + 1):
                w = jnp.exp(cum[t:t + 1] - cum[j:j + 1])
                a = jnp.sum(q[t:t + 1] * k[j:j + 1] * w, axis=-1, keepdims=True)
                o_t = o_t + a * v[j:j + 1]
            o = o + jnp.where(t_v == t, o_t, 0.0)

        kd = (k * jnp.exp(last - cum)).astype(BF16)
        upd = lax.dot_general(kd, v.astype(BF16), (((0,), (0,)), ((), ())),
                              preferred_element_type=F32)
        s_ref[h] = s0 * to_col(jnp.exp(last)) + upd
        o_ref[:, h * dv:(h + 1) * dv] = _gla_out(o, gn_ref, gr).astype(o_ref.dtype)


def _gla_sample(z3, glr3, state, w2p, bias, gla_norm, *, n_heads, dk, dv, col_q, col_k, col_v,
                col_gr, n_seq=8):
    dec_b, dec, zw = z3.shape
    assert dec_b % n_seq == 0
    est = n_seq * (2 * dec * zw * 4 + 2 * 2 * n_heads * dk * dv * 4) + LANE * n_heads * dk * 2 \
        + 8 * dk * dv * 4
    return pl.pallas_call(
        functools.partial(_gla_sample_kernel, n_seq=n_seq, n_heads=n_heads, dk=dk, dv=dv, dec=dec,
                          col_q=col_q, col_k=col_k, col_v=col_v, col_gr=col_gr),
        name="gla_sample",
        grid=(dec_b // n_seq,),
        in_specs=[
            pl.BlockSpec((n_seq, dec, zw), lambda b: (b, 0, 0)),
            pl.BlockSpec((n_seq, dec, LANE), lambda b: (b, 0, 0)),
            pl.BlockSpec((n_seq, n_heads, dk, dv), lambda b: (b, 0, 0, 0)),
            pl.BlockSpec((LANE, n_heads * dk), lambda b: (0, 0)),
            pl.BlockSpec((1, n_heads * dk), lambda b: (0, 0)),
            pl.BlockSpec((1, dv), lambda b: (0, 0)),
        ],
        out_specs=[
            pl.BlockSpec((n_seq, dec, n_heads * dv), lambda b: (b, 0, 0)),
            pl.BlockSpec((n_seq, n_heads, dk, dv), lambda b: (b, 0, 0, 0)),
        ],
        out_shape=[jax.ShapeDtypeStruct((dec_b, dec, n_heads * dv), BF16),
                   jax.ShapeDtypeStruct((dec_b, n_heads, dk, dv), F32)],
        compiler_params=pltpu.CompilerParams(
            dimension_semantics=("parallel",),
            vmem_limit_bytes=_vmem_limit(est)),
    )(z3, glr3, state, w2p, bias, gla_norm)


def _round_up(x, m):
    return (x + m - 1) // m * m


def kernel(x_prompt, x_sample, cache_k, cache_v, state_gla, page_table, norm_ffn1, ffn1_w_gate, ffn1_w_up, ffn1_w_down, norm_mix, w_in, gla_gate_w2, gla_gate_b, lambda_q1, lambda_k1, lambda_q2, lambda_k2, diff_norm, gla_norm, w_out, norm_ffn2, ffn2_w_gate, ffn2_w_up, ffn2_w_down, norm_final):
    batch, seq, d = x_prompt.shape
    dec_b, dec, _ = x_sample.shape
    depth = norm_ffn1.shape[0]
    n_heads, hd = cache_k.shape[3], cache_k.shape[5]
    vd = cache_v.shape[4]
    g_heads, dk, dv = state_gla.shape[2:]
    rank = gla_gate_w2.shape[1]
    d_ff = ffn1_w_gate.shape[2]

    w_qk = n_heads * 2 * hd
    sizes = (w_qk, w_qk, n_heads * vd, g_heads * dk, g_heads * dk, g_heads * dv, g_heads * dv, rank)
    cols = [0]
    for s in sizes:
        cols.append(cols[-1] + s)
    col_dq, col_dk, col_dv, col_gq, col_gk, col_gv, col_gr, col_glr = cols[:8]
    assert w_in.shape[2] == cols[8] and rank <= LANE
    ff_pad = _round_up(d_ff, FFN_TF)

    yp = x_prompt.reshape(batch * seq, d)
    ys = x_sample.reshape(dec_b * dec, d)
    nf = norm_final.reshape(1, d)
    outs = [[] for _ in range(6)]

    for layer in range(depth):
        lam_init = _lambda_init(layer)
        ffn_w = []
        for wg, wu, wd in ((ffn1_w_gate, ffn1_w_up, ffn1_w_down), (ffn2_w_gate, ffn2_w_up, ffn2_w_down)):
            ffn_w.append((_to_bf16(wg, layer, d, ff_pad), _to_bf16(wu, layer, d, ff_pad),
                          _to_bf16(wd, layer, ff_pad, d)))
        w_main, w_glr = _w_in_prep(w_in, layer, col_glr)
        w2p = jnp.pad(gla_gate_w2[layer], ((0, LANE - rank), (0, 0))).astype(BF16)
        gate_b = gla_gate_b[layer].reshape(1, -1)
        lam_p = jnp.stack([lambda_q1[layer], lambda_k1[layer], lambda_q2[layer], lambda_k2[layer]])
        dn = diff_norm[layer].reshape(1, vd)
        gn = gla_norm[layer].reshape(1, dv)
        wo = _to_bf16(w_out, layer, w_out.shape[1], d)
        g1 = norm_ffn1[layer].reshape(1, d)
        gm = norm_mix[layer].reshape(1, d)
        g2 = norm_ffn2[layer].reshape(1, d)
        last = layer == depth - 1

        def pre(x):
            x, hmix = _ffn(x, g1, *ffn_w[0], gm, tail="next")
            return (x,) + tuple(_inproj(hmix, w_main, w_glr, n_heads=n_heads, hd=hd, col_k=col_dk,
                                        col_v=col_dv))

        def post(x, mix_a, mix_b):
            x = _outproj(x, mix_a, mix_b, wo)
            return _ffn(x, g2, *ffn_w[1], nf, tail="final" if last else "none")

        def kv_out(k_rows, v_rows, b, l):
            k = k_rows.reshape(b, l, n_heads, 2, hd)
            v = v_rows.reshape(b, l, 2, n_heads, hd).transpose(0, 1, 3, 2, 4).reshape(b, l, n_heads, vd)
            return k, v

        yp, zp, glr_p, k_rows, v_rows, kv16_p = pre(yp)
        ys, zs, glr_s, k_rows_s, v_rows_s, _ = pre(ys)
        zs3 = zs.reshape(dec_b, dec, col_glr)

        mix_a, mix_a_s = _attention(zp, kv16_p, zs3, cache_k, cache_v, page_table, layer, lam_p, dn,
                                    batch=batch, seq=seq, n_heads=n_heads, hd=hd, vd=vd, col_q=col_dq,
                                    col_k=col_dk, col_v=col_dv, lam_init=lam_init)

        mix_b, s_p = _gla_prompt(zp, glr_p, w2p, gate_b, gn, batch=batch, seq=seq, n_heads=g_heads,
                                 dk=dk, dv=dv, col_q=col_gq, col_k=col_gk, col_v=col_gv, col_gr=col_gr)
        yp = post(yp, mix_a, mix_b)
        k_new, v_new = kv_out(k_rows, v_rows, batch, seq)
        outs[0].append(k_new)
        outs[1].append(v_new)
        outs[2].append(s_p.astype(state_gla.dtype))

        mix_b, s_s = _gla_sample(zs3, glr_s.reshape(dec_b, dec, LANE), state_gla[layer], w2p, gate_b, gn,
                                 n_heads=g_heads, dk=dk, dv=dv, col_q=col_gq, col_k=col_gk,
                                 col_v=col_gv, col_gr=col_gr)
        ys = post(ys, mix_a_s.reshape(dec_b * dec, -1), mix_b.reshape(dec_b * dec, -1))
        k_new, v_new = kv_out(k_rows_s, v_rows_s, dec_b, dec)
        outs[3].append(k_new)
        outs[4].append(v_new)
        outs[5].append(s_s.astype(state_gla.dtype))

    if depth == 0:
        raise ValueError("depth must be positive")
    y_prompt = yp.reshape(batch, seq, d)
    y_sample = ys.reshape(dec_b, dec, d)
    return (y_prompt, y_sample) + tuple(jnp.stack(o) for o in outs)
```

```python
import functools
import math

import jax
import jax.numpy as jnp
from jax import lax
from jax.experimental import pallas as pl
from jax.experimental.pallas import tpu as pltpu

F32 = jnp.float32
BF16 = jnp.bfloat16

NORM_EPS = 1e-6
GLA_TAU = 16.0
LOG2_E = math.log2(math.e)
LANE = 128
VMEM_PHYSICAL = 64 * 2**20
VMEM_RESERVED = 8 * 2**20
VMEM_TEMPORARIES = 4 * 2**20
FFN_TF = 512
CAST_BLOCK_BYTES = 8 * 2**20


def _lambda_init(layer):
    return 0.8 - 0.6 * math.exp(-0.3 * layer)


def _vmem_limit(nbytes):
    return int(min(nbytes * 5 // 4 + VMEM_TEMPORARIES, VMEM_PHYSICAL - VMEM_RESERVED))


def _rms(x, gain):
    return x * lax.rsqrt(jnp.mean(x * x, axis=-1, keepdims=True) + NORM_EPS) * gain


def _dot(a, b):
    return jnp.dot(a, b, preferred_element_type=F32)


def _dot_nt(a, b):
    return lax.dot_general(a, b, (((1,), (1,)), ((), ())), preferred_element_type=F32)


def _to_bf16_kernel(w_ref, o_ref, *, rows, cols, tr):
    r_out, c_out = o_ref.shape
    c_copy = min(cols, c_out)
    w = w_ref[0][:, :c_copy]
    if rows % tr:
        r_idx = pl.program_id(0) * tr + lax.broadcasted_iota(jnp.int32, (r_out, c_copy), 0)
        w = jnp.where(r_idx < rows, w, 0.0)
    o_ref[:, :c_copy] = w.astype(BF16)
    if c_copy < c_out:
        o_ref[:, c_copy:] = jnp.zeros((r_out, c_out - c_copy), BF16)


def _to_bf16(w, layer, rows_out, cols_out):
    _, rows, cols = w.shape
    tr = 512 if 512 * cols * 4 <= CAST_BLOCK_BYTES else 256
    assert rows_out % tr == 0 and rows_out >= rows and min(cols, cols_out) % LANE == 0
    est = 2 * tr * cols * 4 + 2 * tr * cols_out * 2 + tr * cols * 4
    return pl.pallas_call(
        functools.partial(_to_bf16_kernel, rows=rows, cols=cols, tr=tr),
        name="to_bf16",
        grid=(rows_out // tr,),
        in_specs=[pl.BlockSpec((1, tr, cols), lambda i: (layer, i, 0))],
        out_specs=pl.BlockSpec((tr, cols_out), lambda i: (i, 0)),
        out_shape=jax.ShapeDtypeStruct((rows_out, cols_out), BF16),
        compiler_params=pltpu.CompilerParams(
            dimension_semantics=("parallel",),
            vmem_limit_bytes=_vmem_limit(est)),
    )(w)


def _to_bf16_tiles_kernel(w_ref, o_ref, *, cols):
    n_tiles, tr, tile = o_ref.shape
    for t in range(n_tiles):
        valid = min(tile, cols - t * tile)
        o_ref[t, :, :valid] = w_ref[0][:, t * tile:t * tile + valid].astype(BF16)
        if valid < tile:
            o_ref[t, :, valid:] = jnp.zeros((tr, tile - valid), BF16)


def _to_bf16_tiles(w, layer, cols_out, tile, *, tr=256):
    _, rows, cols = w.shape
    assert rows % tr == 0 and cols_out % tile == 0 and cols_out - cols < tile and cols % LANE == 0
    est = 2 * tr * cols * 4 + 2 * tr * cols_out * 2 + tr * cols * 4
    return pl.pallas_call(
        functools.partial(_to_bf16_tiles_kernel, cols=cols),
        name="to_bf16_tiles",
        grid=(rows // tr,),
        in_specs=[pl.BlockSpec((1, tr, cols), lambda i: (layer, i, 0))],
        out_specs=pl.BlockSpec((cols_out // tile, tr, tile), lambda i: (0, i, 0)),
        out_shape=jax.ShapeDtypeStruct((cols_out // tile, rows, tile), BF16),
        compiler_params=pltpu.CompilerParams(
            dimension_semantics=("parallel",),
            vmem_limit_bytes=_vmem_limit(est)),
    )(w)


def _w_in_prep_kernel(wt_ref, wm_ref, wl_ref, *, n_main_blocks, rank, tr):
    j = pl.program_id(0)

    @pl.when(j < n_main_blocks)
    def _():
        wm_ref[...] = wt_ref[0].T.astype(BF16)

    @pl.when(j == n_main_blocks)
    def _():
        tail = wt_ref[0][:LANE]
        r_idx = lax.broadcasted_iota(jnp.int32, tail.shape, 0)
        wl_ref[...] = jnp.where(r_idx < rank, tail, 0.0).T.astype(BF16)


def _w_in_prep(w_in, layer, n_main, *, tr=512):
    wt = jnp.swapaxes(w_in, 1, 2)
    _, n, d = wt.shape
    rank = n - n_main
    nb = n_main // tr
    assert n_main % tr == 0 and 0 < rank <= LANE <= tr
    est = 2 * tr * d * 4 + 2 * d * (tr + LANE) * 2 + 2 * tr * d * 4
    return pl.pallas_call(
        functools.partial(_w_in_prep_kernel, n_main_blocks=nb, rank=rank, tr=tr),
        name="w_in_prep",
        grid=(nb + 1,),
        in_specs=[pl.BlockSpec((1, tr, d), lambda j: (layer, j, 0))],
        out_specs=[pl.BlockSpec((d, tr), lambda j: (0, jnp.minimum(j, nb - 1))),
                   pl.BlockSpec((d, LANE), lambda j: (0, 0))],
        out_shape=[jax.ShapeDtypeStruct((d, n_main), BF16), jax.ShapeDtypeStruct((d, LANE), BF16)],
        compiler_params=pltpu.CompilerParams(
            dimension_semantics=("arbitrary",),
            vmem_limit_bytes=_vmem_limit(est)),
    )(wt)


def _ffn_kernel(x_ref, g_ref, wg_ref, wu_ref, wd_ref, gt_ref, *rest, tail):
    o_ref = rest[0]
    h_ref, acc_ref = rest[-2:]
    j = pl.program_id(1)

    @pl.when(j == 0)
    def _():
        h_ref[...] = _rms(x_ref[...], g_ref[...]).astype(BF16)
        acc_ref[...] = jnp.zeros_like(acc_ref)

    h = h_ref[...]
    gate = _dot(h, wg_ref[0])
    up = _dot(h, wu_ref[0])
    act = (gate * jax.nn.sigmoid(gate)) * up
    acc_ref[...] += _dot(act.astype(BF16), wd_ref[...])

    @pl.when(j == pl.num_programs(1) - 1)
    def _():
        y = x_ref[...] + 0.5 * acc_ref[...]
        if tail == "final":
            y = _rms(y, gt_ref[...])
        o_ref[...] = y
        if tail == "next":
            rest[1][...] = _rms(y, gt_ref[...]).astype(BF16)


def _ffn(x, gain, wg, wu, wd, gain_tail, *, tail, tm=512):
    m, d = x.shape
    n_tiles, _, tf = wg.shape
    fpad = n_tiles * tf
    assert m % tm == 0 and wd.shape == (fpad, d) and tail in ("final", "next", "none")
    est = (2 * 2 * tm * d * 4
           + 2 * tm * d * 2
           + tm * d * (2 + 4)
           + 2 * 3 * d * tf * 2
           + 3 * tm * tf * 4)
    row_spec = pl.BlockSpec((tm, d), lambda i, j: (i, 0))
    out_specs, out_shape = row_spec, jax.ShapeDtypeStruct((m, d), F32)
    if tail == "next":
        out_specs, out_shape = [row_spec, row_spec], [out_shape, jax.ShapeDtypeStruct((m, d), BF16)]
    return pl.pallas_call(
        functools.partial(_ffn_kernel, tail=tail),
        name="ffn_" + tail,
        grid=(m // tm, fpad // tf),
        in_specs=[
            row_spec,
            pl.BlockSpec((1, d), lambda i, j: (0, 0)),
            pl.BlockSpec((1, d, tf), lambda i, j: (j, 0, 0)),
            pl.BlockSpec((1, d, tf), lambda i, j: (j, 0, 0)),
            pl.BlockSpec((tf, d), lambda i, j: (j, 0)),
            pl.BlockSpec((1, d), lambda i, j: (0, 0)),
        ],
        out_specs=out_specs,
        out_shape=out_shape,
        scratch_shapes=[pltpu.VMEM((tm, d), BF16), pltpu.VMEM((tm, d), F32)],
        compiler_params=pltpu.CompilerParams(
            dimension_semantics=("parallel", "arbitrary"),
            vmem_limit_bytes=_vmem_limit(est)),
    )(x, gain, wg, wu, wd, gain_tail)


def _inproj_kernel(h_ref, w_ref, wl_ref, z_ref, glr_ref, kr_ref, vr_ref, kv16_ref,
                   *, tm, n_heads, hd, k_tile, v_tile):
    j = pl.program_id(1)
    pitch = 2 * n_heads
    z_ref[...] = _dot(h_ref[...], w_ref[...])

    @pl.when(j == 0)
    def _():
        glr_ref[...] = _dot(h_ref[...], wl_ref[...])

    @pl.when(j == k_tile)
    def _():
        kv16_ref[...] = z_ref[...].astype(BF16)
        for c in range(pitch):
            kr_ref[pl.ds(c, tm, stride=pitch), :] = z_ref[:, c * hd:(c + 1) * hd]

    @pl.when(j == v_tile)
    def _():
        kv16_ref[...] = z_ref[...].astype(BF16)
        for h in range(n_heads):
            for half in range(2):
                c = 2 * h + half
                vr_ref[pl.ds(n_heads * half + h, tm, stride=pitch), :] = z_ref[:, c * hd:(c + 1) * hd]


def _inproj(h, w, wl, *, n_heads, hd, col_k, col_v, tm=1024):
    m, d = h.shape
    n = w.shape[1]
    tn = n_heads * 2 * hd
    tm = min(tm, m)
    pitch = 2 * n_heads
    assert m % tm == 0 and n % tn == 0 and col_k % tn == 0 and col_v == col_k + tn and hd == LANE
    k_tile = col_k // tn
    est = (2 * tm * d * 2 + 2 * d * (tn + LANE) * 2 + 2 * tm * (tn + LANE) * 4
           + 2 * 2 * tm * pitch * hd * 4 + tm * tn * 4 + 2 * tm * tn * 2)
    return pl.pallas_call(
        functools.partial(_inproj_kernel, tm=tm, n_heads=n_heads, hd=hd, k_tile=k_tile,
                          v_tile=k_tile + 1),
        name="in_proj",
        grid=(m // tm, n // tn),
        in_specs=[
            pl.BlockSpec((tm, d), lambda i, j: (i, 0)),
            pl.BlockSpec((d, tn), lambda i, j: (0, j)),
            pl.BlockSpec((d, LANE), lambda i, j: (0, 0)),
        ],
        out_specs=[
            pl.BlockSpec((tm, tn), lambda i, j: (i, j)),
            pl.BlockSpec((tm, LANE), lambda i, j: (i, 0)),
            pl.BlockSpec((tm * pitch, hd), lambda i, j: (i, 0)),
            pl.BlockSpec((tm * pitch, hd), lambda i, j: (i, 0)),
            pl.BlockSpec((tm, tn), lambda i, j: (i, jnp.clip(j - k_tile, 0, 1))),
        ],
        out_shape=[jax.ShapeDtypeStruct((m, n), F32),
                   jax.ShapeDtypeStruct((m, LANE), F32),
                   jax.ShapeDtypeStruct((m * pitch, hd), F32),
                   jax.ShapeDtypeStruct((m * pitch, hd), F32),
                   jax.ShapeDtypeStruct((m, 2 * tn), BF16)],
        compiler_params=pltpu.CompilerParams(
            dimension_semantics=("parallel", "arbitrary"),
            vmem_limit_bytes=_vmem_limit(est)),
    )(h, w, wl)


def _outproj_kernel(x_ref, a_ref, b_ref, wa_ref, wb_ref, o_ref):
    o_ref[...] = x_ref[...] + _dot(a_ref[...], wa_ref[...]) + _dot(b_ref[...], wb_ref[...])


def _outproj(x, mix_a, mix_b, w, *, tm=512):
    m, d = x.shape
    ka, kb = mix_a.shape[1], mix_b.shape[1]
    assert m % tm == 0 and ka == kb and w.shape == (ka + kb, d)
    est = 2 * 2 * tm * d * 4 + 2 * tm * (ka + kb) * 2 + 2 * (ka + kb) * d * 2 + tm * d * 4
    return pl.pallas_call(
        _outproj_kernel,
        name="out_proj",
        grid=(m // tm,),
        in_specs=[
            pl.BlockSpec((tm, d), lambda i: (i, 0)),
            pl.BlockSpec((tm, ka), lambda i: (i, 0)),
            pl.BlockSpec((tm, kb), lambda i: (i, 0)),
            pl.BlockSpec((ka, d), lambda i: (0, 0)),
            pl.BlockSpec((kb, d), lambda i: (1, 0)),
        ],
        out_specs=pl.BlockSpec((tm, d), lambda i: (i, 0)),
        out_shape=jax.ShapeDtypeStruct((m, d), F32),
        compiler_params=pltpu.CompilerParams(
            dimension_semantics=("parallel",),
            vmem_limit_bytes=_vmem_limit(est)),
    )(x, mix_a, mix_b, w, w)


def _diff_lambda(lam_ref, lam_init):
    lp = lam_ref[...]
    e1 = jnp.exp(jnp.sum(lp[0:1] * lp[1:2], axis=-1, keepdims=True))
    e2 = jnp.exp(jnp.sum(lp[2:3] * lp[3:4], axis=-1, keepdims=True))
    return e1 - e2 + lam_init


def _alibi_slope(h, n_heads):
    slope = F32(0.0)
    for i in range(n_heads):
        slope = jnp.where(h == i, F32(2.0 ** (-8.0 * (i + 1) / n_heads)), slope)
    return slope


def _lane_fold(x, op):
    parts = [x[:, i * LANE:(i + 1) * LANE] for i in range(x.shape[1] // LANE)]
    return functools.reduce(op, parts)


def _attn_prompt_kernel(q_ref, k_ref, v_ref, lam_ref, dn_ref, o_ref, s_ref, m_ref, l_ref, acc_ref,
                        *, tq, tk, hd, n_heads, lam_init):
    h = pl.program_id(1)
    qi = pl.program_id(2)
    scale = hd ** -0.5 * LOG2_E
    slope = _alibi_slope(h, n_heads) * LOG2_E
    n_diag = tq // tk
    n_full = qi * n_diag
    q = q_ref[...]
    zero = jnp.zeros((tq, hd), F32)
    qbd = jnp.concatenate([jnp.concatenate([q[:, :hd], zero], axis=1),
                           jnp.concatenate([zero, q[:, hd:]], axis=1)], axis=0).astype(BF16)
    row = lax.broadcasted_iota(jnp.int32, (tq, tk), 0)
    col = lax.broadcasted_iota(jnp.int32, (tq, tk), 1)
    d0 = (row - col).astype(F32)

    m_ref[...] = jnp.full_like(m_ref, -jnp.inf)

    def scores(c, masked):
        ks = pl.multiple_of(c * tk, tk)
        kc = k_ref[pl.ds(ks, tk), :]
        off = jnp.full((1, 1), qi * tq - c * tk, jnp.int32).astype(F32)
        dist = d0 + off
        bias = slope * dist
        s = _dot_nt(qbd, kc) * scale - jnp.concatenate([bias, bias], axis=0)
        if masked:
            s = jnp.where(jnp.concatenate([dist, dist], axis=0) >= 0, s, -jnp.inf)
        s_ref[c] = s
        m_ref[...] = jnp.maximum(m_ref[...], _lane_fold(s, jnp.maximum))

    def scores_body(c, carry):
        scores(c, False)
        return carry

    lax.fori_loop(0, n_full, scores_body, 0)
    for j in range(n_diag):
        scores(n_full + j, True)

    m_row = jnp.max(m_ref[...], axis=-1, keepdims=True)
    m_ref[...] = jnp.broadcast_to(m_row, m_ref.shape)
    l_ref[...] = jnp.zeros_like(l_ref)
    acc_ref[...] = jnp.zeros_like(acc_ref)

    def pv_body(c, carry):
        ks = pl.multiple_of(c * tk, tk)
        mb = m_ref[...]
        p = jnp.exp2(s_ref[c] - jnp.concatenate([mb] * (tk // LANE), axis=1))
        l_ref[...] += _lane_fold(p, jnp.add)
        acc_ref[...] += _dot(p.astype(BF16), v_ref[pl.ds(ks, tk), :])
        return carry

    lax.fori_loop(0, n_full + n_diag, pv_body, 0)

    lam = _diff_lambda(lam_ref, lam_init)
    on = acc_ref[...] / jnp.sum(l_ref[...], axis=-1, keepdims=True)
    o = on[:tq] - lam * on[tq:]
    o_ref[...] = (_rms(o, dn_ref[...]) * (1.0 - lam_init)).astype(o_ref.dtype)


def _attn_both_kernel(*refs, n_pages, prompt, sample):
    pt_ref = refs[0]
    q_ref, k_ref, v_ref, lam_ref, dn_ref, z_ref = refs[1:7]
    pages = refs[7:7 + 2 * n_pages]
    op_ref, os_ref = refs[7 + 2 * n_pages:9 + 2 * n_pages]
    s_ref, m_ref, l_ref, acc_ref, q2_ref, kn_ref, vn_ref = refs[9 + 2 * n_pages:]
    _attn_sample_kernel(pt_ref, z_ref, *pages, lam_ref, dn_ref, os_ref, q2_ref, kn_ref, vn_ref, **sample)
    _attn_prompt_kernel(q_ref, k_ref, v_ref, lam_ref, dn_ref, op_ref, s_ref, m_ref, l_ref, acc_ref,
                        **prompt)


def _attention(z, kv16, z3, cache_k, cache_v, page_table, layer, lam_p, diff_norm, *, batch, seq,
               n_heads, hd, vd, col_q, col_k, col_v, lam_init, tq=512, tk=512):
    dec_b, dec, zw = z3.shape
    n_pages = page_table.shape[1]
    depth, n_pool, page = cache_k.shape[:3]
    nq = seq // tq
    cw = 2 * hd
    width = n_heads * cw
    prow = page * n_heads * 2
    rows = 2 * dec
    assert vd == 2 * hd and seq % tq == 0 and tq % tk == 0 and tk % LANE == 0 and col_q % cw == 0
    assert dec_b == batch * n_heads * nq
    kf = cache_k.reshape(depth * n_pool, prow, hd)
    vf = cache_v.reshape(depth * n_pool, page, n_heads, 2, hd).transpose(0, 1, 3, 2, 4).reshape(
        depth * n_pool, prow, hd)
    pt = (page_table + layer * n_pool).reshape(-1).astype(jnp.int32)

    def sample_idx(b, h, i):
        return (b * n_heads + h) * nq + i

    def page_spec(p):
        return pl.BlockSpec((1, prow, hd),
                            lambda b, h, i, pt_ref: (pt_ref[sample_idx(b, h, i) * n_pages + p], 0, 0))

    est = (2 * tq * cw * 4 + 2 * 2 * seq * cw * 2 + 2 * tq * vd * 2
           + (seq // tk) * 2 * tq * tk * 4 + 2 * 2 * tq * LANE * 4 + 2 * tq * vd * 4
           + 3 * 2 * tq * tk * 4
           + 2 * dec * zw * 4 + 2 * 2 * n_pages * page * width * 4 + 2 * dec * width * 2
           + rows * 2 * hd * 4 + 2 * page * width * 4)
    grid_spec = pltpu.PrefetchScalarGridSpec(
        num_scalar_prefetch=1,
        grid=(batch, n_heads, nq),
        in_specs=([pl.BlockSpec((tq, cw), lambda b, h, i, pt_ref: (b * nq + i, col_q // cw + h)),
                   pl.BlockSpec((seq, cw), lambda b, h, i, pt_ref: (b, h)),
                   pl.BlockSpec((seq, cw), lambda b, h, i, pt_ref: (b, n_heads + h)),
                   pl.BlockSpec((4, hd), lambda b, h, i, pt_ref: (0, 0)),
                   pl.BlockSpec((1, vd), lambda b, h, i, pt_ref: (0, 0)),
                   pl.BlockSpec((1, dec, zw), lambda b, h, i, pt_ref: (sample_idx(b, h, i), 0, 0))]
                  + [page_spec(p) for p in range(n_pages)]
                  + [page_spec(p) for p in range(n_pages)]),
        out_specs=[pl.BlockSpec((tq, vd), lambda b, h, i, pt_ref: (b * nq + i, h)),
                   pl.BlockSpec((1, dec, width), lambda b, h, i, pt_ref: (sample_idx(b, h, i), 0, 0))],
        scratch_shapes=[pltpu.VMEM((seq // tk, 2 * tq, tk), F32),
                        pltpu.VMEM((2 * tq, LANE), F32), pltpu.VMEM((2 * tq, LANE), F32),
                        pltpu.VMEM((2 * tq, vd), F32),
                        pltpu.VMEM((rows, 2 * hd), F32),
                        pltpu.VMEM((page, width), F32),
                        pltpu.VMEM((page, width), F32)],
    )
    return pl.pallas_call(
        functools.partial(
            _attn_both_kernel, n_pages=n_pages,
            prompt=dict(tq=tq, tk=tk, hd=hd, n_heads=n_heads, lam_init=lam_init),
            sample=dict(n_pages=n_pages, page=page, n_heads=n_heads, hd=hd, dec=dec, col_q=col_q,
                        col_k=col_k, col_v=col_v, lam_init=lam_init)),
        name="attention",
        grid_spec=grid_spec,
        out_shape=[jax.ShapeDtypeStruct((batch * seq, width), BF16),
                   jax.ShapeDtypeStruct((dec_b, dec, width), BF16)],
        compiler_params=pltpu.CompilerParams(
            dimension_semantics=("parallel", "parallel", "arbitrary"),
            vmem_limit_bytes=_vmem_limit(est)),
    )(pt, z, kv16, kv16, lam_p, diff_norm, z3, *([kf] * n_pages), *([vf] * n_pages))


def _attn_sample_kernel(*refs, n_pages, page, n_heads, hd, dec, col_q, col_k, col_v, lam_init):
    z_ref = refs[1]
    kp_refs = refs[2:2 + n_pages]
    vp_refs = refs[2 + n_pages:2 + 2 * n_pages]
    pitch = 2 * n_heads
    lam_ref, dn_ref, o_ref, q2_ref, kn_ref, vn_ref = refs[2 + 2 * n_pages:]
    cw = 2 * hd
    rows = 2 * dec
    scale = hd ** -0.5 * LOG2_E
    past = n_pages * page
    n_keys = past + page
    r_iota = lax.broadcasted_iota(jnp.int32, (rows, n_keys), 0)
    j_iota = lax.broadcasted_iota(jnp.int32, (rows, n_keys), 1)
    q_idx = jnp.where(r_iota >= dec, r_iota - dec, r_iota)
    dist = (past + q_idx - j_iota).astype(F32)
    valid = j_iota <= past + q_idx
    rr = lax.broadcasted_iota(jnp.int32, (rows, cw), 0)
    cc = lax.broadcasted_iota(jnp.int32, (rows, cw), 1)
    keep = (rr < dec) == (cc < hd)
    lam = _diff_lambda(lam_ref, lam_init)

    width = n_heads * cw
    kn_ref[...] = jnp.zeros_like(kn_ref)
    vn_ref[...] = jnp.zeros_like(vn_ref)
    kn_ref[0:dec, :] = z_ref[0, :, col_k:col_k + width]
    vn_ref[0:dec, :] = z_ref[0, :, col_v:col_v + width]

    for h in range(n_heads):
        slope = 2.0 ** (-8.0 * (h + 1) / n_heads) * LOG2_E
        qh = z_ref[0, :, col_q + h * cw: col_q + (h + 1) * cw]
        q2_ref[0:dec, :] = qh
        q2_ref[dec:rows, :] = qh
        qbd = jnp.where(keep, q2_ref[...], 0.0).astype(BF16)
        parts = []
        for kp in kp_refs:
            kcat = jnp.concatenate([kp[0, pl.ds(2 * h + m, page, stride=pitch), :] for m in range(2)],
                                   axis=-1)
            parts.append(_dot_nt(qbd, kcat.astype(BF16)))
        parts.append(_dot_nt(qbd, kn_ref[:, h * cw:(h + 1) * cw].astype(BF16)))
        s = jnp.concatenate(parts, axis=-1) * scale - slope * dist
        s = jnp.where(valid, s, -jnp.inf)
        p = jnp.exp2(s - jnp.max(s, axis=-1, keepdims=True))
        l = jnp.sum(p, axis=-1, keepdims=True)
        p16 = p.astype(BF16)
        acc = _dot(p16[:, past:], vn_ref[:, h * cw:(h + 1) * cw].astype(BF16))
        for i, vp in enumerate(vp_refs):
            vcat = jnp.concatenate(
                [vp[0, pl.ds(n_heads * half + h, page, stride=pitch), :] for half in range(2)], axis=-1)
            acc = acc + _dot(p16[:, i * page:(i + 1) * page], vcat.astype(BF16))
        on = acc / l
        o = on[:dec] - lam * on[dec:]
        o_ref[0, :, h * cw:(h + 1) * cw] = (_rms(o, dn_ref[...]) * (1.0 - lam_init)).astype(o_ref.dtype)


def _log_decay(glr, w2_ref, b_ref):
    x = _dot(glr.astype(BF16), w2_ref[...]) + b_ref[...]
    return (jnp.minimum(x, 0.0) - jnp.log(1.0 + jnp.exp(-jnp.abs(x)))) / GLA_TAU


def _split_bf16(x):
    hi = x.astype(BF16)
    r1 = x - hi.astype(F32)
    mid = r1.astype(BF16)
    lo = (r1 - mid.astype(F32)).astype(BF16)
    return hi, mid, lo


def _gla_out(o, gn_ref, gr):
    return _rms(o, gn_ref[...]) * (gr * jax.nn.sigmoid(gr))


GLA_MAX_CHUNK_DECAY = 60.0


def _gla_prompt_kernel(q_ref, k_ref, v_ref, gr_ref, glr_ref, w2_ref, b_ref, gn_ref,
                       o_ref, s_ref, st_ref, la_ref, of_ref, *, chunk, n_heads, dk, dv):
    c = pl.program_id(1)
    scale = dk ** -0.5

    @pl.when(c == 0)
    def _():
        st_ref[...] = jnp.zeros_like(st_ref)

    log_a = _log_decay(glr_ref[...], w2_ref, b_ref)
    row = lax.broadcasted_iota(jnp.int32, (chunk, chunk), 0)
    col = lax.broadcasted_iota(jnp.int32, (chunk, chunk), 1)
    causal = row >= col
    tri = jnp.where(causal, 1.0, 0.0).astype(BF16)
    hi, mid, lo = _split_bf16(log_a)
    cum_all = _dot(tri, hi) + _dot(tri, mid) + _dot(tri, lo)
    steep = jnp.max(-cum_all[chunk - 1:chunk, :]) > GLA_MAX_CHUNK_DECAY

    @pl.when(jnp.logical_not(steep))
    def _():
        for h in range(n_heads):
            cum = cum_all[:, h * dk:(h + 1) * dk]
            last = cum[chunk - 1:chunk, :]
            q = q_ref[:, h * dk:(h + 1) * dk] * scale
            k = k_ref[:, h * dk:(h + 1) * dk]
            v = v_ref[:, h * dv:(h + 1) * dv]
            st = st_ref[h]
            qt = (q * jnp.exp(cum)).astype(BF16)
            kt = (k * jnp.exp(-cum)).astype(BF16)
            att = jnp.where(causal, _dot_nt(qt, kt), 0.0)
            of_ref[:, h * dv:(h + 1) * dv] = (_dot(att.astype(BF16), v.astype(BF16))
                                              + _dot_nt(qt, st.astype(BF16)))
            kd = (k * jnp.exp(last - cum)).astype(BF16)
            st_ref[h] = st * jnp.exp(last) + _dot(v.T.astype(BF16), kd)

    @pl.when(steep)
    def _():
        la_ref[...] = log_a
        eye = (lax.broadcasted_iota(jnp.int32, (dv, dv), 0)
               == lax.broadcasted_iota(jnp.int32, (dv, dv), 1))

        sub = lax.broadcasted_iota(jnp.int32, (8, dv), 0)

        def tokens(g, carry):
            r0 = pl.multiple_of(g * 8, 8)
            a8 = jnp.exp(la_ref[pl.ds(r0, 8), :])
            q8 = q_ref[pl.ds(r0, 8), :] * scale
            k8 = k_ref[pl.ds(r0, 8), :]
            v8 = v_ref[pl.ds(r0, 8), :]
            for h in range(n_heads):
                st = st_ref[h]
                o8 = jnp.zeros((8, dv), F32)
                for r in range(8):
                    v_t = v8[r:r + 1, h * dv:(h + 1) * dv]
                    v_col = jnp.sum(jnp.where(eye, jnp.broadcast_to(v_t, (dv, dv)), 0.0),
                                    axis=-1, keepdims=True)
                    st = st * a8[r:r + 1, h * dk:(h + 1) * dk] + v_col * k8[r:r + 1, h * dk:(h + 1) * dk]
                    o_col = jnp.sum(st * q8[r:r + 1, h * dk:(h + 1) * dk], axis=-1, keepdims=True)
                    o_row = jnp.sum(jnp.where(eye, jnp.broadcast_to(o_col, (dv, dv)), 0.0),
                                    axis=0, keepdims=True)
                    o8 = jnp.where(sub == r, o_row, o8)
                st_ref[h] = st
                of_ref[pl.ds(r0, 8), h * dv:(h + 1) * dv] = o8
            return carry

        lax.fori_loop(0, chunk // 8, tokens, 0)

    for h in range(n_heads):
        o_ref[:, h * dv:(h + 1) * dv] = _gla_out(
            of_ref[:, h * dv:(h + 1) * dv], gn_ref, gr_ref[:, h * dv:(h + 1) * dv]).astype(o_ref.dtype)

    @pl.when(c == pl.num_programs(1) - 1)
    def _():
        for h in range(n_heads):
            s_ref[0, h] = st_ref[h].T


def _gla_prompt(z, glr, w2p, bias, gla_norm, *, batch, seq, n_heads, dk, dv, col_q, col_k, col_v,
                col_gr, chunk=256):
    assert seq % chunk == 0
    nc = seq // chunk
    wk, wv = n_heads * dk, n_heads * dv
    assert col_q % wk == 0 and col_k % wk == 0 and col_v % wv == 0 and col_gr % wv == 0
    bq, bk, bv, bg = col_q // wk, col_k // wk, col_v // wv, col_gr // wv
    est = (2 * chunk * (2 * wk + 2 * wv + LANE) * 4 + 2 * chunk * wv * 2 + 3 * n_heads * dk * dv * 4
           + chunk * (wk + wv) * 4 + 4 * chunk * wk * 4 + 8 * chunk * chunk * 4 + 12 * chunk * dv * 4)
    return pl.pallas_call(
        functools.partial(_gla_prompt_kernel, chunk=chunk, n_heads=n_heads, dk=dk, dv=dv),
        name="gla_prompt",
        grid=(batch, nc),
        in_specs=[
            pl.BlockSpec((chunk, wk), lambda b, c: (b * nc + c, bq)),
            pl.BlockSpec((chunk, wk), lambda b, c: (b * nc + c, bk)),
            pl.BlockSpec((chunk, wv), lambda b, c: (b * nc + c, bv)),
            pl.BlockSpec((chunk, wv), lambda b, c: (b * nc + c, bg)),
            pl.BlockSpec((chunk, LANE), lambda b, c: (b * nc + c, 0)),
            pl.BlockSpec((LANE, wk), lambda b, c: (0, 0)),
            pl.BlockSpec((1, wk), lambda b, c: (0, 0)),
            pl.BlockSpec((1, dv), lambda b, c: (0, 0)),
        ],
        out_specs=[
            pl.BlockSpec((chunk, wv), lambda b, c: (b * nc + c, 0)),
            pl.BlockSpec((1, n_heads, dk, dv), lambda b, c: (b, 0, 0, 0)),
        ],
        out_shape=[jax.ShapeDtypeStruct((batch * seq, wv), BF16),
                   jax.ShapeDtypeStruct((batch, n_heads, dk, dv), F32)],
        scratch_shapes=[pltpu.VMEM((n_heads, dv, dk), F32),
                        pltpu.VMEM((chunk, wk), F32),
                        pltpu.VMEM((chunk, wv), F32)],
        compiler_params=pltpu.CompilerParams(
            dimension_semantics=("parallel", "arbitrary"),
            vmem_limit_bytes=_vmem_limit(est)),
    )(z, z, z, z, glr, w2p, bias, gla_norm)


def _gla_sample_kernel(z_ref, glr_ref, s0_ref, w2_ref, b_ref, gn_ref, o_ref, s_ref,
                       *, n_seq, n_heads, dk, dv, dec, col_q, col_k, col_v, col_gr):
    for s in range(n_seq):
        _gla_sample_one(z_ref.at[s], glr_ref.at[s], s0_ref.at[s], w2_ref, b_ref, gn_ref,
                        o_ref.at[s], s_ref.at[s], n_heads=n_heads, dk=dk, dv=dv, dec=dec,
                        col_q=col_q, col_k=col_k, col_v=col_v, col_gr=col_gr)


def _gla_sample_one(z_ref, glr_ref, s0_ref, w2_ref, b_ref, gn_ref, o_ref, s_ref,
                    *, n_heads, dk, dv, dec, col_q, col_k, col_v, col_gr):
    log_a_all = _log_decay(glr_ref[...], w2_ref, b_ref)
    t_k = lax.broadcasted_iota(jnp.int32, (dec, dk), 0)
    t_v = lax.broadcasted_iota(jnp.int32, (dec, dv), 0)
    eye = (lax.broadcasted_iota(jnp.int32, (dk, dk), 0)
           == lax.broadcasted_iota(jnp.int32, (dk, dk), 1))

    def to_col(r):
        return jnp.sum(jnp.where(eye, jnp.broadcast_to(r, (dk, dk)), 0.0), axis=-1, keepdims=True)

    for h in range(n_heads):
        log_a = log_a_all[:, h * dk:(h + 1) * dk]
        cum = jnp.zeros((dec, dk), F32)
        for t in range(dec):
            cum = cum + jnp.where(t_k >= t, log_a[t:t + 1], 0.0)
        last = cum[dec - 1:dec]
        q = z_ref[:, col_q + h * dk: col_q + (h + 1) * dk] * (dk ** -0.5)
        k = z_ref[:, col_k + h * dk: col_k + (h + 1) * dk]
        v = z_ref[:, col_v + h * dv: col_v + (h + 1) * dv]
        gr = z_ref[:, col_gr + h * dv: col_gr + (h + 1) * dv]
        s0 = s0_ref[h]

        o = _dot((q * jnp.exp(cum)).astype(BF16), s0.astype(BF16))
        for t in range(dec):
            o_t = jnp.zeros((1, dv), F32)
            for j in range(t + 1):
                w = jnp.exp(cum[t:t + 1] - cum[j:j + 1])
                a = jnp.sum(q[t:t + 1] * k[j:j + 1] * w, axis=-1, keepdims=True)
                o_t = o_t + a * v[j:j + 1]
            o = o + jnp.where(t_v == t, o_t, 0.0)

        kd = (k * jnp.exp(last - cum)).astype(BF16)
        upd = lax.dot_general(kd, v.astype(BF16), (((0,), (0,)), ((), ())),
                              preferred_element_type=F32)
        s_ref[h] = s0 * to_col(jnp.exp(last)) + upd
        o_ref[:, h * dv:(h + 1) * dv] = _gla_out(o, gn_ref, gr).astype(o_ref.dtype)


def _gla_sample(z3, glr3, state, w2p, bias, gla_norm, *, n_heads, dk, dv, col_q, col_k, col_v,
                col_gr, n_seq=8):
    dec_b, dec, zw = z3.shape
    assert dec_b % n_seq == 0
    est = n_seq * (2 * dec * zw * 4 + 2 * 2 * n_heads * dk * dv * 4) + LANE * n_heads * dk * 2 \
        + 8 * dk * dv * 4
    return pl.pallas_call(
        functools.partial(_gla_sample_kernel, n_seq=n_seq, n_heads=n_heads, dk=dk, dv=dv, dec=dec,
                          col_q=col_q, col_k=col_k, col_v=col_v, col_gr=col_gr),
        name="gla_sample",
        grid=(dec_b // n_seq,),
        in_specs=[
            pl.BlockSpec((n_seq, dec, zw), lambda b: (b, 0, 0)),
            pl.BlockSpec((n_seq, dec, LANE), lambda b: (b, 0, 0)),
            pl.BlockSpec((n_seq, n_heads, dk, dv), lambda b: (b, 0, 0, 0)),
            pl.BlockSpec((LANE, n_heads * dk), lambda b: (0, 0)),
            pl.BlockSpec((1, n_heads * dk), lambda b: (0, 0)),
            pl.BlockSpec((1, dv), lambda b: (0, 0)),
        ],
        out_specs=[
            pl.BlockSpec((n_seq, dec, n_heads * dv), lambda b: (b, 0, 0)),
            pl.BlockSpec((n_seq, n_heads, dk, dv), lambda b: (b, 0, 0, 0)),
        ],
        out_shape=[jax.ShapeDtypeStruct((dec_b, dec, n_heads * dv), BF16),
                   jax.ShapeDtypeStruct((dec_b, n_heads, dk, dv), F32)],
        compiler_params=pltpu.CompilerParams(
            dimension_semantics=("parallel",),
            vmem_limit_bytes=_vmem_limit(est)),
    )(z3, glr3, state, w2p, bias, gla_norm)


def _round_up(x, m):
    return (x + m - 1) // m * m


def kernel(x_prompt, x_sample, cache_k, cache_v, state_gla, page_table, norm_ffn1, ffn1_w_gate, ffn1_w_up, ffn1_w_down, norm_mix, w_in, gla_gate_w2, gla_gate_b, lambda_q1, lambda_k1, lambda_q2, lambda_k2, diff_norm, gla_norm, w_out, norm_ffn2, ffn2_w_gate, ffn2_w_up, ffn2_w_down, norm_final):
    batch, seq, d = x_prompt.shape
    dec_b, dec, _ = x_sample.shape
    depth = norm_ffn1.shape[0]
    n_heads, hd = cache_k.shape[3], cache_k.shape[5]
    vd = cache_v.shape[4]
    g_heads, dk, dv = state_gla.shape[2:]
    rank = gla_gate_w2.shape[1]
    d_ff = ffn1_w_gate.shape[2]

    w_qk = n_heads * 2 * hd
    sizes = (w_qk, w_qk, n_heads * vd, g_heads * dk, g_heads * dk, g_heads * dv, g_heads * dv, rank)
    cols = [0]
    for s in sizes:
        cols.append(cols[-1] + s)
    col_dq, col_dk, col_dv, col_gq, col_gk, col_gv, col_gr, col_glr = cols[:8]
    assert w_in.shape[2] == cols[8] and rank <= LANE
    ff_pad = _round_up(d_ff, FFN_TF)

    yp = x_prompt.reshape(batch * seq, d)
    ys = x_sample.reshape(dec_b * dec, d)
    nf = norm_final.reshape(1, d)
    outs = [[] for _ in range(6)]

    for layer in range(depth):
        lam_init = _lambda_init(layer)
        ffn_w = []
        for wg, wu, wd in ((ffn1_w_gate, ffn1_w_up, ffn1_w_down), (ffn2_w_gate, ffn2_w_up, ffn2_w_down)):
            ffn_w.append((_to_bf16_tiles(wg, layer, ff_pad, FFN_TF), _to_bf16_tiles(wu, layer, ff_pad, FFN_TF),
                          _to_bf16(wd, layer, ff_pad, d)))
        w_main, w_glr = _w_in_prep(w_in, layer, col_glr)
        w2p = jnp.pad(gla_gate_w2[layer], ((0, LANE - rank), (0, 0))).astype(BF16)
        gate_b = gla_gate_b[layer].reshape(1, -1)
        lam_p = jnp.stack([lambda_q1[layer], lambda_k1[layer], lambda_q2[layer], lambda_k2[layer]])
        dn = diff_norm[layer].reshape(1, vd)
        gn = gla_norm[layer].reshape(1, dv)
        wo = _to_bf16(w_out, layer, w_out.shape[1], d)
        g1 = norm_ffn1[layer].reshape(1, d)
        gm = norm_mix[layer].reshape(1, d)
        g2 = norm_ffn2[layer].reshape(1, d)
        last = layer == depth - 1

        def pre(x):
            x, hmix = _ffn(x, g1, *ffn_w[0], gm, tail="next")
            return (x,) + tuple(_inproj(hmix, w_main, w_glr, n_heads=n_heads, hd=hd, col_k=col_dk,
                                        col_v=col_dv))

        def post(x, mix_a, mix_b):
            x = _outproj(x, mix_a, mix_b, wo)
            return _ffn(x, g2, *ffn_w[1], nf, tail="final" if last else "none")

        def kv_out(k_rows, v_rows, b, l):
            k = k_rows.reshape(b, l, n_heads, 2, hd)
            v = v_rows.reshape(b, l, 2, n_heads, hd).transpose(0, 1, 3, 2, 4).reshape(b, l, n_heads, vd)
            return k, v

        yp, zp, glr_p, k_rows, v_rows, kv16_p = pre(yp)
        ys, zs, glr_s, k_rows_s, v_rows_s, _ = pre(ys)
        zs3 = zs.reshape(dec_b, dec, col_glr)

        mix_a, mix_a_s = _attention(zp, kv16_p, zs3, cache_k, cache_v, page_table, layer, lam_p, dn,
                                    batch=batch, seq=seq, n_heads=n_heads, hd=hd, vd=vd, col_q=col_dq,
                                    col_k=col_dk, col_v=col_dv, lam_init=lam_init)

        mix_b, s_p = _gla_prompt(zp, glr_p, w2p, gate_b, gn, batch=batch, seq=seq, n_heads=g_heads,
                                 dk=dk, dv=dv, col_q=col_gq, col_k=col_gk, col_v=col_gv, col_gr=col_gr)
        yp = post(yp, mix_a, mix_b)
        k_new, v_new = kv_out(k_rows, v_rows, batch, seq)
        outs[0].append(k_new)
        outs[1].append(v_new)
        outs[2].append(s_p.astype(state_gla.dtype))

        mix_b, s_s = _gla_sample(zs3, glr_s.reshape(dec_b, dec, LANE), state_gla[layer], w2p, gate_b, gn,
                                 n_heads=g_heads, dk=dk, dv=dv, col_q=col_gq, col_k=col_gk,
                                 col_v=col_gv, col_gr=col_gr)
        ys = post(ys, mix_a_s.reshape(dec_b * dec, -1), mix_b.reshape(dec_b * dec, -1))
        k_new, v_new = kv_out(k_rows_s, v_rows_s, dec_b, dec)
        outs[3].append(k_new)
        outs[4].append(v_new)
        outs[5].append(s_s.astype(state_gla.dtype))

    if depth == 0:
        raise ValueError("depth must be positive")
    y_prompt = yp.reshape(batch, seq, d)
    y_sample = ys.reshape(dec_b, dec, d)
    return (y_prompt, y_sample) + tuple(jnp.stack(o) for o in outs)
```

```python
import functools
import math

import jax
import jax.numpy as jnp
from jax import lax
from jax.experimental import pallas as pl
from jax.experimental.pallas import tpu as pltpu

F32 = jnp.float32
BF16 = jnp.bfloat16

NORM_EPS = 1e-6
GLA_TAU = 16.0
LOG2_E = math.log2(math.e)
LANE = 128
VMEM_PHYSICAL = 64 * 2**20
VMEM_RESERVED = 8 * 2**20
VMEM_TEMPORARIES = 4 * 2**20
FFN_TF = 512
CAST_BLOCK_BYTES = 8 * 2**20


def _lambda_init(layer):
    return 0.8 - 0.6 * math.exp(-0.3 * layer)


def _vmem_limit(nbytes):
    return int(min(nbytes * 5 // 4 + VMEM_TEMPORARIES, VMEM_PHYSICAL - VMEM_RESERVED))


def _rms(x, gain):
    return x * lax.rsqrt(jnp.mean(x * x, axis=-1, keepdims=True) + NORM_EPS) * gain


def _dot(a, b):
    return jnp.dot(a, b, preferred_element_type=F32)


def _dot_nt(a, b):
    return lax.dot_general(a, b, (((1,), (1,)), ((), ())), preferred_element_type=F32)


def _to_bf16_kernel(*refs, rows, cols, tr):
    o_ref = refs[-1]
    _, r_out, c_out = o_ref.shape
    c_copy = min(cols, c_out)
    for k, w_ref in enumerate(refs[:-1]):
        w = w_ref[0][:, :c_copy]
        if rows % tr:
            r_idx = pl.program_id(0) * tr + lax.broadcasted_iota(jnp.int32, (r_out, c_copy), 0)
            w = jnp.where(r_idx < rows, w, 0.0)
        o_ref[k, :, :c_copy] = w.astype(BF16)
        if c_copy < c_out:
            o_ref[k, :, c_copy:] = jnp.zeros((r_out, c_out - c_copy), BF16)


def _to_bf16(ws, layer, rows_out, cols_out):
    _, rows, cols = ws[0].shape
    tr = 512 if len(ws) * 512 * cols * 4 <= CAST_BLOCK_BYTES else 256
    assert rows_out % tr == 0 and rows_out >= rows and min(cols, cols_out) % LANE == 0
    est = len(ws) * (2 * tr * cols * 4 + 2 * tr * cols_out * 2 + tr * cols * 4)
    return pl.pallas_call(
        functools.partial(_to_bf16_kernel, rows=rows, cols=cols, tr=tr),
        name="to_bf16",
        grid=(rows_out // tr,),
        in_specs=[pl.BlockSpec((1, tr, cols), lambda i: (layer, i, 0))] * len(ws),
        out_specs=pl.BlockSpec((len(ws), tr, cols_out), lambda i: (0, i, 0)),
        out_shape=jax.ShapeDtypeStruct((len(ws), rows_out, cols_out), BF16),
        compiler_params=pltpu.CompilerParams(
            dimension_semantics=("parallel",),
            vmem_limit_bytes=_vmem_limit(est)),
    )(*ws)


def _to_bf16_tiles_kernel(*refs, cols):
    o_ref = refs[-1]
    _, n_tiles, tr, tile = o_ref.shape
    for k, w_ref in enumerate(refs[:-1]):
        for t in range(n_tiles):
            valid = min(tile, cols - t * tile)
            o_ref[k, t, :, :valid] = w_ref[0][:, t * tile:t * tile + valid].astype(BF16)
            if valid < tile:
                o_ref[k, t, :, valid:] = jnp.zeros((tr, tile - valid), BF16)


def _to_bf16_tiles(ws, layer, cols_out, tile, *, tr=256):
    _, rows, cols = ws[0].shape
    assert rows % tr == 0 and cols_out % tile == 0 and cols_out - cols < tile and cols % LANE == 0
    est = len(ws) * (2 * tr * cols * 4 + 2 * tr * cols_out * 2 + tr * cols * 4)
    return pl.pallas_call(
        functools.partial(_to_bf16_tiles_kernel, cols=cols),
        name="to_bf16_tiles",
        grid=(rows // tr,),
        in_specs=[pl.BlockSpec((1, tr, cols), lambda i: (layer, i, 0))] * len(ws),
        out_specs=pl.BlockSpec((len(ws), cols_out // tile, tr, tile), lambda i: (0, 0, i, 0)),
        out_shape=jax.ShapeDtypeStruct((len(ws), cols_out // tile, rows, tile), BF16),
        compiler_params=pltpu.CompilerParams(
            dimension_semantics=("parallel",),
            vmem_limit_bytes=_vmem_limit(est)),
    )(*ws)


def _w_in_prep_kernel(wt_ref, wm_ref, wl_ref, *, n_main_blocks, rank, tr):
    j = pl.program_id(0)

    @pl.when(j < n_main_blocks)
    def _():
        wm_ref[...] = wt_ref[0].T.astype(BF16)

    @pl.when(j == n_main_blocks)
    def _():
        tail = wt_ref[0][:LANE]
        r_idx = lax.broadcasted_iota(jnp.int32, tail.shape, 0)
        wl_ref[...] = jnp.where(r_idx < rank, tail, 0.0).T.astype(BF16)


def _w_in_prep(w_in, layer, n_main, *, tr=512):
    wt = jnp.swapaxes(w_in, 1, 2)
    _, n, d = wt.shape
    rank = n - n_main
    nb = n_main // tr
    assert n_main % tr == 0 and 0 < rank <= LANE <= tr
    est = 2 * tr * d * 4 + 2 * d * (tr + LANE) * 2 + 2 * tr * d * 4
    return pl.pallas_call(
        functools.partial(_w_in_prep_kernel, n_main_blocks=nb, rank=rank, tr=tr),
        name="w_in_prep",
        grid=(nb + 1,),
        in_specs=[pl.BlockSpec((1, tr, d), lambda j: (layer, j, 0))],
        out_specs=[pl.BlockSpec((d, tr), lambda j: (0, jnp.minimum(j, nb - 1))),
                   pl.BlockSpec((d, LANE), lambda j: (0, 0))],
        out_shape=[jax.ShapeDtypeStruct((d, n_main), BF16), jax.ShapeDtypeStruct((d, LANE), BF16)],
        compiler_params=pltpu.CompilerParams(
            dimension_semantics=("arbitrary",),
            vmem_limit_bytes=_vmem_limit(est)),
    )(wt)


def _ffn_kernel(x_ref, g_ref, wg_ref, wu_ref, wd_ref, gt_ref, *rest, tail):
    o_ref = rest[0]
    h_ref, acc_ref = rest[-2:]
    j = pl.program_id(1)

    @pl.when(j == 0)
    def _():
        h_ref[...] = _rms(x_ref[...], g_ref[...]).astype(BF16)
        acc_ref[...] = jnp.zeros_like(acc_ref)

    h = h_ref[...]
    gate = _dot(h, wg_ref[0, 0])
    up = _dot(h, wu_ref[0, 0])
    act = (gate * jax.nn.sigmoid(gate)) * up
    acc_ref[...] += _dot(act.astype(BF16), wd_ref[0])

    @pl.when(j == pl.num_programs(1) - 1)
    def _():
        y = x_ref[...] + 0.5 * acc_ref[...]
        if tail == "final":
            y = _rms(y, gt_ref[...])
        o_ref[...] = y
        if tail == "next":
            rest[1][...] = _rms(y, gt_ref[...]).astype(BF16)


def _ffn(x, gain, wgu, wd, which, gain_tail, *, tail, tm=512):
    m, d = x.shape
    _, n_tiles, _, tf = wgu.shape
    fpad = n_tiles * tf
    assert m % tm == 0 and wd.shape[1:] == (fpad, d) and tail in ("final", "next", "none")
    est = (2 * 2 * tm * d * 4
           + 2 * tm * d * 2
           + tm * d * (2 + 4)
           + 2 * 3 * d * tf * 2
           + 3 * tm * tf * 4)
    row_spec = pl.BlockSpec((tm, d), lambda i, j: (i, 0))
    out_specs, out_shape = row_spec, jax.ShapeDtypeStruct((m, d), F32)
    if tail == "next":
        out_specs, out_shape = [row_spec, row_spec], [out_shape, jax.ShapeDtypeStruct((m, d), BF16)]
    return pl.pallas_call(
        functools.partial(_ffn_kernel, tail=tail),
        name="ffn_" + tail,
        grid=(m // tm, fpad // tf),
        in_specs=[
            row_spec,
            pl.BlockSpec((1, d), lambda i, j: (0, 0)),
            pl.BlockSpec((1, 1, d, tf), lambda i, j: (0, j, 0, 0)),
            pl.BlockSpec((1, 1, d, tf), lambda i, j: (1, j, 0, 0)),
            pl.BlockSpec((1, tf, d), lambda i, j: (which, j, 0)),
            pl.BlockSpec((1, d), lambda i, j: (0, 0)),
        ],
        out_specs=out_specs,
        out_shape=out_shape,
        scratch_shapes=[pltpu.VMEM((tm, d), BF16), pltpu.VMEM((tm, d), F32)],
        compiler_params=pltpu.CompilerParams(
            dimension_semantics=("parallel", "arbitrary"),
            vmem_limit_bytes=_vmem_limit(est)),
    )(x, gain, wgu, wgu, wd, gain_tail)


def _inproj_kernel(h_ref, w_ref, wl_ref, z_ref, glr_ref, kr_ref, vr_ref, kv16_ref,
                   *, tm, n_heads, hd, k_tile, v_tile):
    j = pl.program_id(1)
    pitch = 2 * n_heads
    z_ref[...] = _dot(h_ref[...], w_ref[...])

    @pl.when(j == 0)
    def _():
        glr_ref[...] = _dot(h_ref[...], wl_ref[...])

    @pl.when(j == k_tile)
    def _():
        kv16_ref[...] = z_ref[...].astype(BF16)
        for c in range(pitch):
            kr_ref[pl.ds(c, tm, stride=pitch), :] = z_ref[:, c * hd:(c + 1) * hd]

    @pl.when(j == v_tile)
    def _():
        kv16_ref[...] = z_ref[...].astype(BF16)
        for h in range(n_heads):
            for half in range(2):
                c = 2 * h + half
                vr_ref[pl.ds(n_heads * half + h, tm, stride=pitch), :] = z_ref[:, c * hd:(c + 1) * hd]


def _inproj(h, w, wl, *, n_heads, hd, col_k, col_v, tm=1024):
    m, d = h.shape
    n = w.shape[1]
    tn = n_heads * 2 * hd
    tm = min(tm, m)
    pitch = 2 * n_heads
    assert m % tm == 0 and n % tn == 0 and col_k % tn == 0 and col_v == col_k + tn and hd == LANE
    k_tile = col_k // tn
    est = (2 * tm * d * 2 + 2 * d * (tn + LANE) * 2 + 2 * tm * (tn + LANE) * 4
           + 2 * 2 * tm * pitch * hd * 4 + tm * tn * 4 + 2 * tm * tn * 2)
    return pl.pallas_call(
        functools.partial(_inproj_kernel, tm=tm, n_heads=n_heads, hd=hd, k_tile=k_tile,
                          v_tile=k_tile + 1),
        name="in_proj",
        grid=(m // tm, n // tn),
        in_specs=[
            pl.BlockSpec((tm, d), lambda i, j: (i, 0)),
            pl.BlockSpec((d, tn), lambda i, j: (0, j)),
            pl.BlockSpec((d, LANE), lambda i, j: (0, 0)),
        ],
        out_specs=[
            pl.BlockSpec((tm, tn), lambda i, j: (i, j)),
            pl.BlockSpec((tm, LANE), lambda i, j: (i, 0)),
            pl.BlockSpec((tm * pitch, hd), lambda i, j: (i, 0)),
            pl.BlockSpec((tm * pitch, hd), lambda i, j: (i, 0)),
            pl.BlockSpec((tm, tn), lambda i, j: (i, jnp.clip(j - k_tile, 0, 1))),
        ],
        out_shape=[jax.ShapeDtypeStruct((m, n), F32),
                   jax.ShapeDtypeStruct((m, LANE), F32),
                   jax.ShapeDtypeStruct((m * pitch, hd), F32),
                   jax.ShapeDtypeStruct((m * pitch, hd), F32),
                   jax.ShapeDtypeStruct((m, 2 * tn), BF16)],
        compiler_params=pltpu.CompilerParams(
            dimension_semantics=("parallel", "arbitrary"),
            vmem_limit_bytes=_vmem_limit(est)),
    )(h, w, wl)


def _outproj_kernel(x_ref, a_ref, b_ref, wa_ref, wb_ref, o_ref):
    o_ref[...] = x_ref[...] + _dot(a_ref[...], wa_ref[...]) + _dot(b_ref[...], wb_ref[...])


def _outproj(x, mix_a, mix_b, w, *, tm=512):
    m, d = x.shape
    ka, kb = mix_a.shape[1], mix_b.shape[1]
    assert m % tm == 0 and ka == kb and w.shape == (ka + kb, d)
    est = 2 * 2 * tm * d * 4 + 2 * tm * (ka + kb) * 2 + 2 * (ka + kb) * d * 2 + tm * d * 4
    return pl.pallas_call(
        _outproj_kernel,
        name="out_proj",
        grid=(m // tm,),
        in_specs=[
            pl.BlockSpec((tm, d), lambda i: (i, 0)),
            pl.BlockSpec((tm, ka), lambda i: (i, 0)),
            pl.BlockSpec((tm, kb), lambda i: (i, 0)),
            pl.BlockSpec((ka, d), lambda i: (0, 0)),
            pl.BlockSpec((kb, d), lambda i: (1, 0)),
        ],
        out_specs=pl.BlockSpec((tm, d), lambda i: (i, 0)),
        out_shape=jax.ShapeDtypeStruct((m, d), F32),
        compiler_params=pltpu.CompilerParams(
            dimension_semantics=("parallel",),
            vmem_limit_bytes=_vmem_limit(est)),
    )(x, mix_a, mix_b, w, w)


def _diff_lambda(lam_ref, lam_init):
    lp = lam_ref[...]
    e1 = jnp.exp(jnp.sum(lp[0:1] * lp[1:2], axis=-1, keepdims=True))
    e2 = jnp.exp(jnp.sum(lp[2:3] * lp[3:4], axis=-1, keepdims=True))
    return e1 - e2 + lam_init


def _alibi_slope(h, n_heads):
    slope = F32(0.0)
    for i in range(n_heads):
        slope = jnp.where(h == i, F32(2.0 ** (-8.0 * (i + 1) / n_heads)), slope)
    return slope


def _lane_fold(x, op):
    parts = [x[:, i * LANE:(i + 1) * LANE] for i in range(x.shape[1] // LANE)]
    return functools.reduce(op, parts)


def _attn_prompt_kernel(q_ref, k_ref, v_ref, lam_ref, dn_ref, o_ref, s_ref, m_ref, l_ref, acc_ref,
                        *, tq, tk, hd, n_heads, lam_init):
    h = pl.program_id(1)
    qi = pl.program_id(2)
    scale = hd ** -0.5 * LOG2_E
    slope = _alibi_slope(h, n_heads) * LOG2_E
    n_diag = tq // tk
    n_full = qi * n_diag
    q = q_ref[...]
    zero = jnp.zeros((tq, hd), F32)
    qbd = jnp.concatenate([jnp.concatenate([q[:, :hd], zero], axis=1),
                           jnp.concatenate([zero, q[:, hd:]], axis=1)], axis=0).astype(BF16)
    row = lax.broadcasted_iota(jnp.int32, (tq, tk), 0)
    col = lax.broadcasted_iota(jnp.int32, (tq, tk), 1)
    d0 = (row - col).astype(F32)

    m_ref[...] = jnp.full_like(m_ref, -jnp.inf)

    def scores(c, masked):
        ks = pl.multiple_of(c * tk, tk)
        kc = k_ref[pl.ds(ks, tk), :]
        off = jnp.full((1, 1), qi * tq - c * tk, jnp.int32).astype(F32)
        dist = d0 + off
        bias = slope * dist
        s = _dot_nt(qbd, kc) * scale - jnp.concatenate([bias, bias], axis=0)
        if masked:
            s = jnp.where(jnp.concatenate([dist, dist], axis=0) >= 0, s, -jnp.inf)
        s_ref[c] = s
        m_ref[...] = jnp.maximum(m_ref[...], _lane_fold(s, jnp.maximum))

    def scores_body(c, carry):
        scores(c, False)
        return carry

    lax.fori_loop(0, n_full, scores_body, 0)
    for j in range(n_diag):
        scores(n_full + j, True)

    m_row = jnp.max(m_ref[...], axis=-1, keepdims=True)
    m_ref[...] = jnp.broadcast_to(m_row, m_ref.shape)
    l_ref[...] = jnp.zeros_like(l_ref)
    acc_ref[...] = jnp.zeros_like(acc_ref)

    def pv_body(c, carry):
        ks = pl.multiple_of(c * tk, tk)
        mb = m_ref[...]
        p = jnp.exp2(s_ref[c] - jnp.concatenate([mb] * (tk // LANE), axis=1))
        l_ref[...] += _lane_fold(p, jnp.add)
        acc_ref[...] += _dot(p.astype(BF16), v_ref[pl.ds(ks, tk), :])
        return carry

    lax.fori_loop(0, n_full + n_diag, pv_body, 0)

    lam = _diff_lambda(lam_ref, lam_init)
    on = acc_ref[...] / jnp.sum(l_ref[...], axis=-1, keepdims=True)
    o = on[:tq] - lam * on[tq:]
    o_ref[...] = (_rms(o, dn_ref[...]) * (1.0 - lam_init)).astype(o_ref.dtype)


def _attn_both_kernel(*refs, n_pages, prompt, sample):
    pt_ref = refs[0]
    q_ref, k_ref, v_ref, lam_ref, dn_ref, z_ref = refs[1:7]
    pages = refs[7:7 + 2 * n_pages]
    op_ref, os_ref = refs[7 + 2 * n_pages:9 + 2 * n_pages]
    s_ref, m_ref, l_ref, acc_ref, q2_ref, kn_ref, vn_ref = refs[9 + 2 * n_pages:]
    _attn_sample_kernel(pt_ref, z_ref, *pages, lam_ref, dn_ref, os_ref, q2_ref, kn_ref, vn_ref, **sample)
    _attn_prompt_kernel(q_ref, k_ref, v_ref, lam_ref, dn_ref, op_ref, s_ref, m_ref, l_ref, acc_ref,
                        **prompt)


def _attention(z, kv16, z3, cache_k, cache_v, page_table, layer, lam_p, diff_norm, *, batch, seq,
               n_heads, hd, vd, col_q, col_k, col_v, lam_init, tq=512, tk=512):
    dec_b, dec, zw = z3.shape
    n_pages = page_table.shape[1]
    depth, n_pool, page = cache_k.shape[:3]
    nq = seq // tq
    cw = 2 * hd
    width = n_heads * cw
    prow = page * n_heads * 2
    rows = 2 * dec
    assert vd == 2 * hd and seq % tq == 0 and tq % tk == 0 and tk % LANE == 0 and col_q % cw == 0
    assert dec_b == batch * n_heads * nq
    kf = cache_k.reshape(depth * n_pool, prow, hd)
    vf = cache_v.reshape(depth * n_pool, page, n_heads, 2, hd).transpose(0, 1, 3, 2, 4).reshape(
        depth * n_pool, prow, hd)
    pt = (page_table + layer * n_pool).reshape(-1).astype(jnp.int32)

    def sample_idx(b, h, i):
        return (b * n_heads + h) * nq + i

    def page_spec(p):
        return pl.BlockSpec((1, prow, hd),
                            lambda b, h, i, pt_ref: (pt_ref[sample_idx(b, h, i) * n_pages + p], 0, 0))

    est = (2 * tq * cw * 4 + 2 * 2 * seq * cw * 2 + 2 * tq * vd * 2
           + (seq // tk) * 2 * tq * tk * 4 + 2 * 2 * tq * LANE * 4 + 2 * tq * vd * 4
           + 3 * 2 * tq * tk * 4
           + 2 * dec * zw * 4 + 2 * 2 * n_pages * page * width * 4 + 2 * dec * width * 2
           + rows * 2 * hd * 4 + 2 * page * width * 4)
    grid_spec = pltpu.PrefetchScalarGridSpec(
        num_scalar_prefetch=1,
        grid=(batch, n_heads, nq),
        in_specs=([pl.BlockSpec((tq, cw), lambda b, h, i, pt_ref: (b * nq + i, col_q // cw + h)),
                   pl.BlockSpec((seq, cw), lambda b, h, i, pt_ref: (b, h)),
                   pl.BlockSpec((seq, cw), lambda b, h, i, pt_ref: (b, n_heads + h)),
                   pl.BlockSpec((4, hd), lambda b, h, i, pt_ref: (0, 0)),
                   pl.BlockSpec((1, vd), lambda b, h, i, pt_ref: (0, 0)),
                   pl.BlockSpec((1, dec, zw), lambda b, h, i, pt_ref: (sample_idx(b, h, i), 0, 0))]
                  + [page_spec(p) for p in range(n_pages)]
                  + [page_spec(p) for p in range(n_pages)]),
        out_specs=[pl.BlockSpec((tq, vd), lambda b, h, i, pt_ref: (b * nq + i, h)),
                   pl.BlockSpec((1, dec, width), lambda b, h, i, pt_ref: (sample_idx(b, h, i), 0, 0))],
        scratch_shapes=[pltpu.VMEM((seq // tk, 2 * tq, tk), F32),
                        pltpu.VMEM((2 * tq, LANE), F32), pltpu.VMEM((2 * tq, LANE), F32),
                        pltpu.VMEM((2 * tq, vd), F32),
                        pltpu.VMEM((rows, 2 * hd), F32),
                        pltpu.VMEM((page, width), F32),
                        pltpu.VMEM((page, width), F32)],
    )
    return pl.pallas_call(
        functools.partial(
            _attn_both_kernel, n_pages=n_pages,
            prompt=dict(tq=tq, tk=tk, hd=hd, n_heads=n_heads, lam_init=lam_init),
            sample=dict(n_pages=n_pages, page=page, n_heads=n_heads, hd=hd, dec=dec, col_q=col_q,
                        col_k=col_k, col_v=col_v, lam_init=lam_init)),
        name="attention",
        grid_spec=grid_spec,
        out_shape=[jax.ShapeDtypeStruct((batch * seq, width), BF16),
                   jax.ShapeDtypeStruct((dec_b, dec, width), BF16)],
        compiler_params=pltpu.CompilerParams(
            dimension_semantics=("parallel", "parallel", "arbitrary"),
            vmem_limit_bytes=_vmem_limit(est)),
    )(pt, z, kv16, kv16, lam_p, diff_norm, z3, *([kf] * n_pages), *([vf] * n_pages))


def _attn_sample_kernel(*refs, n_pages, page, n_heads, hd, dec, col_q, col_k, col_v, lam_init):
    z_ref = refs[1]
    kp_refs = refs[2:2 + n_pages]
    vp_refs = refs[2 + n_pages:2 + 2 * n_pages]
    pitch = 2 * n_heads
    lam_ref, dn_ref, o_ref, q2_ref, kn_ref, vn_ref = refs[2 + 2 * n_pages:]
    cw = 2 * hd
    rows = 2 * dec
    scale = hd ** -0.5 * LOG2_E
    past = n_pages * page
    n_keys = past + page
    r_iota = lax.broadcasted_iota(jnp.int32, (rows, n_keys), 0)
    j_iota = lax.broadcasted_iota(jnp.int32, (rows, n_keys), 1)
    q_idx = jnp.where(r_iota >= dec, r_iota - dec, r_iota)
    dist = (past + q_idx - j_iota).astype(F32)
    valid = j_iota <= past + q_idx
    rr = lax.broadcasted_iota(jnp.int32, (rows, cw), 0)
    cc = lax.broadcasted_iota(jnp.int32, (rows, cw), 1)
    keep = (rr < dec) == (cc < hd)
    lam = _diff_lambda(lam_ref, lam_init)

    width = n_heads * cw
    kn_ref[...] = jnp.zeros_like(kn_ref)
    vn_ref[...] = jnp.zeros_like(vn_ref)
    kn_ref[0:dec, :] = z_ref[0, :, col_k:col_k + width]
    vn_ref[0:dec, :] = z_ref[0, :, col_v:col_v + width]

    for h in range(n_heads):
        slope = 2.0 ** (-8.0 * (h + 1) / n_heads) * LOG2_E
        qh = z_ref[0, :, col_q + h * cw: col_q + (h + 1) * cw]
        q2_ref[0:dec, :] = qh
        q2_ref[dec:rows, :] = qh
        qbd = jnp.where(keep, q2_ref[...], 0.0).astype(BF16)
        parts = []
        for kp in kp_refs:
            kcat = jnp.concatenate([kp[0, pl.ds(2 * h + m, page, stride=pitch), :] for m in range(2)],
                                   axis=-1)
            parts.append(_dot_nt(qbd, kcat.astype(BF16)))
        parts.append(_dot_nt(qbd, kn_ref[:, h * cw:(h + 1) * cw].astype(BF16)))
        s = jnp.concatenate(parts, axis=-1) * scale - slope * dist
        s = jnp.where(valid, s, -jnp.inf)
        p = jnp.exp2(s - jnp.max(s, axis=-1, keepdims=True))
        l = jnp.sum(p, axis=-1, keepdims=True)
        p16 = p.astype(BF16)
        acc = _dot(p16[:, past:], vn_ref[:, h * cw:(h + 1) * cw].astype(BF16))
        for i, vp in enumerate(vp_refs):
            vcat = jnp.concatenate(
                [vp[0, pl.ds(n_heads * half + h, page, stride=pitch), :] for half in range(2)], axis=-1)
            acc = acc + _dot(p16[:, i * page:(i + 1) * page], vcat.astype(BF16))
        on = acc / l
        o = on[:dec] - lam * on[dec:]
        o_ref[0, :, h * cw:(h + 1) * cw] = (_rms(o, dn_ref[...]) * (1.0 - lam_init)).astype(o_ref.dtype)


def _log_decay(glr, w2_ref, b_ref):
    x = _dot(glr.astype(BF16), w2_ref[...]) + b_ref[...]
    return (jnp.minimum(x, 0.0) - jnp.log(1.0 + jnp.exp(-jnp.abs(x)))) / GLA_TAU


def _split_bf16(x):
    hi = x.astype(BF16)
    r1 = x - hi.astype(F32)
    mid = r1.astype(BF16)
    lo = (r1 - mid.astype(F32)).astype(BF16)
    return hi, mid, lo


def _gla_out(o, gn_ref, gr):
    return _rms(o, gn_ref[...]) * (gr * jax.nn.sigmoid(gr))


GLA_MAX_CHUNK_DECAY = 60.0


def _gla_prompt_kernel(q_ref, k_ref, v_ref, gr_ref, glr_ref, w2_ref, b_ref, gn_ref,
                       o_ref, s_ref, st_ref, la_ref, of_ref, *, chunk, n_heads, dk, dv):
    c = pl.program_id(1)
    scale = dk ** -0.5

    @pl.when(c == 0)
    def _():
        st_ref[...] = jnp.zeros_like(st_ref)

    log_a = _log_decay(glr_ref[...], w2_ref, b_ref)
    row = lax.broadcasted_iota(jnp.int32, (chunk, chunk), 0)
    col = lax.broadcasted_iota(jnp.int32, (chunk, chunk), 1)
    causal = row >= col
    tri = jnp.where(causal, 1.0, 0.0).astype(BF16)
    hi, mid, lo = _split_bf16(log_a)
    cum_all = _dot(tri, hi) + _dot(tri, mid) + _dot(tri, lo)
    steep = jnp.max(-cum_all[chunk - 1:chunk, :]) > GLA_MAX_CHUNK_DECAY

    @pl.when(jnp.logical_not(steep))
    def _():
        for h in range(n_heads):
            cum = cum_all[:, h * dk:(h + 1) * dk]
            last = cum[chunk - 1:chunk, :]
            q = q_ref[:, h * dk:(h + 1) * dk] * scale
            k = k_ref[:, h * dk:(h + 1) * dk]
            v = v_ref[:, h * dv:(h + 1) * dv]
            st = st_ref[h]
            qt = (q * jnp.exp(cum)).astype(BF16)
            kt = (k * jnp.exp(-cum)).astype(BF16)
            att = jnp.where(causal, _dot_nt(qt, kt), 0.0)
            of_ref[:, h * dv:(h + 1) * dv] = (_dot(att.astype(BF16), v.astype(BF16))
                                              + _dot_nt(qt, st.astype(BF16)))
            kd = (k * jnp.exp(last - cum)).astype(BF16)
            st_ref[h] = st * jnp.exp(last) + _dot(v.T.astype(BF16), kd)

    @pl.when(steep)
    def _():
        la_ref[...] = log_a
        eye = (lax.broadcasted_iota(jnp.int32, (dv, dv), 0)
               == lax.broadcasted_iota(jnp.int32, (dv, dv), 1))

        sub = lax.broadcasted_iota(jnp.int32, (8, dv), 0)

        def tokens(g, carry):
            r0 = pl.multiple_of(g * 8, 8)
            a8 = jnp.exp(la_ref[pl.ds(r0, 8), :])
            q8 = q_ref[pl.ds(r0, 8), :] * scale
            k8 = k_ref[pl.ds(r0, 8), :]
            v8 = v_ref[pl.ds(r0, 8), :]
            for h in range(n_heads):
                st = st_ref[h]
                o8 = jnp.zeros((8, dv), F32)
                for r in range(8):
                    v_t = v8[r:r + 1, h * dv:(h + 1) * dv]
                    v_col = jnp.sum(jnp.where(eye, jnp.broadcast_to(v_t, (dv, dv)), 0.0),
                                    axis=-1, keepdims=True)
                    st = st * a8[r:r + 1, h * dk:(h + 1) * dk] + v_col * k8[r:r + 1, h * dk:(h + 1) * dk]
                    o_col = jnp.sum(st * q8[r:r + 1, h * dk:(h + 1) * dk], axis=-1, keepdims=True)
                    o_row = jnp.sum(jnp.where(eye, jnp.broadcast_to(o_col, (dv, dv)), 0.0),
                                    axis=0, keepdims=True)
                    o8 = jnp.where(sub == r, o_row, o8)
                st_ref[h] = st
                of_ref[pl.ds(r0, 8), h * dv:(h + 1) * dv] = o8
            return carry

        lax.fori_loop(0, chunk // 8, tokens, 0)

    for h in range(n_heads):
        o_ref[:, h * dv:(h + 1) * dv] = _gla_out(
            of_ref[:, h * dv:(h + 1) * dv], gn_ref, gr_ref[:, h * dv:(h + 1) * dv]).astype(o_ref.dtype)

    @pl.when(c == pl.num_programs(1) - 1)
    def _():
        for h in range(n_heads):
            s_ref[0, h] = st_ref[h].T


def _gla_prompt(z, glr, w2p, bias, gla_norm, *, batch, seq, n_heads, dk, dv, col_q, col_k, col_v,
                col_gr, chunk=256):
    assert seq % chunk == 0
    nc = seq // chunk
    wk, wv = n_heads * dk, n_heads * dv
    assert col_q % wk == 0 and col_k % wk == 0 and col_v % wv == 0 and col_gr % wv == 0
    bq, bk, bv, bg = col_q // wk, col_k // wk, col_v // wv, col_gr // wv
    est = (2 * chunk * (2 * wk + 2 * wv + LANE) * 4 + 2 * chunk * wv * 2 + 3 * n_heads * dk * dv * 4
           + chunk * (wk + wv) * 4 + 4 * chunk * wk * 4 + 8 * chunk * chunk * 4 + 12 * chunk * dv * 4)
    return pl.pallas_call(
        functools.partial(_gla_prompt_kernel, chunk=chunk, n_heads=n_heads, dk=dk, dv=dv),
        name="gla_prompt",
        grid=(batch, nc),
        in_specs=[
            pl.BlockSpec((chunk, wk), lambda b, c: (b * nc + c, bq)),
            pl.BlockSpec((chunk, wk), lambda b, c: (b * nc + c, bk)),
            pl.BlockSpec((chunk, wv), lambda b, c: (b * nc + c, bv)),
            pl.BlockSpec((chunk, wv), lambda b, c: (b * nc + c, bg)),
            pl.BlockSpec((chunk, LANE), lambda b, c: (b * nc + c, 0)),
            pl.BlockSpec((LANE, wk), lambda b, c: (0, 0)),
            pl.BlockSpec((1, wk), lambda b, c: (0, 0)),
            pl.BlockSpec((1, dv), lambda b, c: (0, 0)),
        ],
        out_specs=[
            pl.BlockSpec((chunk, wv), lambda b, c: (b * nc + c, 0)),
            pl.BlockSpec((1, n_heads, dk, dv), lambda b, c: (b, 0, 0, 0)),
        ],
        out_shape=[jax.ShapeDtypeStruct((batch * seq, wv), BF16),
                   jax.ShapeDtypeStruct((batch, n_heads, dk, dv), F32)],
        scratch_shapes=[pltpu.VMEM((n_heads, dv, dk), F32),
                        pltpu.VMEM((chunk, wk), F32),
                        pltpu.VMEM((chunk, wv), F32)],
        compiler_params=pltpu.CompilerParams(
            dimension_semantics=("parallel", "arbitrary"),
            vmem_limit_bytes=_vmem_limit(est)),
    )(z, z, z, z, glr, w2p, bias, gla_norm)


def _gla_sample_kernel(z_ref, glr_ref, s0_ref, w2_ref, b_ref, gn_ref, o_ref, s_ref,
                       *, n_seq, n_heads, dk, dv, dec, col_q, col_k, col_v, col_gr):
    for s in range(n_seq):
        _gla_sample_one(z_ref.at[s], glr_ref.at[s], s0_ref.at[s], w2_ref, b_ref, gn_ref,
                        o_ref.at[s], s_ref.at[s], n_heads=n_heads, dk=dk, dv=dv, dec=dec,
                        col_q=col_q, col_k=col_k, col_v=col_v, col_gr=col_gr)


def _gla_sample_one(z_ref, glr_ref, s0_ref, w2_ref, b_ref, gn_ref, o_ref, s_ref,
                    *, n_heads, dk, dv, dec, col_q, col_k, col_v, col_gr):
    log_a_all = _log_decay(glr_ref[...], w2_ref, b_ref)
    t_k = lax.broadcasted_iota(jnp.int32, (dec, dk), 0)
    t_v = lax.broadcasted_iota(jnp.int32, (dec, dv), 0)
    eye = (lax.broadcasted_iota(jnp.int32, (dk, dk), 0)
           == lax.broadcasted_iota(jnp.int32, (dk, dk), 1))

    def to_col(r):
        return jnp.sum(jnp.where(eye, jnp.broadcast_to(r, (dk, dk)), 0.0), axis=-1, keepdims=True)

    for h in range(n_heads):
        log_a = log_a_all[:, h * dk:(h + 1) * dk]
        cum = jnp.zeros((dec, dk), F32)
        for t in range(dec):
            cum = cum + jnp.where(t_k >= t, log_a[t:t + 1], 0.0)
        last = cum[dec - 1:dec]
        q = z_ref[:, col_q + h * dk: col_q + (h + 1) * dk] * (dk ** -0.5)
        k = z_ref[:, col_k + h * dk: col_k + (h + 1) * dk]
        v = z_ref[:, col_v + h * dv: col_v + (h + 1) * dv]
        gr = z_ref[:, col_gr + h * dv: col_gr + (h + 1) * dv]
        s0 = s0_ref[h]

        o = _dot((q * jnp.exp(cum)).astype(BF16), s0.astype(BF16))
        for t in range(dec):
            o_t = jnp.zeros((1, dv), F32)
            for j in range(t + 1):
                w = jnp.exp(cum[t:t + 1] - cum[j:j + 1])
                a = jnp.sum(q[t:t + 1] * k[j:j + 1] * w, axis=-1, keepdims=True)
                o_t = o_t + a * v[j:j + 1]
            o = o + jnp.where(t_v == t, o_t, 0.0)

        kd = (k * jnp.exp(last - cum)).astype(BF16)
        upd = lax.dot_general(kd, v.astype(BF16), (((0,), (0,)), ((), ())),
                              preferred_element_type=F32)
        s_ref[h] = s0 * to_col(jnp.exp(last)) + upd
        o_ref[:, h * dv:(h + 1) * dv] = _gla_out(o, gn_ref, gr).astype(o_ref.dtype)


def _gla_sample(z3, glr3, state, w2p, bias, gla_norm, *, n_heads, dk, dv, col_q, col_k, col_v,
                col_gr, n_seq=8):
    dec_b, dec, zw = z3.shape
    assert dec_b % n_seq == 0
    est = n_seq * (2 * dec * zw * 4 + 2 * 2 * n_heads * dk * dv * 4) + LANE * n_heads * dk * 2 \
        + 8 * dk * dv * 4
    return pl.pallas_call(
        functools.partial(_gla_sample_kernel, n_seq=n_seq, n_heads=n_heads, dk=dk, dv=dv, dec=dec,
                          col_q=col_q, col_k=col_k, col_v=col_v, col_gr=col_gr),
        name="gla_sample",
        grid=(dec_b // n_seq,),
        in_specs=[
            pl.BlockSpec((n_seq, dec, zw), lambda b: (b, 0, 0)),
            pl.BlockSpec((n_seq, dec, LANE), lambda b: (b, 0, 0)),
            pl.BlockSpec((n_seq, n_heads, dk, dv), lambda b: (b, 0, 0, 0)),
            pl.BlockSpec((LANE, n_heads * dk), lambda b: (0, 0)),
            pl.BlockSpec((1, n_heads * dk), lambda b: (0, 0)),
            pl.BlockSpec((1, dv), lambda b: (0, 0)),
        ],
        out_specs=[
            pl.BlockSpec((n_seq, dec, n_heads * dv), lambda b: (b, 0, 0)),
            pl.BlockSpec((n_seq, n_heads, dk, dv), lambda b: (b, 0, 0, 0)),
        ],
        out_shape=[jax.ShapeDtypeStruct((dec_b, dec, n_heads * dv), BF16),
                   jax.ShapeDtypeStruct((dec_b, n_heads, dk, dv), F32)],
        compiler_params=pltpu.CompilerParams(
            dimension_semantics=("parallel",),
            vmem_limit_bytes=_vmem_limit(est)),
    )(z3, glr3, state, w2p, bias, gla_norm)


def _round_up(x, m):
    return (x + m - 1) // m * m


def kernel(x_prompt, x_sample, cache_k, cache_v, state_gla, page_table, norm_ffn1, ffn1_w_gate, ffn1_w_up, ffn1_w_down, norm_mix, w_in, gla_gate_w2, gla_gate_b, lambda_q1, lambda_k1, lambda_q2, lambda_k2, diff_norm, gla_norm, w_out, norm_ffn2, ffn2_w_gate, ffn2_w_up, ffn2_w_down, norm_final):
    batch, seq, d = x_prompt.shape
    dec_b, dec, _ = x_sample.shape
    depth = norm_ffn1.shape[0]
    n_heads, hd = cache_k.shape[3], cache_k.shape[5]
    vd = cache_v.shape[4]
    g_heads, dk, dv = state_gla.shape[2:]
    rank = gla_gate_w2.shape[1]
    d_ff = ffn1_w_gate.shape[2]

    w_qk = n_heads * 2 * hd
    sizes = (w_qk, w_qk, n_heads * vd, g_heads * dk, g_heads * dk, g_heads * dv, g_heads * dv, rank)
    cols = [0]
    for s in sizes:
        cols.append(cols[-1] + s)
    col_dq, col_dk, col_dv, col_gq, col_gk, col_gv, col_gr, col_glr = cols[:8]
    assert w_in.shape[2] == cols[8] and rank <= LANE
    ff_pad = _round_up(d_ff, FFN_TF)

    yp = x_prompt.reshape(batch * seq, d)
    ys = x_sample.reshape(dec_b * dec, d)
    nf = norm_final.reshape(1, d)
    outs = [[] for _ in range(6)]

    for layer in range(depth):
        lam_init = _lambda_init(layer)
        w_down = _to_bf16((ffn1_w_down, ffn2_w_down), layer, ff_pad, d)
        ffn_w = [(_to_bf16_tiles(wgu, layer, ff_pad, FFN_TF), w_down, k)
                 for k, wgu in enumerate(((ffn1_w_gate, ffn1_w_up), (ffn2_w_gate, ffn2_w_up)))]
        w_main, w_glr = _w_in_prep(w_in, layer, col_glr)
        w2p = jnp.pad(gla_gate_w2[layer], ((0, LANE - rank), (0, 0))).astype(BF16)
        gate_b = gla_gate_b[layer].reshape(1, -1)
        lam_p = jnp.stack([lambda_q1[layer], lambda_k1[layer], lambda_q2[layer], lambda_k2[layer]])
        dn = diff_norm[layer].reshape(1, vd)
        gn = gla_norm[layer].reshape(1, dv)
        wo = _to_bf16((w_out,), layer, w_out.shape[1], d)[0]
        g1 = norm_ffn1[layer].reshape(1, d)
        gm = norm_mix[layer].reshape(1, d)
        g2 = norm_ffn2[layer].reshape(1, d)
        last = layer == depth - 1

        def pre(x):
            x, hmix = _ffn(x, g1, *ffn_w[0], gm, tail="next")
            return (x,) + tuple(_inproj(hmix, w_main, w_glr, n_heads=n_heads, hd=hd, col_k=col_dk,
                                        col_v=col_dv))

        def post(x, mix_a, mix_b):
            x = _outproj(x, mix_a, mix_b, wo)
            return _ffn(x, g2, *ffn_w[1], nf, tail="final" if last else "none")

        def kv_out(k_rows, v_rows, b, l):
            k = k_rows.reshape(b, l, n_heads, 2, hd)
            v = v_rows.reshape(b, l, 2, n_heads, hd).transpose(0, 1, 3, 2, 4).reshape(b, l, n_heads, vd)
            return k, v

        yp, zp, glr_p, k_rows, v_rows, kv16_p = pre(yp)
        ys, zs, glr_s, k_rows_s, v_rows_s, _ = pre(ys)
        zs3 = zs.reshape(dec_b, dec, col_glr)

        mix_a, mix_a_s = _attention(zp, kv16_p, zs3, cache_k, cache_v, page_table, layer, lam_p, dn,
                                    batch=batch, seq=seq, n_heads=n_heads, hd=hd, vd=vd, col_q=col_dq,
                                    col_k=col_dk, col_v=col_dv, lam_init=lam_init)

        mix_b, s_p = _gla_prompt(zp, glr_p, w2p, gate_b, gn, batch=batch, seq=seq, n_heads=g_heads,
                                 dk=dk, dv=dv, col_q=col_gq, col_k=col_gk, col_v=col_gv, col_gr=col_gr)
        yp = post(yp, mix_a, mix_b)
        k_new, v_new = kv_out(k_rows, v_rows, batch, seq)
        outs[0].append(k_new)
        outs[1].append(v_new)
        outs[2].append(s_p.astype(state_gla.dtype))

        mix_b, s_s = _gla_sample(zs3, glr_s.reshape(dec_b, dec, LANE), state_gla[layer], w2p, gate_b, gn,
                                 n_heads=g_heads, dk=dk, dv=dv, col_q=col_gq, col_k=col_gk,
                                 col_v=col_gv, col_gr=col_gr)
        ys = post(ys, mix_a_s.reshape(dec_b * dec, -1), mix_b.reshape(dec_b * dec, -1))
        k_new, v_new = kv_out(k_rows_s, v_rows_s, dec_b, dec)
        outs[3].append(k_new)
        outs[4].append(v_new)
        outs[5].append(s_s.astype(state_gla.dtype))

    if depth == 0:
        raise ValueError("depth must be positive")
    y_prompt = yp.reshape(batch, seq, d)
    y_sample = ys.reshape(dec_b, dec, d)
    return (y_prompt, y_sample) + tuple(jnp.stack(o) for o in outs)
```

```python
import functools
import math

import jax
import jax.numpy as jnp
from jax import lax
from jax.experimental import pallas as pl
from jax.experimental.pallas import tpu as pltpu

F32 = jnp.float32
BF16 = jnp.bfloat16

NORM_EPS = 1e-6
GLA_TAU = 16.0
LOG2_E = math.log2(math.e)
LANE = 128
VMEM_PHYSICAL = 64 * 2**20
VMEM_RESERVED = 8 * 2**20
VMEM_TEMPORARIES = 4 * 2**20
FFN_TF = 512
CAST_BLOCK_BYTES = 8 * 2**20
FFN_HALF = 256


def _lambda_init(layer):
    return 0.8 - 0.6 * math.exp(-0.3 * layer)


def _vmem_limit(nbytes):
    return int(min(nbytes * 5 // 4 + VMEM_TEMPORARIES, VMEM_PHYSICAL - VMEM_RESERVED))


def _rms(x, gain):
    return x * lax.rsqrt(jnp.mean(x * x, axis=-1, keepdims=True) + NORM_EPS) * gain


def _dot(a, b):
    return jnp.dot(a, b, preferred_element_type=F32)


def _dot_nt(a, b):
    return lax.dot_general(a, b, (((1,), (1,)), ((), ())), preferred_element_type=F32)


def _to_bf16_kernel(w_ref, o_ref, *, rows, cols, tr):
    r_out, c_out = o_ref.shape
    c_copy = min(cols, c_out)
    w = w_ref[0][:, :c_copy]
    if rows % tr:
        r_idx = pl.program_id(0) * tr + lax.broadcasted_iota(jnp.int32, (r_out, c_copy), 0)
        w = jnp.where(r_idx < rows, w, 0.0)
    o_ref[:, :c_copy] = w.astype(BF16)
    if c_copy < c_out:
        o_ref[:, c_copy:] = jnp.zeros((r_out, c_out - c_copy), BF16)


def _to_bf16(w, layer, rows_out, cols_out):
    _, rows, cols = w.shape
    tr = 512 if 512 * cols * 4 <= CAST_BLOCK_BYTES else 256
    assert rows_out % tr == 0 and rows_out >= rows and min(cols, cols_out) % LANE == 0
    est = 2 * tr * cols * 4 + 2 * tr * cols_out * 2 + tr * cols * 4
    return pl.pallas_call(
        functools.partial(_to_bf16_kernel, rows=rows, cols=cols, tr=tr),
        name="to_bf16",
        grid=(rows_out // tr,),
        in_specs=[pl.BlockSpec((1, tr, cols), lambda i: (layer, i, 0))],
        out_specs=pl.BlockSpec((tr, cols_out), lambda i: (i, 0)),
        out_shape=jax.ShapeDtypeStruct((rows_out, cols_out), BF16),
        compiler_params=pltpu.CompilerParams(
            dimension_semantics=("parallel",),
            vmem_limit_bytes=_vmem_limit(est)),
    )(w)


def _to_bf16_tiles_kernel(w_ref, o_ref, *, cols):
    n_tiles, tr, tile = o_ref.shape
    for t in range(n_tiles):
        valid = min(tile, cols - t * tile)
        o_ref[t, :, :valid] = w_ref[0][:, t * tile:t * tile + valid].astype(BF16)
        if valid < tile:
            o_ref[t, :, valid:] = jnp.zeros((tr, tile - valid), BF16)


def _to_bf16_tiles(w, layer, cols_out, tile, *, tr=256):
    _, rows, cols = w.shape
    assert rows % tr == 0 and cols_out % tile == 0 and cols_out - cols < tile and cols % LANE == 0
    est = 2 * tr * cols * 4 + 2 * tr * cols_out * 2 + tr * cols * 4
    return pl.pallas_call(
        functools.partial(_to_bf16_tiles_kernel, cols=cols),
        name="to_bf16_tiles",
        grid=(rows // tr,),
        in_specs=[pl.BlockSpec((1, tr, cols), lambda i: (layer, i, 0))],
        out_specs=pl.BlockSpec((cols_out // tile, tr, tile), lambda i: (0, i, 0)),
        out_shape=jax.ShapeDtypeStruct((cols_out // tile, rows, tile), BF16),
        compiler_params=pltpu.CompilerParams(
            dimension_semantics=("parallel",),
            vmem_limit_bytes=_vmem_limit(est)),
    )(w)


def _w_in_prep_kernel(wt_ref, wm_ref, wl_ref, *, n_main_blocks, rank, tr):
    j = pl.program_id(0)

    @pl.when(j < n_main_blocks)
    def _():
        wm_ref[...] = wt_ref[0].T.astype(BF16)

    @pl.when(j == n_main_blocks)
    def _():
        tail = wt_ref[0][:LANE]
        r_idx = lax.broadcasted_iota(jnp.int32, tail.shape, 0)
        wl_ref[...] = jnp.where(r_idx < rank, tail, 0.0).T.astype(BF16)


def _w_in_prep(w_in, layer, n_main, *, tr=512):
    wt = jnp.swapaxes(w_in, 1, 2)
    _, n, d = wt.shape
    rank = n - n_main
    nb = n_main // tr
    assert n_main % tr == 0 and 0 < rank <= LANE <= tr
    est = 2 * tr * d * 4 + 2 * d * (tr + LANE) * 2 + 2 * tr * d * 4
    return pl.pallas_call(
        functools.partial(_w_in_prep_kernel, n_main_blocks=nb, rank=rank, tr=tr),
        name="w_in_prep",
        grid=(nb + 1,),
        in_specs=[pl.BlockSpec((1, tr, d), lambda j: (layer, j, 0))],
        out_specs=[pl.BlockSpec((d, tr), lambda j: (0, jnp.minimum(j, nb - 1))),
                   pl.BlockSpec((d, LANE), lambda j: (0, 0))],
        out_shape=[jax.ShapeDtypeStruct((d, n_main), BF16), jax.ShapeDtypeStruct((d, LANE), BF16)],
        compiler_params=pltpu.CompilerParams(
            dimension_semantics=("arbitrary",),
            vmem_limit_bytes=_vmem_limit(est)),
    )(wt)


def _ffn_kernel(x_ref, g_ref, wg_ref, wu_ref, wd_ref, gt_ref, *rest, tail):
    o_ref = rest[0]
    h_ref, acc_ref = rest[-2:]
    j = pl.program_id(1)

    @pl.when(j == 0)
    def _():
        h_ref[...] = _rms(x_ref[...], g_ref[...]).astype(BF16)
        acc_ref[...] = jnp.zeros_like(acc_ref)

    h = h_ref[...]
    tf = wd_ref.shape[0]
    part = None
    for c in range(0, tf, FFN_HALF):
        gate = _dot(h, wg_ref[0, :, c:c + FFN_HALF])
        up = _dot(h, wu_ref[0, :, c:c + FFN_HALF])
        act = (gate * jax.nn.sigmoid(gate)) * up
        down = _dot(act.astype(BF16), wd_ref[c:c + FFN_HALF, :])
        part = down if part is None else part + down
    acc_ref[...] += part

    @pl.when(j == pl.num_programs(1) - 1)
    def _():
        y = x_ref[...] + 0.5 * acc_ref[...]
        if tail == "final":
            y = _rms(y, gt_ref[...])
        o_ref[...] = y
        if tail == "next":
            rest[1][...] = _rms(y, gt_ref[...]).astype(BF16)


def _ffn(x, gain, wg, wu, wd, gain_tail, *, tail, tm=512):
    m, d = x.shape
    n_tiles, _, tf = wg.shape
    fpad = n_tiles * tf
    assert m % tm == 0 and wd.shape == (fpad, d) and tail in ("final", "next", "none")
    est = (2 * 2 * tm * d * 4
           + 2 * tm * d * 2
           + tm * d * (2 + 4)
           + 2 * 3 * d * tf * 2
           + 3 * tm * tf * 4)
    row_spec = pl.BlockSpec((tm, d), lambda i, j: (i, 0))
    out_specs, out_shape = row_spec, jax.ShapeDtypeStruct((m, d), F32)
    if tail == "next":
        out_specs, out_shape = [row_spec, row_spec], [out_shape, jax.ShapeDtypeStruct((m, d), BF16)]
    return pl.pallas_call(
        functools.partial(_ffn_kernel, tail=tail),
        name="ffn_" + tail,
        grid=(m // tm, fpad // tf),
        in_specs=[
            row_spec,
            pl.BlockSpec((1, d), lambda i, j: (0, 0)),
            pl.BlockSpec((1, d, tf), lambda i, j: (j, 0, 0)),
            pl.BlockSpec((1, d, tf), lambda i, j: (j, 0, 0)),
            pl.BlockSpec((tf, d), lambda i, j: (j, 0)),
            pl.BlockSpec((1, d), lambda i, j: (0, 0)),
        ],
        out_specs=out_specs,
        out_shape=out_shape,
        scratch_shapes=[pltpu.VMEM((tm, d), BF16), pltpu.VMEM((tm, d), F32)],
        compiler_params=pltpu.CompilerParams(
            dimension_semantics=("parallel", "arbitrary"),
            vmem_limit_bytes=_vmem_limit(est)),
    )(x, gain, wg, wu, wd, gain_tail)


def _inproj_kernel(h_ref, w_ref, wl_ref, z_ref, glr_ref, kr_ref, vr_ref, kv16_ref,
                   *, tm, n_heads, hd, k_tile, v_tile):
    j = pl.program_id(1)
    pitch = 2 * n_heads
    z_ref[...] = _dot(h_ref[...], w_ref[...])

    @pl.when(j == 0)
    def _():
        glr_ref[...] = _dot(h_ref[...], wl_ref[...])

    @pl.when(j == k_tile)
    def _():
        kv16_ref[...] = z_ref[...].astype(BF16)
        for c in range(pitch):
            kr_ref[pl.ds(c, tm, stride=pitch), :] = z_ref[:, c * hd:(c + 1) * hd]

    @pl.when(j == v_tile)
    def _():
        kv16_ref[...] = z_ref[...].astype(BF16)
        for h in range(n_heads):
            for half in range(2):
                c = 2 * h + half
                vr_ref[pl.ds(n_heads * half + h, tm, stride=pitch), :] = z_ref[:, c * hd:(c + 1) * hd]


def _inproj(h, w, wl, *, n_heads, hd, col_k, col_v, tm=1024):
    m, d = h.shape
    n = w.shape[1]
    tn = n_heads * 2 * hd
    tm = min(tm, m)
    pitch = 2 * n_heads
    assert m % tm == 0 and n % tn == 0 and col_k % tn == 0 and col_v == col_k + tn and hd == LANE
    k_tile = col_k // tn
    est = (2 * tm * d * 2 + 2 * d * (tn + LANE) * 2 + 2 * tm * (tn + LANE) * 4
           + 2 * 2 * tm * pitch * hd * 4 + tm * tn * 4 + 2 * tm * tn * 2)
    return pl.pallas_call(
        functools.partial(_inproj_kernel, tm=tm, n_heads=n_heads, hd=hd, k_tile=k_tile,
                          v_tile=k_tile + 1),
        name="in_proj",
        grid=(m // tm, n // tn),
        in_specs=[
            pl.BlockSpec((tm, d), lambda i, j: (i, 0)),
            pl.BlockSpec((d, tn), lambda i, j: (0, j)),
            pl.BlockSpec((d, LANE), lambda i, j: (0, 0)),
        ],
        out_specs=[
            pl.BlockSpec((tm, tn), lambda i, j: (i, j)),
            pl.BlockSpec((tm, LANE), lambda i, j: (i, 0)),
            pl.BlockSpec((tm * pitch, hd), lambda i, j: (i, 0)),
            pl.BlockSpec((tm * pitch, hd), lambda i, j: (i, 0)),
            pl.BlockSpec((tm, tn), lambda i, j: (i, jnp.clip(j - k_tile, 0, 1))),
        ],
        out_shape=[jax.ShapeDtypeStruct((m, n), F32),
                   jax.ShapeDtypeStruct((m, LANE), F32),
                   jax.ShapeDtypeStruct((m * pitch, hd), F32),
                   jax.ShapeDtypeStruct((m * pitch, hd), F32),
                   jax.ShapeDtypeStruct((m, 2 * tn), BF16)],
        compiler_params=pltpu.CompilerParams(
            dimension_semantics=("parallel", "arbitrary"),
            vmem_limit_bytes=_vmem_limit(est)),
    )(h, w, wl)


def _outproj_kernel(x_ref, a_ref, b_ref, wa_ref, wb_ref, o_ref):
    o_ref[...] = x_ref[...] + _dot(a_ref[...], wa_ref[...]) + _dot(b_ref[...], wb_ref[...])


def _outproj(x, mix_a, mix_b, w, *, tm=512):
    m, d = x.shape
    ka, kb = mix_a.shape[1], mix_b.shape[1]
    assert m % tm == 0 and ka == kb and w.shape == (ka + kb, d)
    est = 2 * 2 * tm * d * 4 + 2 * tm * (ka + kb) * 2 + 2 * (ka + kb) * d * 2 + tm * d * 4
    return pl.pallas_call(
        _outproj_kernel,
        name="out_proj",
        grid=(m // tm,),
        in_specs=[
            pl.BlockSpec((tm, d), lambda i: (i, 0)),
            pl.BlockSpec((tm, ka), lambda i: (i, 0)),
            pl.BlockSpec((tm, kb), lambda i: (i, 0)),
            pl.BlockSpec((ka, d), lambda i: (0, 0)),
            pl.BlockSpec((kb, d), lambda i: (1, 0)),
        ],
        out_specs=pl.BlockSpec((tm, d), lambda i: (i, 0)),
        out_shape=jax.ShapeDtypeStruct((m, d), F32),
        compiler_params=pltpu.CompilerParams(
            dimension_semantics=("parallel",),
            vmem_limit_bytes=_vmem_limit(est)),
    )(x, mix_a, mix_b, w, w)


def _diff_lambda(lam_ref, lam_init):
    lp = lam_ref[...]
    e1 = jnp.exp(jnp.sum(lp[0:1] * lp[1:2], axis=-1, keepdims=True))
    e2 = jnp.exp(jnp.sum(lp[2:3] * lp[3:4], axis=-1, keepdims=True))
    return e1 - e2 + lam_init


def _alibi_slope(h, n_heads):
    slope = F32(0.0)
    for i in range(n_heads):
        slope = jnp.where(h == i, F32(2.0 ** (-8.0 * (i + 1) / n_heads)), slope)
    return slope


def _lane_fold(x, op):
    parts = [x[:, i * LANE:(i + 1) * LANE] for i in range(x.shape[1] // LANE)]
    return functools.reduce(op, parts)


def _attn_prompt_kernel(q_ref, k_ref, v_ref, lam_ref, dn_ref, o_ref, s_ref, m_ref, l_ref, acc_ref,
                        *, tq, tk, hd, n_heads, lam_init):
    h = pl.program_id(1)
    qi = pl.program_id(2)
    scale = hd ** -0.5 * LOG2_E
    slope = _alibi_slope(h, n_heads) * LOG2_E
    n_diag = tq // tk
    n_full = qi * n_diag
    q = q_ref[...]
    zero = jnp.zeros((tq, hd), F32)
    qbd = jnp.concatenate([jnp.concatenate([q[:, :hd], zero], axis=1),
                           jnp.concatenate([zero, q[:, hd:]], axis=1)], axis=0).astype(BF16)
    row = lax.broadcasted_iota(jnp.int32, (tq, tk), 0)
    col = lax.broadcasted_iota(jnp.int32, (tq, tk), 1)
    d0 = (row - col).astype(F32)

    m_ref[...] = jnp.full_like(m_ref, -jnp.inf)

    def scores(c, masked):
        ks = pl.multiple_of(c * tk, tk)
        kc = k_ref[pl.ds(ks, tk), :]
        off = jnp.full((1, 1), qi * tq - c * tk, jnp.int32).astype(F32)
        dist = d0 + off
        bias = slope * dist
        s = _dot_nt(qbd, kc) * scale - jnp.concatenate([bias, bias], axis=0)
        if masked:
            s = jnp.where(jnp.concatenate([dist, dist], axis=0) >= 0, s, -jnp.inf)
        s_ref[c] = s
        m_ref[...] = jnp.maximum(m_ref[...], _lane_fold(s, jnp.maximum))

    def scores_body(c, carry):
        scores(c, False)
        return carry

    lax.fori_loop(0, n_full, scores_body, 0)
    for j in range(n_diag):
        scores(n_full + j, True)

    m_row = jnp.max(m_ref[...], axis=-1, keepdims=True)
    m_ref[...] = jnp.broadcast_to(m_row, m_ref.shape)
    l_ref[...] = jnp.zeros_like(l_ref)
    acc_ref[...] = jnp.zeros_like(acc_ref)

    def pv_body(c, carry):
        ks = pl.multiple_of(c * tk, tk)
        mb = m_ref[...]
        p = jnp.exp2(s_ref[c] - jnp.concatenate([mb] * (tk // LANE), axis=1))
        l_ref[...] += _lane_fold(p, jnp.add)
        acc_ref[...] += _dot(p.astype(BF16), v_ref[pl.ds(ks, tk), :])
        return carry

    lax.fori_loop(0, n_full + n_diag, pv_body, 0)

    lam = _diff_lambda(lam_ref, lam_init)
    on = acc_ref[...] / jnp.sum(l_ref[...], axis=-1, keepdims=True)
    o = on[:tq] - lam * on[tq:]
    o_ref[...] = (_rms(o, dn_ref[...]) * (1.0 - lam_init)).astype(o_ref.dtype)


def _attn_both_kernel(*refs, n_pages, prompt, sample):
    pt_ref = refs[0]
    q_ref, k_ref, v_ref, lam_ref, dn_ref, z_ref = refs[1:7]
    pages = refs[7:7 + 2 * n_pages]
    op_ref, os_ref = refs[7 + 2 * n_pages:9 + 2 * n_pages]
    s_ref, m_ref, l_ref, acc_ref, q2_ref, kn_ref, vn_ref = refs[9 + 2 * n_pages:]
    _attn_sample_kernel(pt_ref, z_ref, *pages, lam_ref, dn_ref, os_ref, q2_ref, kn_ref, vn_ref, **sample)
    _attn_prompt_kernel(q_ref, k_ref, v_ref, lam_ref, dn_ref, op_ref, s_ref, m_ref, l_ref, acc_ref,
                        **prompt)


def _attention(z, kv16, z3, cache_k, cache_v, page_table, layer, lam_p, diff_norm, *, batch, seq,
               n_heads, hd, vd, col_q, col_k, col_v, lam_init, tq=512, tk=512):
    dec_b, dec, zw = z3.shape
    n_pages = page_table.shape[1]
    depth, n_pool, page = cache_k.shape[:3]
    nq = seq // tq
    cw = 2 * hd
    width = n_heads * cw
    prow = page * n_heads * 2
    rows = 2 * dec
    assert vd == 2 * hd and seq % tq == 0 and tq % tk == 0 and tk % LANE == 0 and col_q % cw == 0
    assert dec_b == batch * n_heads * nq
    kf = cache_k.reshape(depth * n_pool, prow, hd)
    vf = cache_v.reshape(depth * n_pool, page, n_heads, 2, hd).transpose(0, 1, 3, 2, 4).reshape(
        depth * n_pool, prow, hd)
    pt = (page_table + layer * n_pool).reshape(-1).astype(jnp.int32)

    def sample_idx(b, h, i):
        return (b * n_heads + h) * nq + i

    def page_spec(p):
        return pl.BlockSpec((1, prow, hd),
                            lambda b, h, i, pt_ref: (pt_ref[sample_idx(b, h, i) * n_pages + p], 0, 0))

    est = (2 * tq * cw * 4 + 2 * 2 * seq * cw * 2 + 2 * tq * vd * 2
           + (seq // tk) * 2 * tq * tk * 4 + 2 * 2 * tq * LANE * 4 + 2 * tq * vd * 4
           + 3 * 2 * tq * tk * 4
           + 2 * dec * zw * 4 + 2 * 2 * n_pages * page * width * 4 + 2 * dec * width * 2
           + rows * 2 * hd * 4 + 2 * page * width * 4)
    grid_spec = pltpu.PrefetchScalarGridSpec(
        num_scalar_prefetch=1,
        grid=(batch, n_heads, nq),
        in_specs=([pl.BlockSpec((tq, cw), lambda b, h, i, pt_ref: (b * nq + i, col_q // cw + h)),
                   pl.BlockSpec((seq, cw), lambda b, h, i, pt_ref: (b, h)),
                   pl.BlockSpec((seq, cw), lambda b, h, i, pt_ref: (b, n_heads + h)),
                   pl.BlockSpec((4, hd), lambda b, h, i, pt_ref: (0, 0)),
                   pl.BlockSpec((1, vd), lambda b, h, i, pt_ref: (0, 0)),
                   pl.BlockSpec((1, dec, zw), lambda b, h, i, pt_ref: (sample_idx(b, h, i), 0, 0))]
                  + [page_spec(p) for p in range(n_pages)]
                  + [page_spec(p) for p in range(n_pages)]),
        out_specs=[pl.BlockSpec((tq, vd), lambda b, h, i, pt_ref: (b * nq + i, h)),
                   pl.BlockSpec((1, dec, width), lambda b, h, i, pt_ref: (sample_idx(b, h, i), 0, 0))],
        scratch_shapes=[pltpu.VMEM((seq // tk, 2 * tq, tk), F32),
                        pltpu.VMEM((2 * tq, LANE), F32), pltpu.VMEM((2 * tq, LANE), F32),
                        pltpu.VMEM((2 * tq, vd), F32),
                        pltpu.VMEM((rows, 2 * hd), F32),
                        pltpu.VMEM((page, width), F32),
                        pltpu.VMEM((page, width), F32)],
    )
    return pl.pallas_call(
        functools.partial(
            _attn_both_kernel, n_pages=n_pages,
            prompt=dict(tq=tq, tk=tk, hd=hd, n_heads=n_heads, lam_init=lam_init),
            sample=dict(n_pages=n_pages, page=page, n_heads=n_heads, hd=hd, dec=dec, col_q=col_q,
                        col_k=col_k, col_v=col_v, lam_init=lam_init)),
        name="attention",
        grid_spec=grid_spec,
        out_shape=[jax.ShapeDtypeStruct((batch * seq, width), BF16),
                   jax.ShapeDtypeStruct((dec_b, dec, width), BF16)],
        compiler_params=pltpu.CompilerParams(
            dimension_semantics=("parallel", "parallel", "arbitrary"),
            vmem_limit_bytes=_vmem_limit(est)),
    )(pt, z, kv16, kv16, lam_p, diff_norm, z3, *([kf] * n_pages), *([vf] * n_pages))


def _attn_sample_kernel(*refs, n_pages, page, n_heads, hd, dec, col_q, col_k, col_v, lam_init):
    z_ref = refs[1]
    kp_refs = refs[2:2 + n_pages]
    vp_refs = refs[2 + n_pages:2 + 2 * n_pages]
    pitch = 2 * n_heads
    lam_ref, dn_ref, o_ref, q2_ref, kn_ref, vn_ref = refs[2 + 2 * n_pages:]
    cw = 2 * hd
    rows = 2 * dec
    scale = hd ** -0.5 * LOG2_E
    past = n_pages * page
    n_keys = past + page
    r_iota = lax.broadcasted_iota(jnp.int32, (rows, n_keys), 0)
    j_iota = lax.broadcasted_iota(jnp.int32, (rows, n_keys), 1)
    q_idx = jnp.where(r_iota >= dec, r_iota - dec, r_iota)
    dist = (past + q_idx - j_iota).astype(F32)
    valid = j_iota <= past + q_idx
    rr = lax.broadcasted_iota(jnp.int32, (rows, cw), 0)
    cc = lax.broadcasted_iota(jnp.int32, (rows, cw), 1)
    keep = (rr < dec) == (cc < hd)
    lam = _diff_lambda(lam_ref, lam_init)

    width = n_heads * cw
    kn_ref[...] = jnp.zeros_like(kn_ref)
    vn_ref[...] = jnp.zeros_like(vn_ref)
    kn_ref[0:dec, :] = z_ref[0, :, col_k:col_k + width]
    vn_ref[0:dec, :] = z_ref[0, :, col_v:col_v + width]

    for h in range(n_heads):
        slope = 2.0 ** (-8.0 * (h + 1) / n_heads) * LOG2_E
        qh = z_ref[0, :, col_q + h * cw: col_q + (h + 1) * cw]
        q2_ref[0:dec, :] = qh
        q2_ref[dec:rows, :] = qh
        qbd = jnp.where(keep, q2_ref[...], 0.0).astype(BF16)
        parts = []
        for kp in kp_refs:
            kcat = jnp.concatenate([kp[0, pl.ds(2 * h + m, page, stride=pitch), :] for m in range(2)],
                                   axis=-1)
            parts.append(_dot_nt(qbd, kcat.astype(BF16)))
        parts.append(_dot_nt(qbd, kn_ref[:, h * cw:(h + 1) * cw].astype(BF16)))
        s = jnp.concatenate(parts, axis=-1) * scale - slope * dist
        s = jnp.where(valid, s, -jnp.inf)
        p = jnp.exp2(s - jnp.max(s, axis=-1, keepdims=True))
        l = jnp.sum(p, axis=-1, keepdims=True)
        p16 = p.astype(BF16)
        acc = _dot(p16[:, past:], vn_ref[:, h * cw:(h + 1) * cw].astype(BF16))
        for i, vp in enumerate(vp_refs):
            vcat = jnp.concatenate(
                [vp[0, pl.ds(n_heads * half + h, page, stride=pitch), :] for half in range(2)], axis=-1)
            acc = acc + _dot(p16[:, i * page:(i + 1) * page], vcat.astype(BF16))
        on = acc / l
        o = on[:dec] - lam * on[dec:]
        o_ref[0, :, h * cw:(h + 1) * cw] = (_rms(o, dn_ref[...]) * (1.0 - lam_init)).astype(o_ref.dtype)


def _log_decay(glr, w2_ref, b_ref):
    x = _dot(glr.astype(BF16), w2_ref[...]) + b_ref[...]
    return (jnp.minimum(x, 0.0) - jnp.log(1.0 + jnp.exp(-jnp.abs(x)))) / GLA_TAU


def _split_bf16(x):
    hi = x.astype(BF16)
    r1 = x - hi.astype(F32)
    mid = r1.astype(BF16)
    lo = (r1 - mid.astype(F32)).astype(BF16)
    return hi, mid, lo


def _gla_out(o, gn_ref, gr):
    return _rms(o, gn_ref[...]) * (gr * jax.nn.sigmoid(gr))


GLA_MAX_CHUNK_DECAY = 60.0


def _gla_prompt_kernel(q_ref, k_ref, v_ref, gr_ref, glr_ref, w2_ref, b_ref, gn_ref,
                       o_ref, s_ref, st_ref, la_ref, of_ref, *, chunk, n_heads, dk, dv):
    c = pl.program_id(1)
    scale = dk ** -0.5

    @pl.when(c == 0)
    def _():
        st_ref[...] = jnp.zeros_like(st_ref)

    log_a = _log_decay(glr_ref[...], w2_ref, b_ref)
    row = lax.broadcasted_iota(jnp.int32, (chunk, chunk), 0)
    col = lax.broadcasted_iota(jnp.int32, (chunk, chunk), 1)
    causal = row >= col
    tri = jnp.where(causal, 1.0, 0.0).astype(BF16)
    hi, mid, lo = _split_bf16(log_a)
    cum_all = _dot(tri, hi) + _dot(tri, mid) + _dot(tri, lo)
    steep = jnp.max(-cum_all[chunk - 1:chunk, :]) > GLA_MAX_CHUNK_DECAY

    @pl.when(jnp.logical_not(steep))
    def _():
        for h in range(n_heads):
            cum = cum_all[:, h * dk:(h + 1) * dk]
            last = cum[chunk - 1:chunk, :]
            q = q_ref[:, h * dk:(h + 1) * dk] * scale
            k = k_ref[:, h * dk:(h + 1) * dk]
            v = v_ref[:, h * dv:(h + 1) * dv]
            st = st_ref[h]
            qt = (q * jnp.exp(cum)).astype(BF16)
            kt = (k * jnp.exp(-cum)).astype(BF16)
            att = jnp.where(causal, _dot_nt(qt, kt), 0.0)
            of_ref[:, h * dv:(h + 1) * dv] = (_dot(att.astype(BF16), v.astype(BF16))
                                              + _dot_nt(qt, st.astype(BF16)))
            kd = (k * jnp.exp(last - cum)).astype(BF16)
            st_ref[h] = st * jnp.exp(last) + _dot(v.T.astype(BF16), kd)

    @pl.when(steep)
    def _():
        la_ref[...] = log_a
        eye = (lax.broadcasted_iota(jnp.int32, (dv, dv), 0)
               == lax.broadcasted_iota(jnp.int32, (dv, dv), 1))

        sub = lax.broadcasted_iota(jnp.int32, (8, dv), 0)

        def tokens(g, carry):
            r0 = pl.multiple_of(g * 8, 8)
            a8 = jnp.exp(la_ref[pl.ds(r0, 8), :])
            q8 = q_ref[pl.ds(r0, 8), :] * scale
            k8 = k_ref[pl.ds(r0, 8), :]
            v8 = v_ref[pl.ds(r0, 8), :]
            for h in range(n_heads):
                st = st_ref[h]
                o8 = jnp.zeros((8, dv), F32)
                for r in range(8):
                    v_t = v8[r:r + 1, h * dv:(h + 1) * dv]
                    v_col = jnp.sum(jnp.where(eye, jnp.broadcast_to(v_t, (dv, dv)), 0.0),
                                    axis=-1, keepdims=True)
                    st = st * a8[r:r + 1, h * dk:(h + 1) * dk] + v_col * k8[r:r + 1, h * dk:(h + 1) * dk]
                    o_col = jnp.sum(st * q8[r:r + 1, h * dk:(h + 1) * dk], axis=-1, keepdims=True)
                    o_row = jnp.sum(jnp.where(eye, jnp.broadcast_to(o_col, (dv, dv)), 0.0),
                                    axis=0, keepdims=True)
                    o8 = jnp.where(sub == r, o_row, o8)
                st_ref[h] = st
                of_ref[pl.ds(r0, 8), h * dv:(h + 1) * dv] = o8
            return carry

        lax.fori_loop(0, chunk // 8, tokens, 0)

    for h in range(n_heads):
        o_ref[:, h * dv:(h + 1) * dv] = _gla_out(
            of_ref[:, h * dv:(h + 1) * dv], gn_ref, gr_ref[:, h * dv:(h + 1) * dv]).astype(o_ref.dtype)

    @pl.when(c == pl.num_programs(1) - 1)
    def _():
        for h in range(n_heads):
            s_ref[0, h] = st_ref[h].T


def _gla_prompt(z, glr, w2p, bias, gla_norm, *, batch, seq, n_heads, dk, dv, col_q, col_k, col_v,
                col_gr, chunk=256):
    assert seq % chunk == 0
    nc = seq // chunk
    wk, wv = n_heads * dk, n_heads * dv
    assert col_q % wk == 0 and col_k % wk == 0 and col_v % wv == 0 and col_gr % wv == 0
    bq, bk, bv, bg = col_q // wk, col_k // wk, col_v // wv, col_gr // wv
    est = (2 * chunk * (2 * wk + 2 * wv + LANE) * 4 + 2 * chunk * wv * 2 + 3 * n_heads * dk * dv * 4
           + chunk * (wk + wv) * 4 + 4 * chunk * wk * 4 + 8 * chunk * chunk * 4 + 12 * chunk * dv * 4)
    return pl.pallas_call(
        functools.partial(_gla_prompt_kernel, chunk=chunk, n_heads=n_heads, dk=dk, dv=dv),
        name="gla_prompt",
        grid=(batch, nc),
        in_specs=[
            pl.BlockSpec((chunk, wk), lambda b, c: (b * nc + c, bq)),
            pl.BlockSpec((chunk, wk), lambda b, c: (b * nc + c, bk)),
            pl.BlockSpec((chunk, wv), lambda b, c: (b * nc + c, bv)),
            pl.BlockSpec((chunk, wv), lambda b, c: (b * nc + c, bg)),
            pl.BlockSpec((chunk, LANE), lambda b, c: (b * nc + c, 0)),
            pl.BlockSpec((LANE, wk), lambda b, c: (0, 0)),
            pl.BlockSpec((1, wk), lambda b, c: (0, 0)),
            pl.BlockSpec((1, dv), lambda b, c: (0, 0)),
        ],
        out_specs=[
            pl.BlockSpec((chunk, wv), lambda b, c: (b * nc + c, 0)),
            pl.BlockSpec((1, n_heads, dk, dv), lambda b, c: (b, 0, 0, 0)),
        ],
        out_shape=[jax.ShapeDtypeStruct((batch * seq, wv), BF16),
                   jax.ShapeDtypeStruct((batch, n_heads, dk, dv), F32)],
        scratch_shapes=[pltpu.VMEM((n_heads, dv, dk), F32),
                        pltpu.VMEM((chunk, wk), F32),
                        pltpu.VMEM((chunk, wv), F32)],
        compiler_params=pltpu.CompilerParams(
            dimension_semantics=("parallel", "arbitrary"),
            vmem_limit_bytes=_vmem_limit(est)),
    )(z, z, z, z, glr, w2p, bias, gla_norm)


def _gla_sample_kernel(z_ref, glr_ref, s0_ref, w2_ref, b_ref, gn_ref, o_ref, s_ref,
                       *, n_seq, n_heads, dk, dv, dec, col_q, col_k, col_v, col_gr):
    for s in range(n_seq):
        _gla_sample_one(z_ref.at[s], glr_ref.at[s], s0_ref.at[s], w2_ref, b_ref, gn_ref,
                        o_ref.at[s], s_ref.at[s], n_heads=n_heads, dk=dk, dv=dv, dec=dec,
                        col_q=col_q, col_k=col_k, col_v=col_v, col_gr=col_gr)


def _gla_sample_one(z_ref, glr_ref, s0_ref, w2_ref, b_ref, gn_ref, o_ref, s_ref,
                    *, n_heads, dk, dv, dec, col_q, col_k, col_v, col_gr):
    log_a_all = _log_decay(glr_ref[...], w2_ref, b_ref)
    t_k = lax.broadcasted_iota(jnp.int32, (dec, dk), 0)
    t_v = lax.broadcasted_iota(jnp.int32, (dec, dv), 0)
    eye = (lax.broadcasted_iota(jnp.int32, (dk, dk), 0)
           == lax.broadcasted_iota(jnp.int32, (dk, dk), 1))

    def to_col(r):
        return jnp.sum(jnp.where(eye, jnp.broadcast_to(r, (dk, dk)), 0.0), axis=-1, keepdims=True)

    for h in range(n_heads):
        log_a = log_a_all[:, h * dk:(h + 1) * dk]
        cum = jnp.zeros((dec, dk), F32)
        for t in range(dec):
            cum = cum + jnp.where(t_k >= t, log_a[t:t + 1], 0.0)
        last = cum[dec - 1:dec]
        q = z_ref[:, col_q + h * dk: col_q + (h + 1) * dk] * (dk ** -0.5)
        k = z_ref[:, col_k + h * dk: col_k + (h + 1) * dk]
        v = z_ref[:, col_v + h * dv: col_v + (h + 1) * dv]
        gr = z_ref[:, col_gr + h * dv: col_gr + (h + 1) * dv]
        s0 = s0_ref[h]

        o = _dot((q * jnp.exp(cum)).astype(BF16), s0.astype(BF16))
        for t in range(dec):
            o_t = jnp.zeros((1, dv), F32)
            for j in range(t + 1):
                w = jnp.exp(cum[t:t + 1] - cum[j:j + 1])
                a = jnp.sum(q[t:t + 1] * k[j:j + 1] * w, axis=-1, keepdims=True)
                o_t = o_t + a * v[j:j + 1]
            o = o + jnp.where(t_v == t, o_t, 0.0)

        kd = (k * jnp.exp(last - cum)).astype(BF16)
        upd = lax.dot_general(kd, v.astype(BF16), (((0,), (0,)), ((), ())),
                              preferred_element_type=F32)
        s_ref[h] = s0 * to_col(jnp.exp(last)) + upd
        o_ref[:, h * dv:(h + 1) * dv] = _gla_out(o, gn_ref, gr).astype(o_ref.dtype)


def _gla_sample(z3, glr3, state, w2p, bias, gla_norm, *, n_heads, dk, dv, col_q, col_k, col_v,
                col_gr, n_seq=8):
    dec_b, dec, zw = z3.shape
    assert dec_b % n_seq == 0
    est = n_seq * (2 * dec * zw * 4 + 2 * 2 * n_heads * dk * dv * 4) + LANE * n_heads * dk * 2 \
        + 8 * dk * dv * 4
    return pl.pallas_call(
        functools.partial(_gla_sample_kernel, n_seq=n_seq, n_heads=n_heads, dk=dk, dv=dv, dec=dec,
                          col_q=col_q, col_k=col_k, col_v=col_v, col_gr=col_gr),
        name="gla_sample",
        grid=(dec_b // n_seq,),
        in_specs=[
            pl.BlockSpec((n_seq, dec, zw), lambda b: (b, 0, 0)),
            pl.BlockSpec((n_seq, dec, LANE), lambda b: (b, 0, 0)),
            pl.BlockSpec((n_seq, n_heads, dk, dv), lambda b: (b, 0, 0, 0)),
            pl.BlockSpec((LANE, n_heads * dk), lambda b: (0, 0)),
            pl.BlockSpec((1, n_heads * dk), lambda b: (0, 0)),
            pl.BlockSpec((1, dv), lambda b: (0, 0)),
        ],
        out_specs=[
            pl.BlockSpec((n_seq, dec, n_heads * dv), lambda b: (b, 0, 0)),
            pl.BlockSpec((n_seq, n_heads, dk, dv), lambda b: (b, 0, 0, 0)),
        ],
        out_shape=[jax.ShapeDtypeStruct((dec_b, dec, n_heads * dv), BF16),
                   jax.ShapeDtypeStruct((dec_b, n_heads, dk, dv), F32)],
        compiler_params=pltpu.CompilerParams(
            dimension_semantics=("parallel",),
            vmem_limit_bytes=_vmem_limit(est)),
    )(z3, glr3, state, w2p, bias, gla_norm)


def _round_up(x, m):
    return (x + m - 1) // m * m


def kernel(x_prompt, x_sample, cache_k, cache_v, state_gla, page_table, norm_ffn1, ffn1_w_gate, ffn1_w_up, ffn1_w_down, norm_mix, w_in, gla_gate_w2, gla_gate_b, lambda_q1, lambda_k1, lambda_q2, lambda_k2, diff_norm, gla_norm, w_out, norm_ffn2, ffn2_w_gate, ffn2_w_up, ffn2_w_down, norm_final):
    batch, seq, d = x_prompt.shape
    dec_b, dec, _ = x_sample.shape
    depth = norm_ffn1.shape[0]
    n_heads, hd = cache_k.shape[3], cache_k.shape[5]
    vd = cache_v.shape[4]
    g_heads, dk, dv = state_gla.shape[2:]
    rank = gla_gate_w2.shape[1]
    d_ff = ffn1_w_gate.shape[2]

    w_qk = n_heads * 2 * hd
    sizes = (w_qk, w_qk, n_heads * vd, g_heads * dk, g_heads * dk, g_heads * dv, g_heads * dv, rank)
    cols = [0]
    for s in sizes:
        cols.append(cols[-1] + s)
    col_dq, col_dk, col_dv, col_gq, col_gk, col_gv, col_gr, col_glr = cols[:8]
    assert w_in.shape[2] == cols[8] and rank <= LANE
    ff_pad = _round_up(d_ff, FFN_TF)

    yp = x_prompt.reshape(batch * seq, d)
    ys = x_sample.reshape(dec_b * dec, d)
    nf = norm_final.reshape(1, d)
    outs = [[] for _ in range(6)]

    for layer in range(depth):
        lam_init = _lambda_init(layer)
        ffn_w = []
        for wg, wu, wd in ((ffn1_w_gate, ffn1_w_up, ffn1_w_down), (ffn2_w_gate, ffn2_w_up, ffn2_w_down)):
            ffn_w.append((_to_bf16_tiles(wg, layer, ff_pad, FFN_TF), _to_bf16_tiles(wu, layer, ff_pad, FFN_TF),
                          _to_bf16(wd, layer, ff_pad, d)))
        w_main, w_glr = _w_in_prep(w_in, layer, col_glr)
        w2p = jnp.pad(gla_gate_w2[layer], ((0, LANE - rank), (0, 0))).astype(BF16)
        gate_b = gla_gate_b[layer].reshape(1, -1)
        lam_p = jnp.stack([lambda_q1[layer], lambda_k1[layer], lambda_q2[layer], lambda_k2[layer]])
        dn = diff_norm[layer].reshape(1, vd)
        gn = gla_norm[layer].reshape(1, dv)
        wo = _to_bf16(w_out, layer, w_out.shape[1], d)
        g1 = norm_ffn1[layer].reshape(1, d)
        gm = norm_mix[layer].reshape(1, d)
        g2 = norm_ffn2[layer].reshape(1, d)
        last = layer == depth - 1

        def pre(x):
            x, hmix = _ffn(x, g1, *ffn_w[0], gm, tail="next")
            return (x,) + tuple(_inproj(hmix, w_main, w_glr, n_heads=n_heads, hd=hd, col_k=col_dk,
                                        col_v=col_dv))

        def post(x, mix_a, mix_b):
            x = _outproj(x, mix_a, mix_b, wo)
            return _ffn(x, g2, *ffn_w[1], nf, tail="final" if last else "none")

        def kv_out(k_rows, v_rows, b, l):
            k = k_rows.reshape(b, l, n_heads, 2, hd)
            v = v_rows.reshape(b, l, 2, n_heads, hd).transpose(0, 1, 3, 2, 4).reshape(b, l, n_heads, vd)
            return k, v

        yp, zp, glr_p, k_rows, v_rows, kv16_p = pre(yp)
        ys, zs, glr_s, k_rows_s, v_rows_s, _ = pre(ys)
        zs3 = zs.reshape(dec_b, dec, col_glr)

        mix_a, mix_a_s = _attention(zp, kv16_p, zs3, cache_k, cache_v, page_table, layer, lam_p, dn,
                                    batch=batch, seq=seq, n_heads=n_heads, hd=hd, vd=vd, col_q=col_dq,
                                    col_k=col_dk, col_v=col_dv, lam_init=lam_init)

        mix_b, s_p = _gla_prompt(zp, glr_p, w2p, gate_b, gn, batch=batch, seq=seq, n_heads=g_heads,
                                 dk=dk, dv=dv, col_q=col_gq, col_k=col_gk, col_v=col_gv, col_gr=col_gr)
        yp = post(yp, mix_a, mix_b)
        k_new, v_new = kv_out(k_rows, v_rows, batch, seq)
        outs[0].append(k_new)
        outs[1].append(v_new)
        outs[2].append(s_p.astype(state_gla.dtype))

        mix_b, s_s = _gla_sample(zs3, glr_s.reshape(dec_b, dec, LANE), state_gla[layer], w2p, gate_b, gn,
                                 n_heads=g_heads, dk=dk, dv=dv, col_q=col_gq, col_k=col_gk,
                                 col_v=col_gv, col_gr=col_gr)
        ys = post(ys, mix_a_s.reshape(dec_b * dec, -1), mix_b.reshape(dec_b * dec, -1))
        k_new, v_new = kv_out(k_rows_s, v_rows_s, dec_b, dec)
        outs[3].append(k_new)
        outs[4].append(v_new)
        outs[5].append(s_s.astype(state_gla.dtype))

    if depth == 0:
        raise ValueError("depth must be positive")
    y_prompt = yp.reshape(batch, seq, d)
    y_sample = ys.reshape(dec_b, dec, d)
    return (y_prompt, y_sample) + tuple(jnp.stack(o) for o in outs)
```
